```python
import math
import jax
import jax.numpy as jnp
from jax import lax
import numpy as np

D_MODEL = 1024
BATCH = 4
SEQ = 4096
DEPTH = 4

GRID_W = 64
CTX_LEN = 256

MIXERS = ('pool', 'attn', 'ssm', 'gmlp')
N_MIXERS = len(MIXERS)
CTX_READING_MIXERS = ('attn', 'ssm')

DEEPNORM_ALPHA = (2.0 * DEPTH) ** 0.25
DEEPNORM_BETA = (8.0 * DEPTH) ** -0.25
LN_EPS = 1e-5
N_MODS = 6

POOL_WINDOWS = (2, 4, 8, 16)
N_POOL_GROUPS = len(POOL_WINDOWS)
POOL_GROUP = D_MODEL // N_POOL_GROUPS

HEAD_DIM = 64
N_Q_HEADS = D_MODEL // HEAD_DIM
N_KV_HEADS = N_Q_HEADS // 4
GQA_GROUP = N_Q_HEADS // N_KV_HEADS
Q_WIDTH = N_Q_HEADS * HEAD_DIM
KV_WIDTH = N_KV_HEADS * HEAD_DIM
WINDOW = 128
ATTN_BLOCK = 128
ROPE_BASE = 10000.0
NEG_INF = -1e30

SSM_GROUP = 16
SSM_N_GROUPS = D_MODEL // SSM_GROUP
SSM_STATE = 64
DT_MIN = 1e-3
DT_MAX = 1e-1

GMLP_CHUNK = 128
GMLP_HALF = 2 * D_MODEL
GMLP_HEADS = 8
GMLP_HEAD_DIM = GMLP_HALF // GMLP_HEADS

FFN_HIDDEN = 2816
CONV_WIDTH = 3

kernel_name = 'hybrid_interleaved_diffusion_trunk'


def layer_norm(x, g, b):
    xf = x.astype(jnp.float32)
    mu = jnp.mean(xf, axis=-1, keepdims=True)
    var = jnp.mean(jnp.square(xf - mu), axis=-1, keepdims=True)
    y = (xf - mu) * lax.rsqrt(var + LN_EPS) * g.astype(jnp.float32) + b.astype(jnp.float32)
    return y.astype(x.dtype)


def modulate(x, shift, scale):
    return x * (1.0 + scale) + shift


def ada_modulations(cond, w, b):
    return jnp.split(jax.nn.silu(cond) @ w + b, N_MODS, axis=-1)


def post_norm_residual(x, y, gate, g, b):
    return layer_norm(DEEPNORM_ALPHA * x + gate * y, g, b)


def pool_mixer(h, w, b, scale):
    bsz, n, _ = h.shape
    hf = h.astype(jnp.float32)
    csum = jnp.concatenate([jnp.zeros_like(hf[:, :1]), lax.cumsum(hf, axis=1)], axis=1)
    csum = csum.reshape(bsz, n + 1, N_POOL_GROUPS, POOL_GROUP)
    pos = jnp.arange(n)[:, None]
    win = jnp.array(POOL_WINDOWS)[None, :]
    lo = jnp.clip(pos - win // 2, 0, n)
    hi = jnp.clip(pos - win // 2 + win, 0, n)
    grp = jnp.arange(N_POOL_GROUPS)[None, :]
    mean = (csum[:, hi, grp] - csum[:, lo, grp]) / (hi - lo).astype(jnp.float32)[None, :, :, None]
    mixed = mean - hf.reshape(bsz, n, N_POOL_GROUPS, POOL_GROUP)
    y = jnp.einsum('bngc,gcd->bngd', mixed.astype(h.dtype), w) + b.reshape(N_POOL_GROUPS, POOL_GROUP)
    return y.reshape(bsz, n, D_MODEL) * scale


def rope_1d(x, pos):
    half = x.shape[-1] // 2
    freqs = ROPE_BASE ** (-jnp.arange(half, dtype=jnp.float32) / half)
    ang = pos.astype(jnp.float32)[:, None] * freqs[None, :]
    cos = jnp.cos(ang)[None, :, None, :]
    sin = jnp.sin(ang)[None, :, None, :]
    xf = x.astype(jnp.float32)
    x1, x2 = xf[..., :half], xf[..., half:]
    return jnp.concatenate([x1 * cos - x2 * sin, x1 * sin + x2 * cos], axis=-1).astype(x.dtype)


def axial_rope(x, row_pos, col_pos):
    half = HEAD_DIM // 2
    return jnp.concatenate([rope_1d(x[..., :half], row_pos), rope_1d(x[..., half:], col_pos)], axis=-1)


def banded_attention(q, k, v, k_ctx, v_ctx, sink):
    bsz, s_len = q.shape[:2]
    nb = s_len // ATTN_BLOCK
    scale = HEAD_DIM ** -0.5
    qb = q.reshape(bsz, nb, ATTN_BLOCK, N_KV_HEADS, GQA_GROUP, HEAD_DIM)
    pad = ((0, 0), (ATTN_BLOCK, ATTN_BLOCK), (0, 0), (0, 0))

    def band(t):
        tp = jnp.pad(t, pad).reshape(bsz, nb + 2, ATTN_BLOCK, N_KV_HEADS, HEAD_DIM)
        return jnp.concatenate([tp[:, :-2], tp[:, 1:-1], tp[:, 2:]], axis=2)

    kb, vb = band(k), band(v)
    s_loc = jnp.einsum('bnqhgd,bnkhd->bnhgqk', qb, kb).astype(jnp.float32) * scale
    q_pos = jnp.arange(nb)[:, None] * ATTN_BLOCK + jnp.arange(ATTN_BLOCK)[None, :]
    k_pos = (jnp.arange(nb)[:, None] - 1) * ATTN_BLOCK + jnp.arange(3 * ATTN_BLOCK)[None, :]
    rel = k_pos[:, None, :] - q_pos[:, :, None]
    valid = (jnp.abs(rel) <= WINDOW) & (k_pos[:, None, :] >= 0) & (k_pos[:, None, :] < s_len)
    s_loc = jnp.where(valid[None, :, None, None], s_loc, NEG_INF)
    s_ctx = jnp.einsum('bnqhgd,bkhd->bnhgqk', qb, k_ctx).astype(jnp.float32) * scale
    s_sink = jnp.broadcast_to(sink[None, None, :, :, None, None], s_loc.shape[:-1] + (1,))
    p = jax.nn.softmax(jnp.concatenate([s_loc, s_ctx, s_sink], axis=-1), axis=-1).astype(v.dtype)
    n_loc = 3 * ATTN_BLOCK
    n_ctx = k_ctx.shape[1]
    o = (jnp.einsum('bnhgqk,bnkhd->bnqhgd', p[..., :n_loc], vb)
         + jnp.einsum('bnhgqk,bkhd->bnqhgd', p[..., n_loc:n_loc + n_ctx], v_ctx))
    return o.reshape(bsz, s_len, Q_WIDTH)


def context_attention(q, k, v, sink):
    s = jnp.einsum('bqhgd,bkhd->bhgqk', q, k).astype(jnp.float32) * HEAD_DIM ** -0.5
    s_sink = jnp.broadcast_to(sink[None, :, :, None, None], s.shape[:-1] + (1,))
    p = jax.nn.softmax(jnp.concatenate([s, s_sink], axis=-1), axis=-1).astype(v.dtype)
    o = jnp.einsum('bhgqk,bkhd->bqhgd', p[..., :-1], v)
    return o.reshape(q.shape[0], q.shape[1], Q_WIDTH)


def attn_mixer(h_lat, h_ctx, w_qkv, w_o, sink, row_pos, col_pos, need_ctx_out):
    bsz, s_len, _ = h_lat.shape
    n_ctx = h_ctx.shape[1]
    qkv = h_lat @ w_qkv
    q = axial_rope(qkv[..., :Q_WIDTH].reshape(bsz, s_len, N_Q_HEADS, HEAD_DIM), row_pos, col_pos)
    k = axial_rope(qkv[..., Q_WIDTH:Q_WIDTH + KV_WIDTH].reshape(bsz, s_len, N_KV_HEADS, HEAD_DIM), row_pos, col_pos)
    v = qkv[..., Q_WIDTH + KV_WIDTH:].reshape(bsz, s_len, N_KV_HEADS, HEAD_DIM)
    kv_ctx = h_ctx @ w_qkv[:, Q_WIDTH:]
    k_ctx = kv_ctx[..., :KV_WIDTH].reshape(bsz, n_ctx, N_KV_HEADS, HEAD_DIM)
    v_ctx = kv_ctx[..., KV_WIDTH:].reshape(bsz, n_ctx, N_KV_HEADS, HEAD_DIM)
    sink_logit = sink.astype(jnp.float32).reshape(N_KV_HEADS, GQA_GROUP)
    q = q.reshape(bsz, s_len, N_KV_HEADS, GQA_GROUP, HEAD_DIM)
    y_lat = banded_attention(q, k, v, k_ctx, v_ctx, sink_logit) @ w_o
    y_ctx = None
    if need_ctx_out:
        q_ctx = (h_ctx @ w_qkv[:, :Q_WIDTH]).reshape(bsz, n_ctx, N_KV_HEADS, GQA_GROUP, HEAD_DIM)
        y_ctx = context_attention(q_ctx, k_ctx, v_ctx, sink_logit) @ w_o
    return y_lat, y_ctx


def s5_discretise(lam_re, lam_im, log_dt, b_re, b_im):
    lam = lax.complex(lam_re.astype(jnp.float32), lam_im.astype(jnp.float32))
    dt = jnp.exp(log_dt.astype(jnp.float32))[:, None]
    lam_bar = jnp.exp(lam * dt)
    b = lax.complex(b_re.astype(jnp.float32), b_im.astype(jnp.float32))
    b_bar = ((lam_bar - 1.0) / lam)[..., None] * b
    return lam_bar, b_bar


def _linear_recurrence(e1, e2):
    a1, b1 = e1
    a2, b2 = e2
    return a1 * a2, a2 * b1 + b2


def s5_scan(u, lam_bar, b_bar, s0, reverse):
    bu = jnp.einsum('bngc,gpc->bngp', u.astype(jnp.complex64), b_bar)
    if s0 is not None:
        edge = u.shape[1] - 1 if reverse else 0
        bu = bu.at[:, edge].add(lam_bar[None] * s0)
    a = jnp.broadcast_to(lam_bar, (1, u.shape[1]) + lam_bar.shape)
    _, states = lax.associative_scan(_linear_recurrence, (a, bu), reverse=reverse, axis=1)
    return states


def ssm_mixer(h_lat, h_ctx, lam_re, lam_im, log_dt, b_re, b_im, c_re, c_im, d_skip, w_a, w_b, need_ctx_out):
    def groups(h):
        return h.astype(jnp.float32).reshape(h.shape[0], h.shape[1], SSM_N_GROUPS, SSM_GROUP)

    def readout(states, c_mat):
        y = jnp.real(jnp.einsum('bngp,gcp->bngc', states, c_mat))
        return y.reshape(states.shape[0], states.shape[1], D_MODEL)

    def glu(y, dtype):
        g = jax.nn.gelu(y).astype(dtype)
        return (g @ w_a) * jax.nn.sigmoid(g @ w_b)

    u_lat, u_ctx = groups(h_lat), groups(h_ctx)
    d32 = d_skip.astype(jnp.float32)
    y_lat = d32 * h_lat.astype(jnp.float32)
    y_ctx = d32 * h_ctx.astype(jnp.float32) if need_ctx_out else None
    for direction, reverse in enumerate((False, True)):
        lam_bar, b_bar = s5_discretise(lam_re[direction], lam_im[direction], log_dt[direction],
                                       b_re[direction], b_im[direction])
        c_mat = lax.complex(c_re[direction].astype(jnp.float32), c_im[direction].astype(jnp.float32))
        ctx_states = s5_scan(u_ctx, lam_bar, b_bar, None, reverse)
        ctx_final = ctx_states[:, 0] if reverse else ctx_states[:, -1]
        y_lat = y_lat + readout(s5_scan(u_lat, lam_bar, b_bar, ctx_final, reverse), c_mat)
        if need_ctx_out:
            y_ctx = y_ctx + readout(ctx_states, c_mat)
    out_ctx = glu(y_ctx, h_ctx.dtype) if need_ctx_out else None
    return glu(y_lat, h_lat.dtype), out_ctx


def gmlp_mixer(h, w_in, b_in, ln_g, ln_b, w_s, b_s, w_out):
    bsz, n, _ = h.shape
    z = jax.nn.gelu(h @ w_in + b_in)
    u = z[..., :GMLP_HALF]
    v = layer_norm(z[..., GMLP_HALF:], ln_g, ln_b)
    vc = v.reshape(bsz, n // GMLP_CHUNK, GMLP_CHUNK, GMLP_HEADS, GMLP_HEAD_DIM)
    gate = jnp.einsum('hpq,bnqhc->bnphc', w_s, vc) + b_s.T[None, None, :, :, None]
    return (u * gate.reshape(bsz, n, GMLP_HALF)) @ w_out


def conv_ffn(h, w_up, conv_w, conv_b, w_down):
    n = h.shape[1]
    a = h @ w_up
    pad = CONV_WIDTH // 2
    ap = jnp.pad(a, ((0, 0), (pad, pad), (0, 0)))
    a = conv_b + ap[:, 0:n] * conv_w[0]
    for tap in range(1, CONV_WIDTH):
        a = a + ap[:, tap:tap + n] * conv_w[tap]
    val, gate = a[..., :FFN_HIDDEN], a[..., FFN_HIDDEN:]
    return (val * jax.nn.silu(gate)) @ w_down


def _normal(key, shape, std):
    return std * jax.random.normal(key, shape, dtype=jnp.float32)


def _n_layers_of(kind):
    return len(range(MIXERS.index(kind), DEPTH, N_MIXERS))


def setup_inputs(seed: int = 0) -> dict:
    key = jax.random.key(seed)
    keys = iter(jax.random.split(key, 48))
    D = D_MODEL
    n_pool, n_attn, n_ssm, n_gmlp = (_n_layers_of(k) for k in MIXERS)
    qkv_width = Q_WIDTH + 2 * KV_WIDTH
    G, P = SSM_N_GROUPS, SSM_STATE
    return {
        'x': _normal(next(keys), (BATCH, SEQ, D), 1.0),
        'c': _normal(next(keys), (BATCH, D), 1.0),
        'ctx': _normal(next(keys), (BATCH, CTX_LEN, D), 1.0),
        'c_ctx': _normal(next(keys), (D,), 1.0),
        'ada_w': _normal(next(keys), (DEPTH, D, N_MODS * D), 0.5 * D ** -0.5),
        'ada_b': _normal(next(keys), (DEPTH, N_MODS * D), 0.02),
        'ln1_g': 1.0 + _normal(next(keys), (DEPTH, D), 0.02),
        'ln1_b': _normal(next(keys), (DEPTH, D), 0.02),
        'ln2_g': 1.0 + _normal(next(keys), (DEPTH, D), 0.02),
        'ln2_b': _normal(next(keys), (DEPTH, D), 0.02),
        'ffn_w_up': _normal(next(keys), (DEPTH, D, 2 * FFN_HIDDEN), D ** -0.5),
        'ffn_conv_w': _normal(next(keys), (DEPTH, CONV_WIDTH, 2 * FFN_HIDDEN), CONV_WIDTH ** -0.5),
        'ffn_conv_b': _normal(next(keys), (DEPTH, 2 * FFN_HIDDEN), 0.02),
        'ffn_w_down': _normal(next(keys), (DEPTH, FFN_HIDDEN, D), FFN_HIDDEN ** -0.5 * DEEPNORM_BETA),
        'pool_w': _normal(next(keys), (n_pool, N_POOL_GROUPS, POOL_GROUP, POOL_GROUP), POOL_GROUP ** -0.5 * DEEPNORM_BETA),
        'pool_b': _normal(next(keys), (n_pool, D), 0.02),
        'pool_scale': 1.0 + _normal(next(keys), (n_pool, D), 0.02),
        'attn_w_qkv': _normal(next(keys), (n_attn, D, qkv_width), D ** -0.5),
        'attn_w_o': _normal(next(keys), (n_attn, Q_WIDTH, D), Q_WIDTH ** -0.5 * DEEPNORM_BETA),
        'attn_sink': _normal(next(keys), (n_attn, N_Q_HEADS), 0.5),
        'ssm_lambda_re': -0.5 + _normal(next(keys), (n_ssm, 2, G, P), 1e-3),
        'ssm_lambda_im': jnp.pi * jnp.arange(P, dtype=jnp.float32) + _normal(next(keys), (n_ssm, 2, G, P), 1e-3),
        'ssm_log_dt': jax.random.uniform(next(keys), (n_ssm, 2, G), dtype=jnp.float32,
                                         minval=math.log(DT_MIN), maxval=math.log(DT_MAX)),
        'ssm_b_re': _normal(next(keys), (n_ssm, 2, G, P, SSM_GROUP), (2 * SSM_GROUP) ** -0.5),
        'ssm_b_im': _normal(next(keys), (n_ssm, 2, G, P, SSM_GROUP), (2 * SSM_GROUP) ** -0.5),
        'ssm_c_re': _normal(next(keys), (n_ssm, 2, G, SSM_GROUP, P), (2 * P) ** -0.5),
        'ssm_c_im': _normal(next(keys), (n_ssm, 2, G, SSM_GROUP, P), (2 * P) ** -0.5),
        'ssm_d': _normal(next(keys), (n_ssm, D), 1.0),
        'ssm_w_glu_a': _normal(next(keys), (n_ssm, D, D), D ** -0.5 * DEEPNORM_BETA),
        'ssm_w_glu_b': _normal(next(keys), (n_ssm, D, D), D ** -0.5),
        'gmlp_w_in': _normal(next(keys), (n_gmlp, D, 2 * GMLP_HALF), D ** -0.5),
        'gmlp_b_in': _normal(next(keys), (n_gmlp, 2 * GMLP_HALF), 0.02),
        'gmlp_ln_g': 1.0 + _normal(next(keys), (n_gmlp, GMLP_HALF), 0.02),
        'gmlp_ln_b': _normal(next(keys), (n_gmlp, GMLP_HALF), 0.02),
        'gmlp_w_s': _normal(next(keys), (n_gmlp, GMLP_HEADS, GMLP_CHUNK, GMLP_CHUNK), GMLP_CHUNK ** -0.5),
        'gmlp_b_s': 1.0 + _normal(next(keys), (n_gmlp, GMLP_HEADS, GMLP_CHUNK), 0.02),
        'gmlp_w_out': _normal(next(keys), (n_gmlp, GMLP_HALF, D), GMLP_HALF ** -0.5 * DEEPNORM_BETA),
    }


def reference(x, c, ctx, c_ctx, ada_w, ada_b, ln1_g, ln1_b, ln2_g, ln2_b,
              ffn_w_up, ffn_conv_w, ffn_conv_b, ffn_w_down,
              pool_w, pool_b, pool_scale,
              attn_w_qkv, attn_w_o, attn_sink,
              ssm_lambda_re, ssm_lambda_im, ssm_log_dt, ssm_b_re, ssm_b_im, ssm_c_re, ssm_c_im,
              ssm_d, ssm_w_glu_a, ssm_w_glu_b,
              gmlp_w_in, gmlp_b_in, gmlp_ln_g, gmlp_ln_b, gmlp_w_s, gmlp_b_s, gmlp_w_out):
    seq_len = x.shape[1]
    ROWS = seq_len // GRID_W
    row_pos = jnp.repeat(jnp.arange(ROWS), GRID_W)
    col_pos = jnp.tile(jnp.arange(GRID_W), ROWS)
    x_lat, x_ctx = x, ctx
    for layer in range(DEPTH):
        kind = MIXERS[layer % N_MIXERS]
        j = layer // N_MIXERS
        ctx_out = any(MIXERS[m % N_MIXERS] in CTX_READING_MIXERS for m in range(layer + 1, DEPTH))
        ctx_in = ctx_out or kind in CTX_READING_MIXERS
        sh1, sc1, gt1, sh2, sc2, gt2 = [m[:, None, :] for m in ada_modulations(c, ada_w[layer], ada_b[layer])]
        h_lat = modulate(x_lat, sh1, sc1)
        h_ctx = None
        if ctx_in:
            csh1, csc1, cgt1, csh2, csc2, cgt2 = ada_modulations(c_ctx, ada_w[layer], ada_b[layer])
            h_ctx = modulate(x_ctx, csh1, csc1)
        if kind == 'pool':
            y_lat = pool_mixer(h_lat, pool_w[j], pool_b[j], pool_scale[j])
            y_ctx = pool_mixer(h_ctx, pool_w[j], pool_b[j], pool_scale[j]) if ctx_out else None
        elif kind == 'attn':
            y_lat, y_ctx = attn_mixer(h_lat, h_ctx, attn_w_qkv[j], attn_w_o[j], attn_sink[j],
                                      row_pos, col_pos, ctx_out)
        elif kind == 'ssm':
            y_lat, y_ctx = ssm_mixer(h_lat, h_ctx, ssm_lambda_re[j], ssm_lambda_im[j], ssm_log_dt[j],
                                     ssm_b_re[j], ssm_b_im[j], ssm_c_re[j], ssm_c_im[j], ssm_d[j],
                                     ssm_w_glu_a[j], ssm_w_glu_b[j], ctx_out)
        else:
            y_lat = gmlp_mixer(h_lat, gmlp_w_in[j], gmlp_b_in[j], gmlp_ln_g[j], gmlp_ln_b[j],
                               gmlp_w_s[j], gmlp_b_s[j], gmlp_w_out[j])
            y_ctx = (gmlp_mixer(h_ctx, gmlp_w_in[j], gmlp_b_in[j], gmlp_ln_g[j], gmlp_ln_b[j],
                                gmlp_w_s[j], gmlp_b_s[j], gmlp_w_out[j]) if ctx_out else None)
        x_lat = post_norm_residual(x_lat, y_lat, gt1, ln1_g[layer], ln1_b[layer])
        f_lat = conv_ffn(modulate(x_lat, sh2, sc2), ffn_w_up[layer], ffn_conv_w[layer],
                         ffn_conv_b[layer], ffn_w_down[layer])
        x_lat = post_norm_residual(x_lat, f_lat, gt2, ln2_g[layer], ln2_b[layer])
        if ctx_out:
            x_ctx = post_norm_residual(x_ctx, y_ctx, cgt1, ln1_g[layer], ln1_b[layer])
            f_ctx = conv_ffn(modulate(x_ctx, csh2, csc2), ffn_w_up[layer], ffn_conv_w[layer],
                             ffn_conv_b[layer], ffn_w_down[layer])
            x_ctx = post_norm_residual(x_ctx, f_ctx, cgt2, ln2_g[layer], ln2_b[layer])
    return x_lat
```

```python
import functools
import math

import jax
import jax.numpy as jnp
from jax import lax
from jax.experimental import pallas as pl
from jax.experimental.pallas import tpu as pltpu

F32 = jnp.float32
BF16 = jnp.bfloat16

D_MODEL = 1024
DEPTH = 4
MIXERS = ("pool", "attn", "ssm", "gmlp")
CTX_READING_MIXERS = ("attn", "ssm")
GRID_W = 64
N_MODS = 6
DEEPNORM_ALPHA = (2.0 * DEPTH) ** 0.25
LN_EPS = 1e-5

POOL_WINDOWS = (2, 4, 8, 16)
POOL_GROUP = D_MODEL // len(POOL_WINDOWS)

HEAD_DIM = 64
N_Q_HEADS = D_MODEL // HEAD_DIM
N_KV_HEADS = N_Q_HEADS // 4
Q_WIDTH = N_Q_HEADS * HEAD_DIM
KV_WIDTH = N_KV_HEADS * HEAD_DIM
WINDOW = 128
ATTN_BLOCK = 128
ROPE_BASE = 10000.0
NEG_INF = -1e30

SSM_GROUP = 16
SSM_N_GROUPS = D_MODEL // SSM_GROUP
SSM_STATE = 64
SSM_CHUNK = 16

GMLP_CHUNK = 128
GMLP_HALF = 2 * D_MODEL
GMLP_HEADS = 8
GMLP_HEAD_DIM = GMLP_HALF // GMLP_HEADS

FFN_HIDDEN = 2816

LANES = 128
HALO = 16
VMEM_LIMIT = 56 * 1024 * 1024


def _cparams(*sem):
    return pltpu.CompilerParams(dimension_semantics=sem, vmem_limit_bytes=VMEM_LIMIT)


def _const_spec(shape):
    nd = len(shape)
    return pl.BlockSpec(shape, lambda *_: (0,) * nd, pipeline_mode=pl.Buffered(1))


def _post_norm(x, y, gate, g, b):
    z = DEEPNORM_ALPHA * x + gate * y
    mu = jnp.mean(z, axis=-1, keepdims=True)
    zc = z - mu
    var = jnp.mean(zc * zc, axis=-1, keepdims=True)
    return zc * lax.rsqrt(var + LN_EPS) * g + b


def _dot(a, b):
    return jnp.dot(a, b, preferred_element_type=F32)


def _dot_nt(a, b):
    return lax.dot_general(a, b, (((1,), (1,)), ((), ())), preferred_element_type=F32)


def _ada_kernel(c_ref, w_ref, b_ref, o_ref):
    c = c_ref[...]
    s = (c * jax.nn.sigmoid(c)).astype(BF16)
    o_ref[0] = _dot(s, w_ref[0].astype(BF16)) + b_ref[0]


def _ada(cond, ada_w, ada_b):
    depth, d, n = ada_w.shape
    rows = cond.shape[0]
    tn = 1536
    return pl.pallas_call(
        _ada_kernel,
        out_shape=jax.ShapeDtypeStruct((depth, rows, n), F32),
        grid=(depth, n // tn),
        in_specs=[pl.BlockSpec((rows, d), lambda l, j: (0, 0)),
                  pl.BlockSpec((1, d, tn), lambda l, j: (l, 0, j)),
                  pl.BlockSpec((1, 1, tn), lambda l, j: (l, 0, j))],
        out_specs=pl.BlockSpec((1, rows, tn), lambda l, j: (l, 0, j)),
        compiler_params=_cparams("arbitrary", "arbitrary"),
        name="ada",
    )(cond, ada_w, ada_b.reshape(depth, 1, n))


def _halo_specs(t, n):
    per = t // HALO
    last = n // HALO - 1
    prev = pl.BlockSpec((1, HALO, D_MODEL), lambda b, i, *_: (b, jnp.maximum(i * per - 1, 0), 0))
    nxt = pl.BlockSpec((1, HALO, D_MODEL), lambda b, i, *_: (b, jnp.minimum((i + 1) * per, last), 0))
    return prev, nxt


def _ffn_kernel(xp_ref, x_ref, xn_ref, mod_ref, wv_ref, wg_ref, cwv_ref, cwg_ref, cbv_ref, cbg_ref,
                wd_ref, g_ref, b_ref, o_ref, h_ref, acc_ref, *, t, nt, nk):
    i = pl.program_id(1)
    k = pl.program_id(2)
    rows = t + 2 * HALO

    @pl.when(k == 0)
    def _():
        m = mod_ref[0]
        sh, sc = m[3:4], 1.0 + m[4:5]
        keep_p = jnp.where(i > 0, 1.0, 0.0)
        keep_n = jnp.where(i < nt - 1, 1.0, 0.0)
        h_ref[0:HALO, :] = ((xp_ref[0] * sc + sh) * keep_p).astype(BF16)
        h_ref[HALO:HALO + t, :] = (x_ref[0] * sc + sh).astype(BF16)
        h_ref[HALO + t:rows, :] = ((xn_ref[0] * sc + sh) * keep_n).astype(BF16)
        acc_ref[...] = jnp.zeros_like(acc_ref)

    h = h_ref[...]

    def conv(w_ref, cw_ref, cb_ref):
        u = _dot(h, w_ref[...])
        cw = cw_ref[...]
        a = cb_ref[...] + pltpu.roll(u, 1, 0) * cw[0:1]
        a = a + u * cw[1:2]
        a = a + pltpu.roll(u, rows - 1, 0) * cw[2:3]
        return a[HALO:HALO + t]

    val = conv(wv_ref, cwv_ref, cbv_ref)
    gate = conv(wg_ref, cwg_ref, cbg_ref)
    act = (val * (gate * jax.nn.sigmoid(gate))).astype(BF16)
    acc_ref[...] += _dot(act, wd_ref[...])

    @pl.when(k == nk - 1)
    def _():
        m = mod_ref[0]
        o_ref[0] = _post_norm(x_ref[0], acc_ref[...], m[5:6], g_ref[...], b_ref[...])


def _ffn(x, mods, w_up, conv_w, conv_b, w_down, ln_g, ln_b, *, t, fc):
    bsz, n, d = x.shape
    f = w_down.shape[0]
    nt, nk = n // t, f // fc
    prev, nxt = _halo_specs(t, n)
    kern = functools.partial(_ffn_kernel, t=t, nt=nt, nk=nk)
    return pl.pallas_call(
        kern,
        out_shape=jax.ShapeDtypeStruct((bsz, n, d), F32),
        grid=(bsz, nt, nk),
        in_specs=[prev,
                  pl.BlockSpec((1, t, d), lambda b, i, k: (b, i, 0)),
                  nxt,
                  pl.BlockSpec((1, N_MODS, d), lambda b, i, k: (b, 0, 0)),
                  pl.BlockSpec((d, fc), lambda b, i, k: (0, k)),
                  pl.BlockSpec((d, fc), lambda b, i, k: (0, nk + k)),
                  pl.BlockSpec((3, fc), lambda b, i, k: (0, k)),
                  pl.BlockSpec((3, fc), lambda b, i, k: (0, nk + k)),
                  pl.BlockSpec((1, fc), lambda b, i, k: (0, k)),
                  pl.BlockSpec((1, fc), lambda b, i, k: (0, nk + k)),
                  pl.BlockSpec((fc, d), lambda b, i, k: (k, 0)),
                  pl.BlockSpec((1, d), lambda b, i, k: (0, 0)),
                  pl.BlockSpec((1, d), lambda b, i, k: (0, 0))],
        out_specs=pl.BlockSpec((1, t, d), lambda b, i, k: (b, i, 0)),
        scratch_shapes=[pltpu.VMEM((t + 2 * HALO, d), BF16), pltpu.VMEM((t, d), F32)],
        compiler_params=_cparams("arbitrary", "arbitrary", "arbitrary"),
        name="conv_ffn",
    )(x, x, x, mods, w_up, w_up, conv_w, conv_w, conv_b, conv_b, w_down, ln_g, ln_b)


def _pool_kernel(xp_ref, x_ref, xn_ref, mod_ref, w_ref, pb_ref, ps_ref, g_ref, b_ref, o_ref, *, t, nt, n):
    i = pl.program_id(1)
    rows = t + 2 * HALO
    m = mod_ref[0]
    sh, sc = m[0:1], 1.0 + m[1:2]
    keep_p = jnp.where(i > 0, 1.0, 0.0)
    keep_n = jnp.where(i < nt - 1, 1.0, 0.0)
    x = x_ref[0]
    h = jnp.concatenate([(xp_ref[0] * sc + sh) * keep_p, x * sc + sh, (xn_ref[0] * sc + sh) * keep_n], axis=0)
    pos = i * t + lax.broadcasted_iota(jnp.int32, (t, 1), 0)
    outs = []
    for gi, win in enumerate(POOL_WINDOWS):
        hg = h[:, gi * POOL_GROUP:(gi + 1) * POOL_GROUP]
        s = hg + pltpu.roll(hg, 1, 0)
        half = 1
        while 2 * half < win:
            s = pltpu.roll(s, half, 0) + pltpu.roll(s, rows - half, 0)
            half *= 2
        lo = jnp.maximum(pos - win // 2, 0)
        hi = jnp.minimum(pos - win // 2 + win, n)
        mean = s[HALO:HALO + t] / (hi - lo).astype(F32)
        mixed = (mean - hg[HALO:HALO + t]).astype(BF16)
        outs.append(_dot(mixed, w_ref[gi]))
    y = (jnp.concatenate(outs, axis=1) + pb_ref[...]) * ps_ref[...]
    o_ref[0] = _post_norm(x, y, m[2:3], g_ref[...], b_ref[...])


def _pool(x, mods, w, pb, ps, ln_g, ln_b, *, t):
    bsz, n, d = x.shape
    nt = n // t
    prev, nxt = _halo_specs(t, n)
    kern = functools.partial(_pool_kernel, t=t, nt=nt, n=n)
    vec = pl.BlockSpec((1, d), lambda b, i: (0, 0))
    return pl.pallas_call(
        kern,
        out_shape=jax.ShapeDtypeStruct((bsz, n, d), F32),
        grid=(bsz, nt),
        in_specs=[prev, pl.BlockSpec((1, t, d), lambda b, i: (b, i, 0)), nxt,
                  pl.BlockSpec((1, N_MODS, d), lambda b, i: (b, 0, 0)),
                  pl.BlockSpec(w.shape, lambda b, i: (0, 0, 0)),
                  vec, vec, vec, vec],
        out_specs=pl.BlockSpec((1, t, d), lambda b, i: (b, i, 0)),
        compiler_params=_cparams("arbitrary", "arbitrary"),
        name="pool_mixer",
    )(x, x, x, mods, w, pb, ps, ln_g, ln_b)


def _rope_tables(n):
    tpos = jnp.arange(n)
    half = HEAD_DIM // 4
    freqs = ROPE_BASE ** (-jnp.arange(half, dtype=F32) / half)
    ang_r = (tpos // GRID_W).astype(F32)[:, None] * freqs[None, :]
    ang_c = (tpos % GRID_W).astype(F32)[:, None] * freqs[None, :]
    zero = jnp.zeros_like(ang_r)
    cr, sr, cc, sc = jnp.cos(ang_r), jnp.sin(ang_r), jnp.cos(ang_c), jnp.sin(ang_c)
    cos = jnp.tile(jnp.concatenate([cr, cr, cc, cc], axis=1), (1, 2))
    sin_first = jnp.tile(jnp.concatenate([-sr, zero, -sc, zero], axis=1), (1, 2))
    sin_second = jnp.tile(jnp.concatenate([zero, sr, zero, sc], axis=1), (1, 2))
    return cos, sin_first, sin_second


def _dup_heads(chunk, lo):
    sw = pltpu.roll(chunk, HEAD_DIM, 1)
    return jnp.where(lo, chunk, sw), jnp.where(lo, sw, chunk)


def _qkv_kernel(*refs, rope):
    if rope:
        x_ref, mod_ref, w_ref, cos_ref, sa_ref, sb_ref, q_ref, kd_ref, vd_ref = refs
    else:
        x_ref, mod_ref, w_ref, q_ref, kd_ref, vd_ref = refs
    m = mod_ref[0]
    h = (x_ref[0] * (1.0 + m[1:2]) + m[0:1]).astype(BF16)
    qkv = _dot(h, w_ref[...])
    t = qkv.shape[0]
    lo = lax.broadcasted_iota(jnp.int32, (t, LANES), 1) < HEAD_DIM
    scale = HEAD_DIM ** -0.5

    def rot(v):
        if not rope:
            return v
        quarter = HEAD_DIM // 4
        return (v * cos_ref[...] + pltpu.roll(v, LANES - quarter, 1) * sa_ref[...]
                + pltpu.roll(v, quarter, 1) * sb_ref[...])

    for c in range(Q_WIDTH // LANES):
        q_ref[0, :, c * LANES:(c + 1) * LANES] = (rot(qkv[:, c * LANES:(c + 1) * LANES]) * scale).astype(BF16)
    for c in range(KV_WIDTH // LANES):
        k0, k1 = _dup_heads(rot(qkv[:, Q_WIDTH + c * LANES:Q_WIDTH + (c + 1) * LANES]), lo)
        kd_ref[0, :, (2 * c) * LANES:(2 * c + 1) * LANES] = k0.astype(BF16)
        kd_ref[0, :, (2 * c + 1) * LANES:(2 * c + 2) * LANES] = k1.astype(BF16)
        v0, v1 = _dup_heads(qkv[:, Q_WIDTH + KV_WIDTH + c * LANES:Q_WIDTH + KV_WIDTH + (c + 1) * LANES], lo)
        vd_ref[0, :, (2 * c) * LANES:(2 * c + 1) * LANES] = v0.astype(BF16)
        vd_ref[0, :, (2 * c + 1) * LANES:(2 * c + 2) * LANES] = v1.astype(BF16)


def _qkv(x, mods, w_qkv, *, t, rope):
    bsz, n, d = x.shape
    kdw = N_KV_HEADS * LANES
    ins = [x, mods, w_qkv]
    specs = [pl.BlockSpec((1, t, d), lambda b, i: (b, i, 0)),
             pl.BlockSpec((1, N_MODS, d), lambda b, i: (b, 0, 0)),
             pl.BlockSpec(w_qkv.shape, lambda b, i: (0, 0))]
    if rope:
        ins += list(_rope_tables(n))
        specs += [pl.BlockSpec((t, LANES), lambda b, i: (i, 0))] * 3
    return pl.pallas_call(
        functools.partial(_qkv_kernel, rope=rope),
        out_shape=(jax.ShapeDtypeStruct((bsz, n, Q_WIDTH), BF16),
                   jax.ShapeDtypeStruct((bsz, n, kdw), BF16),
                   jax.ShapeDtypeStruct((bsz, n, kdw), BF16)),
        grid=(bsz, n // t),
        in_specs=specs,
        out_specs=(pl.BlockSpec((1, t, Q_WIDTH), lambda b, i: (b, i, 0)),
                   pl.BlockSpec((1, t, kdw), lambda b, i: (b, i, 0)),
                   pl.BlockSpec((1, t, kdw), lambda b, i: (b, i, 0))),
        compiler_params=_cparams("arbitrary", "arbitrary"),
        name="qkv_rope" if rope else "qkv_ctx",
    )(*ins)


def _attn_kernel(sink_ref, q_ref, *refs, nb, local):
    if local:
        kp_ref, kc_ref, kn_ref, kx_ref, vp_ref, vc_ref, vn_ref, vx_ref, o_ref = refs
        k_refs, v_refs = (kp_ref, kc_ref, kn_ref, kx_ref), (vp_ref, vc_ref, vn_ref, vx_ref)
    else:
        kx_ref, vx_ref, o_ref = refs
        k_refs, v_refs = (kx_ref,), (vx_ref,)
    blk = ATTN_BLOCK
    group = N_Q_HEADS // N_KV_HEADS
    rows = group * blk
    n_keys = sum(r.shape[1] for r in k_refs)
    nblk = pl.program_id(1)
    if local:
        qi = lax.broadcasted_iota(jnp.int32, (rows, n_keys), 0) % blk
        kj = lax.broadcasted_iota(jnp.int32, (rows, n_keys), 1)
        rel = kj - blk - qi
        valid = (jnp.abs(rel) <= WINDOW) & ((kj >= blk) | (nblk > 0)) & ((kj < 2 * blk) | (nblk < nb - 1))
        valid = valid | (kj >= 3 * blk)
    lo = lax.broadcasted_iota(jnp.int32, (blk, LANES), 1) < HEAD_DIM
    head_of_row = lax.broadcasted_iota(jnp.int32, (rows, 1), 0) // blk
    for hk in range(N_KV_HEADS):
        kx = jnp.concatenate([r[0, :, hk * LANES:(hk + 1) * LANES] for r in k_refs], axis=0)
        vx = jnp.concatenate([r[0, :, hk * LANES:(hk + 1) * LANES] for r in v_refs], axis=0)
        parts = []
        for c in range(group // 2):
            qc = q_ref[0, :, (hk * group // 2 + c) * LANES:(hk * group // 2 + c + 1) * LANES]
            zero = jnp.zeros_like(qc)
            parts += [jnp.where(lo, qc, zero), jnp.where(lo, zero, qc)]
        q4 = jnp.concatenate(parts, axis=0)
        s = _dot_nt(q4, kx)
        if local:
            s = jnp.where(valid, s, NEG_INF)
        sink = jnp.zeros((rows, 1), F32)
        for g in range(group):
            sink = jnp.where(head_of_row == g, sink_ref[hk * group + g], sink)
        mx = jnp.maximum(jnp.max(s, axis=-1, keepdims=True), sink)
        p = jnp.exp(s - mx)
        den = jnp.sum(p, axis=-1, keepdims=True) + jnp.exp(sink - mx)
        o4 = _dot(p.astype(BF16), vx) / den
        for c in range(group // 2):
            oc = jnp.where(lo, o4[(2 * c) * blk:(2 * c + 1) * blk], o4[(2 * c + 1) * blk:(2 * c + 2) * blk])
            o_ref[0, :, (hk * group // 2 + c) * LANES:(hk * group // 2 + c + 1) * LANES] = oc.astype(BF16)


def _attn(sink, q, kd, vd, kd_ctx, vd_ctx, *, local):
    bsz, n, _ = q.shape
    blk = ATTN_BLOCK
    nb = n // blk
    lctx = kd_ctx.shape[1]
    kdw = kd_ctx.shape[2]
    smem = pl.BlockSpec(memory_space=pltpu.SMEM)
    qspec = pl.BlockSpec((1, blk, Q_WIDTH), lambda b, i: (b, i, 0))
    ctx_spec = pl.BlockSpec((1, lctx, kdw), lambda b, i: (b, 0, 0))
    if local:
        band = [pl.BlockSpec((1, blk, kdw), lambda b, i: (b, jnp.maximum(i - 1, 0), 0)),
                pl.BlockSpec((1, blk, kdw), lambda b, i: (b, i, 0)),
                pl.BlockSpec((1, blk, kdw), lambda b, i: (b, jnp.minimum(i + 1, nb - 1), 0))]
        specs = [smem, qspec] + band + [ctx_spec] + band + [ctx_spec]
        args = (sink, q, kd, kd, kd, kd_ctx, vd, vd, vd, vd_ctx)
    else:
        specs = [smem, qspec, ctx_spec, ctx_spec]
        args = (sink, q, kd_ctx, vd_ctx)
    return pl.pallas_call(
        functools.partial(_attn_kernel, nb=nb, local=local),
        out_shape=jax.ShapeDtypeStruct((bsz, n, Q_WIDTH), BF16),
        grid=(bsz, nb),
        in_specs=specs,
        out_specs=pl.BlockSpec((1, blk, Q_WIDTH), lambda b, i: (b, i, 0)),
        compiler_params=_cparams("arbitrary", "arbitrary"),
        name="banded_attn" if local else "ctx_attn",
    )(*args)


def _proj_norm_kernel(a_ref, x_ref, mod_ref, w_ref, g_ref, b_ref, o_ref):
    y = _dot(a_ref[0], w_ref[...])
    o_ref[0] = _post_norm(x_ref[0], y, mod_ref[0][2:3], g_ref[...], b_ref[...])


def _proj_norm(a, x, mods, w, ln_g, ln_b, *, t):
    bsz, n, d = x.shape
    ka = a.shape[2]
    vec = pl.BlockSpec((1, d), lambda b, i: (0, 0))
    return pl.pallas_call(
        _proj_norm_kernel,
        out_shape=jax.ShapeDtypeStruct((bsz, n, d), F32),
        grid=(bsz, n // t),
        in_specs=[pl.BlockSpec((1, t, ka), lambda b, i: (b, i, 0)),
                  pl.BlockSpec((1, t, d), lambda b, i: (b, i, 0)),
                  pl.BlockSpec((1, N_MODS, d), lambda b, i: (b, 0, 0)),
                  pl.BlockSpec(w.shape, lambda b, i: (0, 0)),
                  vec, vec],
        out_specs=pl.BlockSpec((1, t, d), lambda b, i: (b, i, 0)),
        compiler_params=_cparams("arbitrary", "arbitrary"),
        name="attn_out_norm",
    )(a, x, mods, w, ln_g, ln_b)


def _ssm_prep_kernel(lr_ref, li_ref, ldt_ref, btr_ref, bti_ref, cr_ref, ci_ref, wb_ref, toep_ref, wc_ref, a_ref):
    c = SSM_CHUNK
    p = SSM_STATE
    g = pl.program_id(0)
    wb_cols, wc_cols, toep, a_cols = [], [], None, []
    lag = lax.broadcasted_iota(jnp.int32, (c + 1, p), 0).astype(F32)
    lane = lax.broadcasted_iota(jnp.int32, (SSM_GROUP, c * SSM_GROUP), 1)
    for d in range(2):
        lr, li = lr_ref[0, d], li_ref[0, d]
        dt = jnp.exp(jnp.full((1, p), ldt_ref[g, d], F32))
        mag = jnp.exp(lag * (lr * dt))
        ang = lag * (li * dt)
        pw_r, pw_i = mag * jnp.cos(ang), mag * jnp.sin(ang)
        lbr, lbi = pw_r[1:2], pw_i[1:2]
        den = lr * lr + li * li
        qr = ((lbr - 1.0) * lr + lbi * li) / den
        qi = (lbi * lr - (lbr - 1.0) * li) / den
        btr, bti = btr_ref[0, d], bti_ref[0, d]
        bbr = qr * btr - qi * bti
        bbi = qr * bti + qi * btr
        cr, ci = cr_ref[0, d], ci_ref[0, d]

        def cl(j):
            return jnp.concatenate([cr * pw_r[j:j + 1] - ci * pw_i[j:j + 1],
                                    -(cr * pw_i[j:j + 1] + ci * pw_r[j:j + 1])], axis=1)

        def bl(j):
            return bbr * pw_r[j:j + 1] - bbi * pw_i[j:j + 1], bbr * pw_i[j:j + 1] + bbi * pw_r[j:j + 1]

        lags = list(range(c)) if d == 0 else list(range(c - 1, -1, -1))
        e = jnp.concatenate([cl(j) for j in lags], axis=0)
        bb = jnp.concatenate([bbr, bbi], axis=1)
        kt = lax.dot_general(bb, e, (((1,), (1,)), ((), ())), preferred_element_type=F32,
                             precision=lax.Precision.HIGHEST)
        blocks = []
        for k in range(c):
            if d == 0:
                sh = SSM_GROUP * k
                blk = kt if k == 0 else pltpu.roll(kt, sh, 1)
                blk = jnp.where(lane >= sh, blk, 0.0)
            else:
                sh = SSM_GROUP * (c - 1 - k)
                blk = kt if sh == 0 else pltpu.roll(kt, c * SSM_GROUP - sh, 1)
                blk = jnp.where(lane < SSM_GROUP * (k + 1), blk, 0.0)
            blocks.append(blk)
        td = jnp.concatenate(blocks, axis=0)
        toep = td if toep is None else toep + td
        wbl = [bl(c - 1 - k) if d == 0 else bl(k) for k in range(c)]
        wb_cols.append((jnp.concatenate([w[0] for w in wbl], axis=0), jnp.concatenate([w[1] for w in wbl], axis=0)))
        wc_cols.append(jnp.concatenate([cl(k + 1) if d == 0 else cl(c - k) for k in range(c)], axis=0))
        a_cols.append((pw_r[c:c + 1], pw_i[c:c + 1]))
    wb_ref[0] = jnp.concatenate([wb_cols[0][0], wb_cols[1][0], wb_cols[0][1], wb_cols[1][1]], axis=1).astype(BF16)
    toep_ref[0] = toep.astype(BF16)
    wc_ref[0] = jnp.concatenate([wc_cols[0][:, :p], wc_cols[1][:, :p], wc_cols[0][:, p:], wc_cols[1][:, p:]],
                                axis=1).astype(BF16)
    a_ref[0] = jnp.concatenate([a_cols[0][0], a_cols[1][0], a_cols[0][1], a_cols[1][1]], axis=1)


def _ssm_prep(lam_re, lam_im, log_dt, b_re, b_im, c_re, c_im):
    g, p, cw = SSM_N_GROUPS, SSM_STATE, SSM_CHUNK * SSM_GROUP
    per_g = lambda a: jnp.swapaxes(a, 0, 1)
    lam_spec = pl.BlockSpec((1, 2, 1, p), lambda i: (i, 0, 0, 0))
    mat_spec = pl.BlockSpec((1, 2, SSM_GROUP, p), lambda i: (i, 0, 0, 0))
    out_spec = pl.BlockSpec((1, cw, cw), lambda i: (i, 0, 0))
    return pl.pallas_call(
        _ssm_prep_kernel,
        out_shape=(jax.ShapeDtypeStruct((g, cw, 4 * p), BF16), jax.ShapeDtypeStruct((g, cw, cw), BF16),
                   jax.ShapeDtypeStruct((g, cw, 4 * p), BF16), jax.ShapeDtypeStruct((g, 1, 4 * p), F32)),
        grid=(g,),
        in_specs=[lam_spec, lam_spec, pl.BlockSpec(memory_space=pltpu.SMEM), mat_spec, mat_spec, mat_spec, mat_spec],
        out_specs=(out_spec, out_spec, out_spec, pl.BlockSpec((1, 1, 4 * p), lambda i: (i, 0, 0))),
        compiler_params=_cparams("arbitrary"),
        name="ssm_prep",
    )(per_g(lam_re)[:, :, None, :], per_g(lam_im)[:, :, None, :], per_g(log_dt),
      jnp.swapaxes(per_g(b_re), 2, 3), jnp.swapaxes(per_g(b_im), 2, 3), per_g(c_re), per_g(c_im))


def _ssm_modulate(x_ref, sc_ref, sh_ref, seq_rows):
    parts, start = [], 0
    for s, nrows in enumerate(seq_rows):
        parts.append(x_ref[0, start:start + nrows, :] * sc_ref[0, s:s + 1, :] + sh_ref[0, s:s + 1, :])
        start += nrows
    return jnp.concatenate(parts, axis=0)


def _ssm_state_kernel(x_ref, sc_ref, sh_ref, wb_ref, o_ref, *, seq_rows):
    u = _ssm_modulate(x_ref, sc_ref, sh_ref, seq_rows)
    o_ref[0] = _dot(u.astype(BF16), wb_ref[0])


def _ssm_states(xg, scale, shift, wb, *, seq_rows):
    g, r, cw = xg.shape
    ns = scale.shape[1]
    return pl.pallas_call(
        functools.partial(_ssm_state_kernel, seq_rows=seq_rows),
        out_shape=jax.ShapeDtypeStruct((g, r, cw), F32),
        grid=(g,),
        in_specs=[pl.BlockSpec((1, r, cw), lambda i: (i, 0, 0)),
                  pl.BlockSpec((1, ns, cw), lambda i: (i, 0, 0)),
                  pl.BlockSpec((1, ns, cw), lambda i: (i, 0, 0)),
                  pl.BlockSpec((1, cw, cw), lambda i: (i, 0, 0))],
        out_specs=pl.BlockSpec((1, r, cw), lambda i: (i, 0, 0)),
        compiler_params=_cparams("arbitrary"),
        name="ssm_chunk_states",
    )(xg, scale, shift, wb)


def _ssm_scan_kernel(x_ref, a_ref, o_ref):
    steps = x_ref.shape[0]
    half = a_ref.shape[1] // 2
    ar, ai = a_ref[:, :half], a_ref[:, half:]

    def body(i, carry):
        sr, si = carry
        o_ref[i, :, :half] = sr
        o_ref[i, :, half:] = si
        xr, xi = x_ref[i, :, :half], x_ref[i, :, half:]
        return ar * sr - ai * si + xr, ar * si + ai * sr + xi

    zero = jnp.zeros(ar.shape, F32)
    lax.fori_loop(0, steps, body, (zero, zero))


def _ssm_scan(x_time, a_rows, *, rb):
    steps, rows, w = x_time.shape
    return pl.pallas_call(
        _ssm_scan_kernel,
        out_shape=jax.ShapeDtypeStruct((steps, rows, w), F32),
        grid=(rows // rb,),
        in_specs=[pl.BlockSpec((steps, rb, w), lambda i: (0, i, 0)),
                  pl.BlockSpec((rb, w), lambda i: (i, 0))],
        out_specs=pl.BlockSpec((steps, rb, w), lambda i: (0, i, 0)),
        compiler_params=_cparams("arbitrary"),
        name="ssm_chunk_scan",
    )(x_time, a_rows)


def _ssm_out_kernel(x_ref, sc_ref, sh_ref, sin_ref, toep_ref, wc_ref, d_ref, o_ref, *, seq_rows):
    u = _ssm_modulate(x_ref, sc_ref, sh_ref, seq_rows)
    y = _dot(u.astype(BF16), toep_ref[0]) + _dot_nt(sin_ref[0], wc_ref[0])
    o_ref[0] = y + d_ref[0] * u


def _ssm_out(xg, scale, shift, s_in, toep, wc, d_tiled, *, seq_rows):
    g, r, cw = xg.shape
    ns = scale.shape[1]
    blk = lambda rows: pl.BlockSpec((1, rows, cw), lambda i: (i, 0, 0))
    return pl.pallas_call(
        functools.partial(_ssm_out_kernel, seq_rows=seq_rows),
        out_shape=jax.ShapeDtypeStruct((g, r, cw), F32),
        grid=(g,),
        in_specs=[blk(r), blk(ns), blk(ns), blk(r), blk(cw), blk(cw), blk(1)],
        out_specs=blk(r),
        compiler_params=_cparams("arbitrary"),
        name="ssm_chunk_out",
    )(xg, scale, shift, s_in, toep, wc, d_tiled)


def _glu_norm_kernel(y_ref, x_ref, mod_ref, wa_ref, wb_ref, g_ref, b_ref, o_ref):
    gl = jax.nn.gelu(y_ref[0]).astype(BF16)
    out = _dot(gl, wa_ref[...]) * jax.nn.sigmoid(_dot(gl, wb_ref[...]))
    o_ref[0] = _post_norm(x_ref[0], out, mod_ref[0][2:3], g_ref[...], b_ref[...])


def _glu_norm(y, x, mods, wa, wb, ln_g, ln_b, *, t):
    bsz, n, d = x.shape
    tile = pl.BlockSpec((1, t, d), lambda b, i: (b, i, 0))
    vec = pl.BlockSpec((1, d), lambda b, i: (0, 0))
    wspec = pl.BlockSpec((d, d), lambda b, i: (0, 0))
    return pl.pallas_call(
        _glu_norm_kernel,
        out_shape=jax.ShapeDtypeStruct((bsz, n, d), F32),
        grid=(bsz, n // t),
        in_specs=[tile, tile, pl.BlockSpec((1, N_MODS, d), lambda b, i: (b, 0, 0)), wspec, wspec, vec, vec],
        out_specs=tile,
        compiler_params=_cparams("arbitrary", "arbitrary"),
        name="ssm_glu_norm",
    )(y, x, mods, wa, wb, ln_g, ln_b)


def _to_groups(x):
    bsz, n, _ = x.shape
    j = n // SSM_CHUNK
    xg = x.reshape(bsz, j, SSM_CHUNK, SSM_N_GROUPS, SSM_GROUP).transpose(3, 0, 1, 2, 4)
    return xg.reshape(SSM_N_GROUPS, bsz * j, SSM_CHUNK * SSM_GROUP)


def _from_groups(yg, bsz):
    g, r, _ = yg.shape
    j = r // bsz
    y = yg.reshape(g, bsz, j, SSM_CHUNK, SSM_GROUP).transpose(1, 2, 3, 0, 4)
    return y.reshape(bsz, j * SSM_CHUNK, D_MODEL)


def _group_vec(v):
    s = v.shape[0]
    vg = v.reshape(s, SSM_N_GROUPS, 1, SSM_GROUP).transpose(1, 0, 2, 3)
    return jnp.broadcast_to(vg, (SSM_N_GROUPS, s, SSM_CHUNK, SSM_GROUP)).reshape(SSM_N_GROUPS, s, -1)


def _ssm_mixer(x_lat, x_ctx, mods_lat, mods_ctx, lam_re, lam_im, log_dt, b_re, b_im, c_re, c_im, d_skip):
    bsz, n, _ = x_lat.shape
    lctx = x_ctx.shape[1]
    g, p = SSM_N_GROUPS, SSM_STATE
    j_lat, j_ctx = n // SSM_CHUNK, lctx // SSM_CHUNK
    wb, toep, wc, a = _ssm_prep(lam_re, lam_im, log_dt, b_re, b_im, c_re, c_im)

    xg = jnp.concatenate([_to_groups(x_lat), _to_groups(x_ctx)], axis=1)
    scale = _group_vec(jnp.concatenate([1.0 + mods_lat[:, 1], 1.0 + mods_ctx[:, 1]], axis=0))
    shift = _group_vec(jnp.concatenate([mods_lat[:, 0], mods_ctx[:, 0]], axis=0))
    seq_rows = (j_lat,) * bsz + (j_ctx,) * bsz
    s_loc = _ssm_states(xg, scale, shift, wb, seq_rows=seq_rows)

    s_lat = s_loc[:, :bsz * j_lat].reshape(g, bsz, j_lat, 4, p)
    s_ctx = s_loc[:, bsz * j_lat:].reshape(g, bsz, j_ctx, 4, p)
    chain = jnp.concatenate([s_ctx, s_lat], axis=2)
    chain_rev = jnp.concatenate([s_ctx[:, :, ::-1], s_lat[:, :, ::-1]], axis=2)
    x_time = jnp.stack([chain[..., 0, :], chain_rev[..., 1, :], chain[..., 2, :], chain_rev[..., 3, :]], axis=3)
    x_time = x_time.transpose(2, 0, 1, 3, 4).reshape(j_ctx + j_lat, g * bsz, 4 * p)
    a_rows = jnp.broadcast_to(a, (g, bsz, 4 * p)).reshape(g * bsz, 4 * p)
    s_pre = _ssm_scan(x_time, a_rows, rb=32)
    s_pre = s_pre[j_ctx:].reshape(j_lat, g, bsz, 4, p)
    s_in = jnp.stack([s_pre[..., 0, :], s_pre[::-1, ..., 1, :], s_pre[..., 2, :], s_pre[::-1, ..., 3, :]], axis=3)
    s_in = s_in.transpose(1, 2, 0, 3, 4).reshape(g, bsz * j_lat, 4 * p).astype(BF16)

    n_lat_rows = bsz * j_lat
    yg = _ssm_out(xg[:, :n_lat_rows], scale[:, :bsz], shift[:, :bsz], s_in, toep, wc,
                  _group_vec(d_skip[None, :]), seq_rows=(j_lat,) * bsz)
    return _from_groups(yg, bsz)


def _gmlp_kernel(x_ref, mod_ref, win_ref, bin_ref, lg_ref, lb_ref, ws_ref, bs_ref, wout_ref, g_ref, b_ref, o_ref):
    m = mod_ref[0]
    x = x_ref[0]
    t = x.shape[0]
    h = (x * (1.0 + m[1:2]) + m[0:1]).astype(BF16)
    z = jax.nn.gelu(_dot(h, win_ref[...]) + bin_ref[...])
    u = z[:, :GMLP_HALF]
    v = z[:, GMLP_HALF:]
    mu = jnp.mean(v, axis=-1, keepdims=True)
    vc = v - mu
    var = jnp.mean(vc * vc, axis=-1, keepdims=True)
    v = (vc * lax.rsqrt(var + LN_EPS) * lg_ref[...] + lb_ref[...]).astype(BF16)
    bs = bs_ref[...]
    rows = []
    for c in range(t // GMLP_CHUNK):
        r0 = c * GMLP_CHUNK
        cols = []
        for hd in range(GMLP_HEADS):
            c0 = hd * GMLP_HEAD_DIM
            gate = _dot(ws_ref[hd], v[r0:r0 + GMLP_CHUNK, c0:c0 + GMLP_HEAD_DIM]) + bs[:, hd:hd + 1]
            cols.append((u[r0:r0 + GMLP_CHUNK, c0:c0 + GMLP_HEAD_DIM] * gate).astype(BF16))
        rows.append(jnp.concatenate(cols, axis=1))
    gated = jnp.concatenate(rows, axis=0)
    y = _dot(gated, wout_ref[...])
    o_ref[0] = _post_norm(x, y, m[2:3], g_ref[...], b_ref[...])


def _gmlp(x, mods, w_in, b_in, lg, lb, w_s, b_s_t, w_out, ln_g, ln_b, *, t):
    bsz, n, d = x.shape
    tile = pl.BlockSpec((1, t, d), lambda b, i: (b, i, 0))
    const2 = lambda a: pl.BlockSpec(a.shape, lambda b, i: (0,) * a.ndim)
    return pl.pallas_call(
        _gmlp_kernel,
        out_shape=jax.ShapeDtypeStruct((bsz, n, d), F32),
        grid=(bsz, n // t),
        in_specs=[tile, pl.BlockSpec((1, N_MODS, d), lambda b, i: (b, 0, 0)),
                  const2(w_in), const2(b_in), const2(lg), const2(lb), const2(w_s), const2(b_s_t), const2(w_out),
                  const2(ln_g), const2(ln_b)],
        out_specs=tile,
        compiler_params=_cparams("arbitrary", "arbitrary"),
        name="gmlp_mixer",
    )(x, mods, w_in, b_in, lg, lb, w_s, b_s_t, w_out, ln_g, ln_b)


def _tile(n, pref):
    return pref if n % pref == 0 else n


def _layer(layer, x_lat, x_ctx, mods, p):
    bsz, _, d = x_lat.shape
    lctx = x_ctx.shape[1]
    row = lambda v: v.reshape(1, -1)
    t_lat, t_ctx = 512, _tile(lctx, 256)
    kind = MIXERS[layer % len(MIXERS)]
    j = layer // len(MIXERS)
    ctx_out = any(MIXERS[m % len(MIXERS)] in CTX_READING_MIXERS for m in range(layer + 1, DEPTH))
    m_lat = mods[layer, :bsz]
    m_ctx = jnp.broadcast_to(mods[layer, bsz:bsz + 1], (bsz, N_MODS, d))
    g1, b1 = row(p["ln1_g"][layer]), row(p["ln1_b"][layer])

    def ffn(xs, ms, t):
        return _ffn(xs, ms, p["ffn_w_up"][layer].astype(BF16), p["ffn_conv_w"][layer], row(p["ffn_conv_b"][layer]),
                    p["ffn_w_down"][layer].astype(BF16), row(p["ln2_g"][layer]), row(p["ln2_b"][layer]),
                    t=t, fc=FFN_HIDDEN // 2)

    if kind == "pool":
        args = (p["pool_w"][j].astype(BF16), row(p["pool_b"][j]), row(p["pool_scale"][j]), g1, b1)
        x_lat = _pool(x_lat, m_lat, *args, t=t_lat)
        if ctx_out:
            x_ctx = _pool(x_ctx, m_ctx, *args, t=t_ctx)
    elif kind == "attn":
        wqkv = p["attn_w_qkv"][j].astype(BF16)
        wo = p["attn_w_o"][j].astype(BF16)
        sink = p["attn_sink"][j].astype(F32)
        q, kd, vd = _qkv(x_lat, m_lat, wqkv, t=t_lat, rope=True)
        qc, kdc, vdc = _qkv(x_ctx, m_ctx, wqkv, t=t_ctx, rope=False)
        o_lat = _attn(sink, q, kd, vd, kdc, vdc, local=True)
        x_lat = _proj_norm(o_lat, x_lat, m_lat, wo, g1, b1, t=t_lat)
        if ctx_out:
            o_ctx = _attn(sink, qc, None, None, kdc, vdc, local=False)
            x_ctx = _proj_norm(o_ctx, x_ctx, m_ctx, wo, g1, b1, t=t_ctx)
    elif kind == "ssm":
        assert not ctx_out
        y = _ssm_mixer(x_lat, x_ctx, m_lat, m_ctx, p["ssm_lambda_re"][j], p["ssm_lambda_im"][j], p["ssm_log_dt"][j],
                       p["ssm_b_re"][j], p["ssm_b_im"][j], p["ssm_c_re"][j], p["ssm_c_im"][j], p["ssm_d"][j])
        x_lat = _glu_norm(y, x_lat, m_lat, p["ssm_w_glu_a"][j].astype(BF16), p["ssm_w_glu_b"][j].astype(BF16),
                          g1, b1, t=t_lat)
    else:
        assert not ctx_out
        x_lat = _gmlp(x_lat, m_lat, p["gmlp_w_in"][j].astype(BF16), row(p["gmlp_b_in"][j]), row(p["gmlp_ln_g"][j]),
                      row(p["gmlp_ln_b"][j]), p["gmlp_w_s"][j].astype(BF16), p["gmlp_b_s"][j].T,
                      p["gmlp_w_out"][j].astype(BF16), g1, b1, t=256)
    x_lat = ffn(x_lat, m_lat, t_lat)
    if ctx_out:
        x_ctx = ffn(x_ctx, m_ctx, t_ctx)
    return x_lat, x_ctx


def _mods(c, c_ctx, ada_w, ada_b):
    bsz, d = c.shape
    cond = jnp.concatenate([c, c_ctx[None, :], jnp.zeros((8 - bsz - 1, d), F32)], axis=0)
    return _ada(cond, ada_w, ada_b).reshape(DEPTH, 8, N_MODS, d)


def kernel(x, c, ctx, c_ctx, ada_w, ada_b, ln1_g, ln1_b, ln2_g, ln2_b, ffn_w_up, ffn_conv_w, ffn_conv_b, ffn_w_down, pool_w, pool_b, pool_scale, attn_w_qkv, attn_w_o, attn_sink, ssm_lambda_re, ssm_lambda_im, ssm_log_dt, ssm_b_re, ssm_b_im, ssm_c_re, ssm_c_im, ssm_d, ssm_w_glu_a, ssm_w_glu_b, gmlp_w_in, gmlp_b_in, gmlp_ln_g, gmlp_ln_b, gmlp_w_s, gmlp_b_s, gmlp_w_out):
    bsz, n, d = x.shape
    assert d == D_MODEL and bsz < 8 and n % 512 == 0 and ctx.shape[1] % ATTN_BLOCK == 0
    p = dict(ln1_g=ln1_g, ln1_b=ln1_b, ln2_g=ln2_g, ln2_b=ln2_b, ffn_w_up=ffn_w_up, ffn_conv_w=ffn_conv_w,
             ffn_conv_b=ffn_conv_b, ffn_w_down=ffn_w_down, pool_w=pool_w, pool_b=pool_b, pool_scale=pool_scale,
             attn_w_qkv=attn_w_qkv, attn_w_o=attn_w_o, attn_sink=attn_sink, ssm_lambda_re=ssm_lambda_re,
             ssm_lambda_im=ssm_lambda_im, ssm_log_dt=ssm_log_dt, ssm_b_re=ssm_b_re, ssm_b_im=ssm_b_im,
             ssm_c_re=ssm_c_re, ssm_c_im=ssm_c_im, ssm_d=ssm_d, ssm_w_glu_a=ssm_w_glu_a, ssm_w_glu_b=ssm_w_glu_b,
             gmlp_w_in=gmlp_w_in, gmlp_b_in=gmlp_b_in, gmlp_ln_g=gmlp_ln_g, gmlp_ln_b=gmlp_ln_b, gmlp_w_s=gmlp_w_s,
             gmlp_b_s=gmlp_b_s, gmlp_w_out=gmlp_w_out)
    mods = _mods(c, c_ctx, ada_w, ada_b)
    x_lat, x_ctx = x, ctx
    for layer in range(DEPTH):
        x_lat, x_ctx = _layer(layer, x_lat, x_ctx, mods, p)
    return x_lat
```

```python
import functools
import math

import jax
import jax.numpy as jnp
from jax import lax
from jax.experimental import pallas as pl
from jax.experimental.pallas import tpu as pltpu

F32 = jnp.float32
BF16 = jnp.bfloat16

D_MODEL = 1024
DEPTH = 4
MIXERS = ("pool", "attn", "ssm", "gmlp")
CTX_READING_MIXERS = ("attn", "ssm")
GRID_W = 64
N_MODS = 6
DEEPNORM_ALPHA = (2.0 * DEPTH) ** 0.25
LN_EPS = 1e-5

POOL_WINDOWS = (2, 4, 8, 16)
POOL_GROUP = D_MODEL // len(POOL_WINDOWS)

HEAD_DIM = 64
N_Q_HEADS = D_MODEL // HEAD_DIM
N_KV_HEADS = N_Q_HEADS // 4
Q_WIDTH = N_Q_HEADS * HEAD_DIM
KV_WIDTH = N_KV_HEADS * HEAD_DIM
WINDOW = 128
ATTN_BLOCK = 128
ROPE_BASE = 10000.0
NEG_INF = -1e30

SSM_GROUP = 16
SSM_N_GROUPS = D_MODEL // SSM_GROUP
SSM_STATE = 64
SSM_CHUNK = 16

GMLP_CHUNK = 128
GMLP_HALF = 2 * D_MODEL
GMLP_HEADS = 8
GMLP_HEAD_DIM = GMLP_HALF // GMLP_HEADS

FFN_HIDDEN = 2816

LANES = 128
HALO = 16
VMEM_LIMIT = 56 * 1024 * 1024


def _cparams(*sem):
    return pltpu.CompilerParams(dimension_semantics=sem, vmem_limit_bytes=VMEM_LIMIT)


def _const_spec(shape):
    nd = len(shape)
    return pl.BlockSpec(shape, lambda *_: (0,) * nd, pipeline_mode=pl.Buffered(1))


def _post_norm(x, y, gate, g, b):
    z = DEEPNORM_ALPHA * x + gate * y
    mu = jnp.mean(z, axis=-1, keepdims=True)
    zc = z - mu
    var = jnp.mean(zc * zc, axis=-1, keepdims=True)
    return zc * lax.rsqrt(var + LN_EPS) * g + b


def _dot(a, b):
    return jnp.dot(a, b, preferred_element_type=F32)


def _dot_nt(a, b):
    return lax.dot_general(a, b, (((1,), (1,)), ((), ())), preferred_element_type=F32)


def _ada_kernel(c_ref, w_ref, b_ref, o_ref):
    c = c_ref[...]
    s = (c * jax.nn.sigmoid(c)).astype(BF16)
    o_ref[0] = _dot(s, w_ref[0].astype(BF16)) + b_ref[0]


def _ada(cond, ada_w, ada_b):
    depth, d, n = ada_w.shape
    rows = cond.shape[0]
    tn = 1536
    return pl.pallas_call(
        _ada_kernel,
        out_shape=jax.ShapeDtypeStruct((depth, rows, n), F32),
        grid=(depth, n // tn),
        in_specs=[pl.BlockSpec((rows, d), lambda l, j: (0, 0)),
                  pl.BlockSpec((1, d, tn), lambda l, j: (l, 0, j)),
                  pl.BlockSpec((1, 1, tn), lambda l, j: (l, 0, j))],
        out_specs=pl.BlockSpec((1, rows, tn), lambda l, j: (l, 0, j)),
        compiler_params=_cparams("arbitrary", "arbitrary"),
        name="ada",
    )(cond, ada_w, ada_b.reshape(depth, 1, n))


def _halo_specs(t, n):
    per = t // HALO
    last = n // HALO - 1
    prev = pl.BlockSpec((1, HALO, D_MODEL), lambda b, i, *_: (b, jnp.maximum(i * per - 1, 0), 0))
    nxt = pl.BlockSpec((1, HALO, D_MODEL), lambda b, i, *_: (b, jnp.minimum((i + 1) * per, last), 0))
    return prev, nxt


def _ffn_kernel(xp_ref, x_ref, xn_ref, mod_ref, wv_ref, wg_ref, cwv_ref, cwg_ref, cbv_ref, cbg_ref,
                wd_ref, g_ref, b_ref, o_ref, h_ref, acc_ref, *, t, nt, nk):
    i = pl.program_id(1)
    k = pl.program_id(2)
    rows = t + 2 * HALO

    @pl.when(k == 0)
    def _():
        m = mod_ref[0]
        sh, sc = m[3:4], 1.0 + m[4:5]
        keep_p = jnp.where(i > 0, 1.0, 0.0)
        keep_n = jnp.where(i < nt - 1, 1.0, 0.0)
        h_ref[0:HALO, :] = ((xp_ref[0] * sc + sh) * keep_p).astype(BF16)
        h_ref[HALO:HALO + t, :] = (x_ref[0] * sc + sh).astype(BF16)
        h_ref[HALO + t:rows, :] = ((xn_ref[0] * sc + sh) * keep_n).astype(BF16)
        acc_ref[...] = jnp.zeros_like(acc_ref)

    h = h_ref[...]

    def conv(w_ref, cw_ref, cb_ref):
        u = _dot(h, w_ref[...])
        cw = cw_ref[...]
        a = cb_ref[...] + pltpu.roll(u, 1, 0) * cw[0:1]
        a = a + u * cw[1:2]
        a = a + pltpu.roll(u, rows - 1, 0) * cw[2:3]
        return a[HALO:HALO + t]

    val = conv(wv_ref, cwv_ref, cbv_ref)
    gate = conv(wg_ref, cwg_ref, cbg_ref)
    act = (val * (gate * jax.nn.sigmoid(gate))).astype(BF16)
    acc_ref[...] += _dot(act, wd_ref[...])

    @pl.when(k == nk - 1)
    def _():
        m = mod_ref[0]
        o_ref[0] = _post_norm(x_ref[0], acc_ref[...], m[5:6], g_ref[...], b_ref[...])


def _ffn(x, mods, w_up, conv_w, conv_b, w_down, ln_g, ln_b, *, t, fc):
    bsz, n, d = x.shape
    f = w_down.shape[0]
    nt, nk = n // t, f // fc
    prev, nxt = _halo_specs(t, n)
    kern = functools.partial(_ffn_kernel, t=t, nt=nt, nk=nk)
    return pl.pallas_call(
        kern,
        out_shape=jax.ShapeDtypeStruct((bsz, n, d), F32),
        grid=(bsz, nt, nk),
        in_specs=[prev,
                  pl.BlockSpec((1, t, d), lambda b, i, k: (b, i, 0)),
                  nxt,
                  pl.BlockSpec((1, N_MODS, d), lambda b, i, k: (b, 0, 0)),
                  pl.BlockSpec((d, fc), lambda b, i, k: (0, k)),
                  pl.BlockSpec((d, fc), lambda b, i, k: (0, nk + k)),
                  pl.BlockSpec((3, fc), lambda b, i, k: (0, k)),
                  pl.BlockSpec((3, fc), lambda b, i, k: (0, nk + k)),
                  pl.BlockSpec((1, fc), lambda b, i, k: (0, k)),
                  pl.BlockSpec((1, fc), lambda b, i, k: (0, nk + k)),
                  pl.BlockSpec((fc, d), lambda b, i, k: (k, 0)),
                  pl.BlockSpec((1, d), lambda b, i, k: (0, 0)),
                  pl.BlockSpec((1, d), lambda b, i, k: (0, 0))],
        out_specs=pl.BlockSpec((1, t, d), lambda b, i, k: (b, i, 0)),
        scratch_shapes=[pltpu.VMEM((t + 2 * HALO, d), BF16), pltpu.VMEM((t, d), F32)],
        compiler_params=_cparams("arbitrary", "arbitrary", "arbitrary"),
        name="conv_ffn",
    )(x, x, x, mods, w_up, w_up, conv_w, conv_w, conv_b, conv_b, w_down, ln_g, ln_b)


def _pool_kernel(xp_ref, x_ref, xn_ref, mod_ref, w_ref, pb_ref, ps_ref, g_ref, b_ref, o_ref, *, t, nt, n):
    i = pl.program_id(1)
    rows = t + 2 * HALO
    m = mod_ref[0]
    sh, sc = m[0:1], 1.0 + m[1:2]
    keep_p = jnp.where(i > 0, 1.0, 0.0)
    keep_n = jnp.where(i < nt - 1, 1.0, 0.0)
    x = x_ref[0]
    h = jnp.concatenate([(xp_ref[0] * sc + sh) * keep_p, x * sc + sh, (xn_ref[0] * sc + sh) * keep_n], axis=0)
    pos = i * t + lax.broadcasted_iota(jnp.int32, (t, 1), 0)
    outs = []
    for gi, win in enumerate(POOL_WINDOWS):
        hg = h[:, gi * POOL_GROUP:(gi + 1) * POOL_GROUP]
        s = hg + pltpu.roll(hg, 1, 0)
        half = 1
        while 2 * half < win:
            s = pltpu.roll(s, half, 0) + pltpu.roll(s, rows - half, 0)
            half *= 2
        lo = jnp.maximum(pos - win // 2, 0)
        hi = jnp.minimum(pos - win // 2 + win, n)
        mean = s[HALO:HALO + t] / (hi - lo).astype(F32)
        mixed = (mean - hg[HALO:HALO + t]).astype(BF16)
        outs.append(_dot(mixed, w_ref[gi]))
    y = (jnp.concatenate(outs, axis=1) + pb_ref[...]) * ps_ref[...]
    o_ref[0] = _post_norm(x, y, m[2:3], g_ref[...], b_ref[...])


def _pool(x, mods, w, pb, ps, ln_g, ln_b, *, t):
    bsz, n, d = x.shape
    nt = n // t
    prev, nxt = _halo_specs(t, n)
    kern = functools.partial(_pool_kernel, t=t, nt=nt, n=n)
    vec = pl.BlockSpec((1, d), lambda b, i: (0, 0))
    return pl.pallas_call(
        kern,
        out_shape=jax.ShapeDtypeStruct((bsz, n, d), F32),
        grid=(bsz, nt),
        in_specs=[prev, pl.BlockSpec((1, t, d), lambda b, i: (b, i, 0)), nxt,
                  pl.BlockSpec((1, N_MODS, d), lambda b, i: (b, 0, 0)),
                  pl.BlockSpec(w.shape, lambda b, i: (0, 0, 0)),
                  vec, vec, vec, vec],
        out_specs=pl.BlockSpec((1, t, d), lambda b, i: (b, i, 0)),
        compiler_params=_cparams("arbitrary", "arbitrary"),
        name="pool_mixer",
    )(x, x, x, mods, w, pb, ps, ln_g, ln_b)


def _rope_tables(n):
    tpos = jnp.arange(n)
    half = HEAD_DIM // 4
    freqs = ROPE_BASE ** (-jnp.arange(half, dtype=F32) / half)
    ang_r = (tpos // GRID_W).astype(F32)[:, None] * freqs[None, :]
    ang_c = (tpos % GRID_W).astype(F32)[:, None] * freqs[None, :]
    zero = jnp.zeros_like(ang_r)
    cr, sr, cc, sc = jnp.cos(ang_r), jnp.sin(ang_r), jnp.cos(ang_c), jnp.sin(ang_c)
    cos = jnp.tile(jnp.concatenate([cr, cr, cc, cc], axis=1), (1, 2))
    sin_first = jnp.tile(jnp.concatenate([-sr, zero, -sc, zero], axis=1), (1, 2))
    sin_second = jnp.tile(jnp.concatenate([zero, sr, zero, sc], axis=1), (1, 2))
    return cos, sin_first, sin_second


def _dup_heads(chunk, lo):
    sw = pltpu.roll(chunk, HEAD_DIM, 1)
    return jnp.where(lo, chunk, sw), jnp.where(lo, sw, chunk)


def _qkv_kernel(*refs, rope):
    if rope:
        x_ref, mod_ref, w_ref, cos_ref, sa_ref, sb_ref, q_ref, kd_ref, vd_ref = refs
    else:
        x_ref, mod_ref, w_ref, q_ref, kd_ref, vd_ref = refs
    m = mod_ref[0]
    h = (x_ref[0] * (1.0 + m[1:2]) + m[0:1]).astype(BF16)
    qkv = _dot(h, w_ref[...])
    t = qkv.shape[0]
    lo = lax.broadcasted_iota(jnp.int32, (t, LANES), 1) < HEAD_DIM
    scale = HEAD_DIM ** -0.5

    def rot(v):
        if not rope:
            return v
        quarter = HEAD_DIM // 4
        return (v * cos_ref[...] + pltpu.roll(v, LANES - quarter, 1) * sa_ref[...]
                + pltpu.roll(v, quarter, 1) * sb_ref[...])

    for c in range(Q_WIDTH // LANES):
        q_ref[0, :, c * LANES:(c + 1) * LANES] = (rot(qkv[:, c * LANES:(c + 1) * LANES]) * scale).astype(BF16)
    for c in range(KV_WIDTH // LANES):
        k0, k1 = _dup_heads(rot(qkv[:, Q_WIDTH + c * LANES:Q_WIDTH + (c + 1) * LANES]), lo)
        kd_ref[0, :, (2 * c) * LANES:(2 * c + 1) * LANES] = k0.astype(BF16)
        kd_ref[0, :, (2 * c + 1) * LANES:(2 * c + 2) * LANES] = k1.astype(BF16)
        v0, v1 = _dup_heads(qkv[:, Q_WIDTH + KV_WIDTH + c * LANES:Q_WIDTH + KV_WIDTH + (c + 1) * LANES], lo)
        vd_ref[0, :, (2 * c) * LANES:(2 * c + 1) * LANES] = v0.astype(BF16)
        vd_ref[0, :, (2 * c + 1) * LANES:(2 * c + 2) * LANES] = v1.astype(BF16)


def _qkv(x, mods, w_qkv, *, t, rope):
    bsz, n, d = x.shape
    kdw = N_KV_HEADS * LANES
    ins = [x, mods, w_qkv]
    specs = [pl.BlockSpec((1, t, d), lambda b, i: (b, i, 0)),
             pl.BlockSpec((1, N_MODS, d), lambda b, i: (b, 0, 0)),
             pl.BlockSpec(w_qkv.shape, lambda b, i: (0, 0))]
    if rope:
        ins += list(_rope_tables(n))
        specs += [pl.BlockSpec((t, LANES), lambda b, i: (i, 0))] * 3
    return pl.pallas_call(
        functools.partial(_qkv_kernel, rope=rope),
        out_shape=(jax.ShapeDtypeStruct((bsz, n, Q_WIDTH), BF16),
                   jax.ShapeDtypeStruct((bsz, n, kdw), BF16),
                   jax.ShapeDtypeStruct((bsz, n, kdw), BF16)),
        grid=(bsz, n // t),
        in_specs=specs,
        out_specs=(pl.BlockSpec((1, t, Q_WIDTH), lambda b, i: (b, i, 0)),
                   pl.BlockSpec((1, t, kdw), lambda b, i: (b, i, 0)),
                   pl.BlockSpec((1, t, kdw), lambda b, i: (b, i, 0))),
        compiler_params=_cparams("arbitrary", "arbitrary"),
        name="qkv_rope" if rope else "qkv_ctx",
    )(*ins)


def _attn_kernel(sink_ref, q_ref, *refs, nb, local):
    if local:
        kp_ref, kc_ref, kn_ref, kx_ref, vp_ref, vc_ref, vn_ref, vx_ref, o_ref = refs
        k_refs, v_refs = (kp_ref, kc_ref, kn_ref, kx_ref), (vp_ref, vc_ref, vn_ref, vx_ref)
    else:
        kx_ref, vx_ref, o_ref = refs
        k_refs, v_refs = (kx_ref,), (vx_ref,)
    blk = ATTN_BLOCK
    group = N_Q_HEADS // N_KV_HEADS
    rows = group * blk
    n_keys = sum(r.shape[1] for r in k_refs)
    nblk = pl.program_id(1)
    if local:
        qi = lax.broadcasted_iota(jnp.int32, (rows, n_keys), 0) % blk
        kj = lax.broadcasted_iota(jnp.int32, (rows, n_keys), 1)
        rel = kj - blk - qi
        valid = (jnp.abs(rel) <= WINDOW) & ((kj >= blk) | (nblk > 0)) & ((kj < 2 * blk) | (nblk < nb - 1))
        valid = valid | (kj >= 3 * blk)
    lo = lax.broadcasted_iota(jnp.int32, (blk, LANES), 1) < HEAD_DIM
    head_of_row = lax.broadcasted_iota(jnp.int32, (rows, 1), 0) // blk
    for hk in range(N_KV_HEADS):
        kx = jnp.concatenate([r[0, :, hk * LANES:(hk + 1) * LANES] for r in k_refs], axis=0)
        vx = jnp.concatenate([r[0, :, hk * LANES:(hk + 1) * LANES] for r in v_refs], axis=0)
        parts = []
        for c in range(group // 2):
            qc = q_ref[0, :, (hk * group // 2 + c) * LANES:(hk * group // 2 + c + 1) * LANES]
            zero = jnp.zeros_like(qc)
            parts += [jnp.where(lo, qc, zero), jnp.where(lo, zero, qc)]
        q4 = jnp.concatenate(parts, axis=0)
        s = _dot_nt(q4, kx)
        if local:
            s = jnp.where(valid, s, NEG_INF)
        sink = jnp.zeros((rows, 1), F32)
        for g in range(group):
            sink = jnp.where(head_of_row == g, sink_ref[hk * group + g], sink)
        mx = jnp.maximum(jnp.max(s, axis=-1, keepdims=True), sink)
        p = jnp.exp(s - mx)
        den = jnp.sum(p, axis=-1, keepdims=True) + jnp.exp(sink - mx)
        o4 = _dot(p.astype(BF16), vx) / den
        for c in range(group // 2):
            oc = jnp.where(lo, o4[(2 * c) * blk:(2 * c + 1) * blk], o4[(2 * c + 1) * blk:(2 * c + 2) * blk])
            o_ref[0, :, (hk * group // 2 + c) * LANES:(hk * group // 2 + c + 1) * LANES] = oc.astype(BF16)


def _attn(sink, q, kd, vd, kd_ctx, vd_ctx, *, local):
    bsz, n, _ = q.shape
    blk = ATTN_BLOCK
    nb = n // blk
    lctx = kd_ctx.shape[1]
    kdw = kd_ctx.shape[2]
    smem = pl.BlockSpec(memory_space=pltpu.SMEM)
    qspec = pl.BlockSpec((1, blk, Q_WIDTH), lambda b, i: (b, i, 0))
    ctx_spec = pl.BlockSpec((1, lctx, kdw), lambda b, i: (b, 0, 0))
    if local:
        band = [pl.BlockSpec((1, blk, kdw), lambda b, i: (b, jnp.maximum(i - 1, 0), 0)),
                pl.BlockSpec((1, blk, kdw), lambda b, i: (b, i, 0)),
                pl.BlockSpec((1, blk, kdw), lambda b, i: (b, jnp.minimum(i + 1, nb - 1), 0))]
        specs = [smem, qspec] + band + [ctx_spec] + band + [ctx_spec]
        args = (sink, q, kd, kd, kd, kd_ctx, vd, vd, vd, vd_ctx)
    else:
        specs = [smem, qspec, ctx_spec, ctx_spec]
        args = (sink, q, kd_ctx, vd_ctx)
    return pl.pallas_call(
        functools.partial(_attn_kernel, nb=nb, local=local),
        out_shape=jax.ShapeDtypeStruct((bsz, n, Q_WIDTH), BF16),
        grid=(bsz, nb),
        in_specs=specs,
        out_specs=pl.BlockSpec((1, blk, Q_WIDTH), lambda b, i: (b, i, 0)),
        compiler_params=_cparams("arbitrary", "arbitrary"),
        name="banded_attn" if local else "ctx_attn",
    )(*args)


def _proj_norm_kernel(a_ref, x_ref, mod_ref, w_ref, g_ref, b_ref, o_ref):
    y = _dot(a_ref[0], w_ref[...])
    o_ref[0] = _post_norm(x_ref[0], y, mod_ref[0][2:3], g_ref[...], b_ref[...])


def _proj_norm(a, x, mods, w, ln_g, ln_b, *, t):
    bsz, n, d = x.shape
    ka = a.shape[2]
    vec = pl.BlockSpec((1, d), lambda b, i: (0, 0))
    return pl.pallas_call(
        _proj_norm_kernel,
        out_shape=jax.ShapeDtypeStruct((bsz, n, d), F32),
        grid=(bsz, n // t),
        in_specs=[pl.BlockSpec((1, t, ka), lambda b, i: (b, i, 0)),
                  pl.BlockSpec((1, t, d), lambda b, i: (b, i, 0)),
                  pl.BlockSpec((1, N_MODS, d), lambda b, i: (b, 0, 0)),
                  pl.BlockSpec(w.shape, lambda b, i: (0, 0)),
                  vec, vec],
        out_specs=pl.BlockSpec((1, t, d), lambda b, i: (b, i, 0)),
        compiler_params=_cparams("arbitrary", "arbitrary"),
        name="attn_out_norm",
    )(a, x, mods, w, ln_g, ln_b)


def _glu_norm_kernel(y_ref, x_ref, mod_ref, wa_ref, wb_ref, g_ref, b_ref, o_ref):
    gl = jax.nn.gelu(y_ref[0]).astype(BF16)
    out = _dot(gl, wa_ref[...]) * jax.nn.sigmoid(_dot(gl, wb_ref[...]))
    o_ref[0] = _post_norm(x_ref[0], out, mod_ref[0][2:3], g_ref[...], b_ref[...])


def _glu_norm(y, x, mods, wa, wb, ln_g, ln_b, *, t):
    bsz, n, d = x.shape
    tile = pl.BlockSpec((1, t, d), lambda b, i: (b, i, 0))
    vec = pl.BlockSpec((1, d), lambda b, i: (0, 0))
    wspec = pl.BlockSpec((d, d), lambda b, i: (0, 0))
    return pl.pallas_call(
        _glu_norm_kernel,
        out_shape=jax.ShapeDtypeStruct((bsz, n, d), F32),
        grid=(bsz, n // t),
        in_specs=[tile, tile, pl.BlockSpec((1, N_MODS, d), lambda b, i: (b, 0, 0)), wspec, wspec, vec, vec],
        out_specs=tile,
        compiler_params=_cparams("arbitrary", "arbitrary"),
        name="ssm_glu_norm",
    )(y, x, mods, wa, wb, ln_g, ln_b)


SSM_QG = LANES // SSM_GROUP
SSM_NQ = D_MODEL // LANES
SSM_ROWS = 8
SSM_W = SSM_CHUNK * LANES
SSM_HALF = SSM_QG * SSM_STATE
SSM_TAP = SSM_CHUNK * SSM_GROUP
SSM_VR = 2 * SSM_HALF // LANES


def _s5_prep_kernel(lr_ref, li_ref, ldt_ref, btr_ref, bti_ref, cr_ref, ci_ref, wb_ref, wc_ref, bd_ref, a_ref):
    c = SSM_CHUNK
    q = pl.program_id(0)
    lag = lax.broadcasted_iota(jnp.int32, (c + 1, LANES), 0).astype(F32)
    lane = lax.broadcasted_iota(jnp.int32, (1, LANES), 1)
    taps = [[[] for _ in range(c)] for _ in range(2)]
    for g8 in range(SSM_QG):
        own = jnp.where((lane < SSM_STATE) == (g8 % 2 == 0), 1.0, 0.0)
        wb_cols, wc_cols, a_cols = [], [], []
        for d in range(2):
            lr, li = lr_ref[g8, d], li_ref[g8, d]
            dt = jnp.exp(jnp.full((1, LANES), ldt_ref[q * SSM_QG + g8, d], F32))
            mag = jnp.exp(lag * (lr * dt))
            ang = lag * (li * dt)
            pw_r, pw_i = mag * jnp.cos(ang), mag * jnp.sin(ang)
            lbr, lbi = pw_r[1:2], pw_i[1:2]
            den = lr * lr + li * li
            qr = ((lbr - 1.0) * lr + lbi * li) / den
            qi = (lbi * lr - (lbr - 1.0) * li) / den
            btr, bti = btr_ref[g8, d], bti_ref[g8, d]
            bbr = qr * btr - qi * bti
            bbi = qr * bti + qi * btr
            cr, ci = cr_ref[g8, d], ci_ref[g8, d]

            def cl(j):
                return cr * pw_r[j:j + 1] - ci * pw_i[j:j + 1], -(cr * pw_i[j:j + 1] + ci * pw_r[j:j + 1])

            def bl(j):
                return bbr * pw_r[j:j + 1] - bbi * pw_i[j:j + 1], bbr * pw_i[j:j + 1] + bbi * pw_r[j:j + 1]

            e = jnp.concatenate([jnp.concatenate(cl(j), axis=1) for j in range(c)], axis=0)
            pieces = [jnp.concatenate([bbr * own, bbi * own], axis=1)]
            if g8 > 0:
                pieces.insert(0, jnp.zeros((g8 * SSM_GROUP, 2 * LANES), F32))
            if g8 < SSM_QG - 1:
                pieces.append(jnp.zeros(((SSM_QG - 1 - g8) * SSM_GROUP, 2 * LANES), F32))
            kt = lax.dot_general(e, jnp.concatenate(pieces, axis=0), (((1,), (1,)), ((), ())),
                                 preferred_element_type=F32, precision=lax.Precision.HIGHEST)
            for j in range(c):
                taps[d][j].append(kt[j * SSM_GROUP:(j + 1) * SSM_GROUP])
            wbl = [bl(c - 1 - k) if d == 0 else bl(k) for k in range(c)]
            wb_cols += [jnp.concatenate([w[0] for w in wbl], axis=0) * own,
                        jnp.concatenate([w[1] for w in wbl], axis=0) * own]
            wcl = [cl(k + 1) if d == 0 else cl(c - k) for k in range(c)]
            wc_cols += [jnp.concatenate([w[0] for w in wcl], axis=0) * own,
                        jnp.concatenate([w[1] for w in wcl], axis=0) * own]
            a_cols += [pw_r[c:c + 1], pw_i[c:c + 1]]
        wb_ref[g8] = jnp.concatenate(wb_cols, axis=1).astype(BF16)
        wc_ref[g8] = jnp.concatenate(wc_cols, axis=1).astype(BF16)
        a_ref[g8] = jnp.concatenate(a_cols, axis=1)
    for d in range(2):
        for j in range(c):
            blk = jnp.concatenate(taps[d][j], axis=0)
            if d == 0 and j == 0:
                blk = blk + jnp.concatenate(taps[1][0], axis=0)
            bd_ref[0, d, j] = blk.astype(BF16)


def _s5_prep(lam_re, lam_im, log_dt, b_re, b_im, c_re, c_im):
    g, nq, qg = SSM_N_GROUPS, SSM_NQ, SSM_QG
    per_g = lambda a: jnp.swapaxes(a, 0, 1)
    dup = lambda a: jnp.concatenate([a, a], axis=-1)
    lam_spec = pl.BlockSpec((qg, 2, 1, LANES), lambda i: (i, 0, 0, 0))
    mat_spec = pl.BlockSpec((qg, 2, SSM_GROUP, LANES), lambda i: (i, 0, 0, 0))
    w_spec = pl.BlockSpec((qg, SSM_TAP, 4 * LANES), lambda i: (i, 0, 0))
    return pl.pallas_call(
        _s5_prep_kernel,
        out_shape=(jax.ShapeDtypeStruct((g, SSM_TAP, 4 * LANES), BF16),
                   jax.ShapeDtypeStruct((g, SSM_TAP, 4 * LANES), BF16),
                   jax.ShapeDtypeStruct((nq, 2, SSM_CHUNK, LANES, LANES), BF16),
                   jax.ShapeDtypeStruct((g, 1, 4 * LANES), F32)),
        grid=(nq,),
        in_specs=[lam_spec, lam_spec, pl.BlockSpec(memory_space=pltpu.SMEM), mat_spec, mat_spec, mat_spec, mat_spec],
        out_specs=(w_spec, w_spec, pl.BlockSpec((1, 2, SSM_CHUNK, LANES, LANES), lambda i: (i, 0, 0, 0, 0)),
                   pl.BlockSpec((qg, 1, 4 * LANES), lambda i: (i, 0, 0))),
        compiler_params=_cparams("arbitrary"),
        name="s5_prep",
    )(dup(per_g(lam_re))[:, :, None, :], dup(per_g(lam_im))[:, :, None, :], per_g(log_dt),
      dup(jnp.swapaxes(per_g(b_re), 2, 3)), dup(jnp.swapaxes(per_g(b_im), 2, 3)), dup(per_g(c_re)), dup(per_g(c_im)))


def _s5_expand(dst_ref, src_ref):
    dst_ref[...] = jnp.zeros_like(dst_ref)
    for g8 in range(SSM_QG):
        for k in range(SSM_CHUNK):
            for c4 in range(4):
                r0 = k * LANES + g8 * SSM_GROUP
                c0 = c4 * SSM_HALF + (g8 // 2) * LANES
                dst_ref[r0:r0 + SSM_GROUP, c0:c0 + LANES] = src_ref[g8, k * SSM_GROUP:(k + 1) * SSM_GROUP,
                                                                    c4 * LANES:(c4 + 1) * LANES]


def _s5_modulate(x_ref, mod_ref, u_ref, bsz):
    for b in range(bsz):
        m = mod_ref[b]
        u_ref[b] = x_ref[b] * (1.0 + m[1:2]) + m[0:1]


def _s5_gather(u_ref, bsz, nchunk):
    rows = [jnp.concatenate([u_ref[b, pl.ds(pos, nchunk, stride=SSM_CHUNK), :] for pos in range(SSM_CHUNK)],
                            axis=1).astype(BF16) for b in range(bsz)]
    return jnp.concatenate(rows, axis=0)


def _s5_states_kernel(x_ref, mod_ref, wbc_ref, o_ref, wb_ref, u_ref, *, bsz, nchunk):
    @pl.when(pl.program_id(1) == 0)
    def _():
        _s5_expand(wb_ref, wbc_ref)

    _s5_modulate(x_ref, mod_ref, u_ref, bsz)
    s = _dot(_s5_gather(u_ref, bsz, nchunk), wb_ref[...])
    for b in range(bsz):
        for v in range(SSM_VR):
            for d in range(2):
                o_ref[v, pl.ds(d * bsz + b, nchunk, stride=SSM_ROWS), :] = (
                    s[b * nchunk:(b + 1) * nchunk, (d * SSM_VR + v) * LANES:(d * SSM_VR + v + 1) * LANES])


def _s5_states(x, mods, wbc, *, t):
    bsz, n, _ = x.shape
    nchunk = t // SSM_CHUNK
    return pl.pallas_call(
        functools.partial(_s5_states_kernel, bsz=bsz, nchunk=nchunk),
        out_shape=jax.ShapeDtypeStruct((SSM_NQ * SSM_VR, n // SSM_CHUNK * SSM_ROWS, LANES), F32),
        grid=(SSM_NQ, n // t),
        in_specs=[pl.BlockSpec((bsz, t, LANES), lambda q, i: (0, i, q)),
                  pl.BlockSpec((bsz, N_MODS, LANES), lambda q, i: (0, 0, q)),
                  pl.BlockSpec((SSM_QG, SSM_TAP, 4 * LANES), lambda q, i: (q, 0, 0))],
        out_specs=pl.BlockSpec((SSM_VR, nchunk * SSM_ROWS, LANES), lambda q, i: (q, i, 0)),
        scratch_shapes=[pltpu.VMEM((SSM_W, 4 * SSM_HALF), BF16), pltpu.VMEM((bsz, t, LANES), F32)],
        compiler_params=_cparams("arbitrary", "arbitrary"),
        name="s5_chunk_states",
    )(x, mods, wbc)


def _s5_scan_kernel(xc_ref, xl_ref, a_ref, o_ref):
    h = SSM_VR // 2
    ar, ai = a_ref[:h], a_ref[h:]
    nc, nl = xc_ref.shape[1] // SSM_ROWS, xl_ref.shape[1] // SSM_ROWS
    fwd_rows = lax.broadcasted_iota(jnp.int32, (SSM_VR, SSM_ROWS, LANES), 1) < SSM_ROWS // 2

    def step(ref, i, s):
        x = ref[:, pl.ds(pl.multiple_of(i * SSM_ROWS, SSM_ROWS), SSM_ROWS), :]
        return ar * s[0] - ai * s[1] + x[:h], ar * s[1] + ai * s[0] + x[h:]

    def fwd(i, s):
        o_ref[:, pl.ds(pl.multiple_of(i * SSM_ROWS, SSM_ROWS), SSM_ROWS), :] = jnp.concatenate(s, axis=0)
        return step(xl_ref, i, s)

    def bwd(k, s):
        rows = pl.ds(pl.multiple_of((nl - 1 - k) * SSM_ROWS, SSM_ROWS), SSM_ROWS)
        o_ref[:, rows, :] = jnp.where(fwd_rows, o_ref[:, rows, :], jnp.concatenate(s, axis=0))
        return step(xl_ref, nl - 1 - k, s)

    zero = (jnp.zeros((h, SSM_ROWS, LANES), F32), jnp.zeros((h, SSM_ROWS, LANES), F32))
    s = lax.fori_loop(0, nc, lambda i, s: step(xc_ref, i, s), zero)
    lax.fori_loop(0, nl, fwd, s)
    s = lax.fori_loop(0, nc, lambda k, s: step(xc_ref, nc - 1 - k, s), zero)
    lax.fori_loop(0, nl, bwd, s)


def _s5_scan(s_ctx, s_lat, a_rows):
    blk = lambda rows: pl.BlockSpec((SSM_VR, rows, LANES), lambda q: (q, 0, 0))
    return pl.pallas_call(
        _s5_scan_kernel,
        out_shape=jax.ShapeDtypeStruct(s_lat.shape, F32),
        grid=(SSM_NQ,),
        in_specs=[blk(s_ctx.shape[1]), blk(s_lat.shape[1]), blk(SSM_ROWS)],
        out_specs=blk(s_lat.shape[1]),
        compiler_params=_cparams("arbitrary"),
        name="s5_chunk_scan",
    )(s_ctx, s_lat, a_rows)


def _s5_out_kernel(x_ref, mod_ref, s_ref, bd_ref, wcc_ref, d_ref, o_ref, toep_ref, wc_ref, u_ref, y_ref, *,
                   bsz, nchunk):
    c = SSM_CHUNK

    @pl.when(pl.program_id(1) == 0)
    def _():
        _s5_expand(wc_ref, wcc_ref)
        for t in range(c):
            for k in range(c):
                toep_ref[t * LANES:(t + 1) * LANES, k * LANES:(k + 1) * LANES] = (
                    bd_ref[0, 0, t - k] if t >= k else bd_ref[0, 1, k - t])

    _s5_modulate(x_ref, mod_ref, u_ref, bsz)
    lhs_s = jnp.concatenate(
        [jnp.concatenate([s_ref[v, pl.ds(d * bsz + b, nchunk, stride=SSM_ROWS), :]
                          for d in range(2) for v in range(SSM_VR)], axis=1).astype(BF16)
         for b in range(bsz)], axis=0)
    y = _dot_nt(_s5_gather(u_ref, bsz, nchunk), toep_ref[...]) + _dot_nt(lhs_s, wc_ref[...])
    for b in range(bsz):
        for t in range(c):
            y_ref[b, pl.ds(t, nchunk, stride=c), :] = y[b * nchunk:(b + 1) * nchunk, t * LANES:(t + 1) * LANES]
    for b in range(bsz):
        o_ref[b] = y_ref[b] + d_ref[...] * u_ref[b]


def _s5_out(x, mods, s_in, bd, wcc, d_row, *, t):
    bsz, n, d = x.shape
    nchunk = t // SSM_CHUNK
    tile = pl.BlockSpec((bsz, t, LANES), lambda q, i: (0, i, q))
    return pl.pallas_call(
        functools.partial(_s5_out_kernel, bsz=bsz, nchunk=nchunk),
        out_shape=jax.ShapeDtypeStruct((bsz, n, d), F32),
        grid=(SSM_NQ, n // t),
        in_specs=[tile,
                  pl.BlockSpec((bsz, N_MODS, LANES), lambda q, i: (0, 0, q)),
                  pl.BlockSpec((SSM_VR, nchunk * SSM_ROWS, LANES), lambda q, i: (q, i, 0)),
                  pl.BlockSpec((1, 2, SSM_CHUNK, LANES, LANES), lambda q, i: (q, 0, 0, 0, 0)),
                  pl.BlockSpec((SSM_QG, SSM_TAP, 4 * LANES), lambda q, i: (q, 0, 0)),
                  pl.BlockSpec((1, LANES), lambda q, i: (0, q))],
        out_specs=tile,
        scratch_shapes=[pltpu.VMEM((SSM_W, SSM_W), BF16), pltpu.VMEM((SSM_W, 4 * SSM_HALF), BF16),
                        pltpu.VMEM((bsz, t, LANES), F32), pltpu.VMEM((bsz, t, LANES), F32)],
        compiler_params=_cparams("arbitrary", "arbitrary"),
        name="s5_chunk_out",
    )(x, mods, s_in, bd, wcc, d_row)


def _s5_mixer(x_lat, x_ctx, mods_lat, mods_ctx, lam_re, lam_im, log_dt, b_re, b_im, c_re, c_im, d_skip):
    bsz, n, _ = x_lat.shape
    assert 2 * bsz == SSM_ROWS, "state rows are (direction, batch) on the 8 sublanes"
    wbc, wcc, bd, a = _s5_prep(lam_re, lam_im, log_dt, b_re, b_im, c_re, c_im)
    t_lat = _tile(n, 1024)
    s_lat = _s5_states(x_lat, mods_lat, wbc, t=t_lat)
    s_ctx = _s5_states(x_ctx, mods_ctx, wbc, t=x_ctx.shape[1])
    a4 = a.reshape(SSM_NQ, SSM_QG, 2, 2, LANES)[..., :SSM_STATE]
    a_rows = jnp.repeat(a4.transpose(2, 0, 3, 1, 4).reshape(2, -1), bsz, axis=0)
    a_rows = a_rows.reshape(SSM_ROWS, SSM_NQ * SSM_VR, LANES).transpose(1, 0, 2)
    s_in = _s5_scan(s_ctx, s_lat, a_rows)
    return _s5_out(x_lat, mods_lat, s_in, bd, wcc, d_skip.reshape(1, -1), t=t_lat)


def _gmlp_kernel(x_ref, mod_ref, win_ref, bin_ref, lg_ref, lb_ref, ws_ref, bs_ref, wout_ref, g_ref, b_ref, o_ref):
    m = mod_ref[0]
    x = x_ref[0]
    t = x.shape[0]
    h = (x * (1.0 + m[1:2]) + m[0:1]).astype(BF16)
    z = jax.nn.gelu(_dot(h, win_ref[...]) + bin_ref[...])
    u = z[:, :GMLP_HALF]
    v = z[:, GMLP_HALF:]
    mu = jnp.mean(v, axis=-1, keepdims=True)
    vc = v - mu
    var = jnp.mean(vc * vc, axis=-1, keepdims=True)
    v = (vc * lax.rsqrt(var + LN_EPS) * lg_ref[...] + lb_ref[...]).astype(BF16)
    bs = bs_ref[...]
    rows = []
    for c in range(t // GMLP_CHUNK):
        r0 = c * GMLP_CHUNK
        cols = []
        for hd in range(GMLP_HEADS):
            c0 = hd * GMLP_HEAD_DIM
            gate = _dot(ws_ref[hd], v[r0:r0 + GMLP_CHUNK, c0:c0 + GMLP_HEAD_DIM]) + bs[:, hd:hd + 1]
            cols.append((u[r0:r0 + GMLP_CHUNK, c0:c0 + GMLP_HEAD_DIM] * gate).astype(BF16))
        rows.append(jnp.concatenate(cols, axis=1))
    gated = jnp.concatenate(rows, axis=0)
    y = _dot(gated, wout_ref[...])
    o_ref[0] = _post_norm(x, y, m[2:3], g_ref[...], b_ref[...])


def _gmlp(x, mods, w_in, b_in, lg, lb, w_s, b_s_t, w_out, ln_g, ln_b, *, t):
    bsz, n, d = x.shape
    tile = pl.BlockSpec((1, t, d), lambda b, i: (b, i, 0))
    const2 = lambda a: pl.BlockSpec(a.shape, lambda b, i: (0,) * a.ndim)
    return pl.pallas_call(
        _gmlp_kernel,
        out_shape=jax.ShapeDtypeStruct((bsz, n, d), F32),
        grid=(bsz, n // t),
        in_specs=[tile, pl.BlockSpec((1, N_MODS, d), lambda b, i: (b, 0, 0)),
                  const2(w_in), const2(b_in), const2(lg), const2(lb), const2(w_s), const2(b_s_t), const2(w_out),
                  const2(ln_g), const2(ln_b)],
        out_specs=tile,
        compiler_params=_cparams("arbitrary", "arbitrary"),
        name="gmlp_mixer",
    )(x, mods, w_in, b_in, lg, lb, w_s, b_s_t, w_out, ln_g, ln_b)


def _tile(n, pref):
    return pref if n % pref == 0 else n


def _layer(layer, x_lat, x_ctx, mods, p):
    bsz, _, d = x_lat.shape
    lctx = x_ctx.shape[1]
    row = lambda v: v.reshape(1, -1)
    t_lat, t_ctx = 512, _tile(lctx, 256)
    kind = MIXERS[layer % len(MIXERS)]
    j = layer // len(MIXERS)
    ctx_out = any(MIXERS[m % len(MIXERS)] in CTX_READING_MIXERS for m in range(layer + 1, DEPTH))
    m_lat = mods[layer, :bsz]
    m_ctx = jnp.broadcast_to(mods[layer, bsz:bsz + 1], (bsz, N_MODS, d))
    g1, b1 = row(p["ln1_g"][layer]), row(p["ln1_b"][layer])

    def ffn(xs, ms, t):
        return _ffn(xs, ms, p["ffn_w_up"][layer].astype(BF16), p["ffn_conv_w"][layer], row(p["ffn_conv_b"][layer]),
                    p["ffn_w_down"][layer].astype(BF16), row(p["ln2_g"][layer]), row(p["ln2_b"][layer]),
                    t=t, fc=FFN_HIDDEN // 2)

    if kind == "pool":
        args = (p["pool_w"][j].astype(BF16), row(p["pool_b"][j]), row(p["pool_scale"][j]), g1, b1)
        x_lat = _pool(x_lat, m_lat, *args, t=t_lat)
        if ctx_out:
            x_ctx = _pool(x_ctx, m_ctx, *args, t=t_ctx)
    elif kind == "attn":
        wqkv = p["attn_w_qkv"][j].astype(BF16)
        wo = p["attn_w_o"][j].astype(BF16)
        sink = p["attn_sink"][j].astype(F32)
        q, kd, vd = _qkv(x_lat, m_lat, wqkv, t=t_lat, rope=True)
        qc, kdc, vdc = _qkv(x_ctx, m_ctx, wqkv, t=t_ctx, rope=False)
        o_lat = _attn(sink, q, kd, vd, kdc, vdc, local=True)
        x_lat = _proj_norm(o_lat, x_lat, m_lat, wo, g1, b1, t=t_lat)
        if ctx_out:
            o_ctx = _attn(sink, qc, None, None, kdc, vdc, local=False)
            x_ctx = _proj_norm(o_ctx, x_ctx, m_ctx, wo, g1, b1, t=t_ctx)
    elif kind == "ssm":
        assert not ctx_out
        y = _s5_mixer(x_lat, x_ctx, m_lat, m_ctx, p["ssm_lambda_re"][j], p["ssm_lambda_im"][j], p["ssm_log_dt"][j],
                       p["ssm_b_re"][j], p["ssm_b_im"][j], p["ssm_c_re"][j], p["ssm_c_im"][j], p["ssm_d"][j])
        x_lat = _glu_norm(y, x_lat, m_lat, p["ssm_w_glu_a"][j].astype(BF16), p["ssm_w_glu_b"][j].astype(BF16),
                          g1, b1, t=t_lat)
    else:
        assert not ctx_out
        x_lat = _gmlp(x_lat, m_lat, p["gmlp_w_in"][j].astype(BF16), row(p["gmlp_b_in"][j]), row(p["gmlp_ln_g"][j]),
                      row(p["gmlp_ln_b"][j]), p["gmlp_w_s"][j].astype(BF16), p["gmlp_b_s"][j].T,
                      p["gmlp_w_out"][j].astype(BF16), g1, b1, t=256)
    x_lat = ffn(x_lat, m_lat, t_lat)
    if ctx_out:
        x_ctx = ffn(x_ctx, m_ctx, t_ctx)
    return x_lat, x_ctx


def _mods(c, c_ctx, ada_w, ada_b):
    bsz, d = c.shape
    cond = jnp.concatenate([c, c_ctx[None, :], jnp.zeros((8 - bsz - 1, d), F32)], axis=0)
    return _ada(cond, ada_w, ada_b).reshape(DEPTH, 8, N_MODS, d)


def kernel(x, c, ctx, c_ctx, ada_w, ada_b, ln1_g, ln1_b, ln2_g, ln2_b, ffn_w_up, ffn_conv_w, ffn_conv_b, ffn_w_down, pool_w, pool_b, pool_scale, attn_w_qkv, attn_w_o, attn_sink, ssm_lambda_re, ssm_lambda_im, ssm_log_dt, ssm_b_re, ssm_b_im, ssm_c_re, ssm_c_im, ssm_d, ssm_w_glu_a, ssm_w_glu_b, gmlp_w_in, gmlp_b_in, gmlp_ln_g, gmlp_ln_b, gmlp_w_s, gmlp_b_s, gmlp_w_out):
    bsz, n, d = x.shape
    assert d == D_MODEL and bsz < 8 and n % 512 == 0 and ctx.shape[1] % ATTN_BLOCK == 0
    p = dict(ln1_g=ln1_g, ln1_b=ln1_b, ln2_g=ln2_g, ln2_b=ln2_b, ffn_w_up=ffn_w_up, ffn_conv_w=ffn_conv_w,
             ffn_conv_b=ffn_conv_b, ffn_w_down=ffn_w_down, pool_w=pool_w, pool_b=pool_b, pool_scale=pool_scale,
             attn_w_qkv=attn_w_qkv, attn_w_o=attn_w_o, attn_sink=attn_sink, ssm_lambda_re=ssm_lambda_re,
             ssm_lambda_im=ssm_lambda_im, ssm_log_dt=ssm_log_dt, ssm_b_re=ssm_b_re, ssm_b_im=ssm_b_im,
             ssm_c_re=ssm_c_re, ssm_c_im=ssm_c_im, ssm_d=ssm_d, ssm_w_glu_a=ssm_w_glu_a, ssm_w_glu_b=ssm_w_glu_b,
             gmlp_w_in=gmlp_w_in, gmlp_b_in=gmlp_b_in, gmlp_ln_g=gmlp_ln_g, gmlp_ln_b=gmlp_ln_b, gmlp_w_s=gmlp_w_s,
             gmlp_b_s=gmlp_b_s, gmlp_w_out=gmlp_w_out)
    mods = _mods(c, c_ctx, ada_w, ada_b)
    x_lat, x_ctx = x, ctx
    for layer in range(DEPTH):
        x_lat, x_ctx = _layer(layer, x_lat, x_ctx, mods, p)
    return x_lat
```

```python
import functools
import math

import jax
import jax.numpy as jnp
from jax import lax
from jax.experimental import pallas as pl
from jax.experimental.pallas import tpu as pltpu

F32 = jnp.float32
BF16 = jnp.bfloat16

D_MODEL = 1024
DEPTH = 4
MIXERS = ("pool", "attn", "ssm", "gmlp")
CTX_READING_MIXERS = ("attn", "ssm")
GRID_W = 64
N_MODS = 6
DEEPNORM_ALPHA = (2.0 * DEPTH) ** 0.25
LN_EPS = 1e-5

POOL_WINDOWS = (2, 4, 8, 16)
POOL_GROUP = D_MODEL // len(POOL_WINDOWS)

HEAD_DIM = 64
N_Q_HEADS = D_MODEL // HEAD_DIM
N_KV_HEADS = N_Q_HEADS // 4
Q_WIDTH = N_Q_HEADS * HEAD_DIM
KV_WIDTH = N_KV_HEADS * HEAD_DIM
WINDOW = 128
ATTN_BLOCK = 128
ROPE_BASE = 10000.0
NEG_INF = -1e30

SSM_GROUP = 16
SSM_N_GROUPS = D_MODEL // SSM_GROUP
SSM_STATE = 64
SSM_CHUNK = 16

GMLP_CHUNK = 128
GMLP_HALF = 2 * D_MODEL
GMLP_HEADS = 8
GMLP_HEAD_DIM = GMLP_HALF // GMLP_HEADS

FFN_HIDDEN = 2816
FFN_CHUNK = 256

LANES = 128
HALO = 16
VMEM_LIMIT = 56 * 1024 * 1024


def _cparams(*sem):
    return pltpu.CompilerParams(dimension_semantics=sem, vmem_limit_bytes=VMEM_LIMIT)


def _const_spec(shape):
    nd = len(shape)
    return pl.BlockSpec(shape, lambda *_: (0,) * nd, pipeline_mode=pl.Buffered(1))


def _post_norm(x, y, gate, g, b):
    z = DEEPNORM_ALPHA * x + gate * y
    mu = jnp.mean(z, axis=-1, keepdims=True)
    zc = z - mu
    var = jnp.mean(zc * zc, axis=-1, keepdims=True)
    return zc * lax.rsqrt(var + LN_EPS) * g + b


def _dot(a, b):
    return jnp.dot(a, b, preferred_element_type=F32)


def _dot_nt(a, b):
    return lax.dot_general(a, b, (((1,), (1,)), ((), ())), preferred_element_type=F32)


def _ada_kernel(c_ref, w_ref, b_ref, o_ref):
    c = c_ref[...]
    s = (c * jax.nn.sigmoid(c)).astype(BF16)
    o_ref[0] = _dot(s, w_ref[0].astype(BF16)) + b_ref[0]


def _ada(cond, ada_w, ada_b):
    depth, d, n = ada_w.shape
    rows = cond.shape[0]
    tn = 1536
    return pl.pallas_call(
        _ada_kernel,
        out_shape=jax.ShapeDtypeStruct((depth, rows, n), F32),
        grid=(depth, n // tn),
        in_specs=[pl.BlockSpec((rows, d), lambda l, j: (0, 0)),
                  pl.BlockSpec((1, d, tn), lambda l, j: (l, 0, j)),
                  pl.BlockSpec((1, 1, tn), lambda l, j: (l, 0, j))],
        out_specs=pl.BlockSpec((1, rows, tn), lambda l, j: (l, 0, j)),
        compiler_params=_cparams("arbitrary", "arbitrary"),
        name="ada",
    )(cond, ada_w, ada_b.reshape(depth, 1, n))


def _halo_specs(t, n):
    per = t // HALO
    last = n // HALO - 1
    prev = pl.BlockSpec((1, HALO, D_MODEL), lambda b, i, *_: (b, jnp.maximum(i * per - 1, 0), 0))
    nxt = pl.BlockSpec((1, HALO, D_MODEL), lambda b, i, *_: (b, jnp.minimum((i + 1) * per, last), 0))
    return prev, nxt


def _ffn_kernel(xp_ref, x_ref, xn_ref, mod_ref, wu_ref, cw_ref, cb_ref, wd_ref, g_ref, b_ref, o_ref, h_ref, act_ref,
                *, t, nt, f, cc):
    i = pl.program_id(1)
    rows = t + 2 * HALO
    m = mod_ref[0]
    sh, sc = m[3:4], 1.0 + m[4:5]
    keep_p = jnp.where(i > 0, 1.0, 0.0)
    keep_n = jnp.where(i < nt - 1, 1.0, 0.0)
    x = x_ref[0]
    h_ref[0:HALO, :] = ((xp_ref[0] * sc + sh) * keep_p).astype(BF16)
    h_ref[HALO:HALO + t, :] = (x * sc + sh).astype(BF16)
    h_ref[HALO + t:rows, :] = ((xn_ref[0] * sc + sh) * keep_n).astype(BF16)
    h = h_ref[...]

    def conv(off):
        u = _dot(h, wu_ref[:, off:off + cc])
        cw = cw_ref[:, off:off + cc]
        a = cb_ref[:, off:off + cc] + pltpu.roll(u, 1, 0) * cw[0:1]
        a = a + u * cw[1:2]
        a = a + pltpu.roll(u, rows - 1, 0) * cw[2:3]
        return a[HALO:HALO + t]

    for c in range(f // cc):
        val = conv(c * cc)
        gate = conv(f + c * cc)
        act_ref[:, c * cc:(c + 1) * cc] = (val * (gate * jax.nn.sigmoid(gate))).astype(BF16)
    y = _dot(act_ref[...], wd_ref[...])
    o_ref[0] = _post_norm(x, y, m[5:6], g_ref[...], b_ref[...])


def _ffn(x, mods, w_up, conv_w, conv_b, w_down, ln_g, ln_b, *, t, cc):
    bsz, n, d = x.shape
    f = w_down.shape[0]
    nt = n // t
    prev, nxt = _halo_specs(t, n)
    kern = functools.partial(_ffn_kernel, t=t, nt=nt, f=f, cc=cc)
    return pl.pallas_call(
        kern,
        out_shape=jax.ShapeDtypeStruct((bsz, n, d), F32),
        grid=(bsz, nt),
        in_specs=[prev,
                  pl.BlockSpec((1, t, d), lambda b, i: (b, i, 0)),
                  nxt,
                  pl.BlockSpec((1, N_MODS, d), lambda b, i: (b, 0, 0)),
                  _const_spec(w_up.shape), _const_spec(conv_w.shape), _const_spec(conv_b.shape),
                  _const_spec(w_down.shape), _const_spec(ln_g.shape), _const_spec(ln_b.shape)],
        out_specs=pl.BlockSpec((1, t, d), lambda b, i: (b, i, 0)),
        scratch_shapes=[pltpu.VMEM((t + 2 * HALO, d), BF16), pltpu.VMEM((t, f), BF16)],
        compiler_params=_cparams("arbitrary", "arbitrary"),
        name="conv_ffn",
    )(x, x, x, mods, w_up, conv_w, conv_b, w_down, ln_g, ln_b)


def _pool_kernel(xp_ref, x_ref, xn_ref, mod_ref, w_ref, pb_ref, ps_ref, g_ref, b_ref, o_ref, *, t, nt, n):
    i = pl.program_id(1)
    rows = t + 2 * HALO
    m = mod_ref[0]
    sh, sc = m[0:1], 1.0 + m[1:2]
    keep_p = jnp.where(i > 0, 1.0, 0.0)
    keep_n = jnp.where(i < nt - 1, 1.0, 0.0)
    x = x_ref[0]
    h = jnp.concatenate([(xp_ref[0] * sc + sh) * keep_p, x * sc + sh, (xn_ref[0] * sc + sh) * keep_n], axis=0)
    pos = i * t + lax.broadcasted_iota(jnp.int32, (t, 1), 0)
    outs = []
    for gi, win in enumerate(POOL_WINDOWS):
        hg = h[:, gi * POOL_GROUP:(gi + 1) * POOL_GROUP]
        s = hg + pltpu.roll(hg, 1, 0)
        half = 1
        while 2 * half < win:
            s = pltpu.roll(s, half, 0) + pltpu.roll(s, rows - half, 0)
            half *= 2
        lo = jnp.maximum(pos - win // 2, 0)
        hi = jnp.minimum(pos - win // 2 + win, n)
        mean = s[HALO:HALO + t] / (hi - lo).astype(F32)
        mixed = (mean - hg[HALO:HALO + t]).astype(BF16)
        outs.append(_dot(mixed, w_ref[gi]))
    y = (jnp.concatenate(outs, axis=1) + pb_ref[...]) * ps_ref[...]
    o_ref[0] = _post_norm(x, y, m[2:3], g_ref[...], b_ref[...])


def _pool(x, mods, w, pb, ps, ln_g, ln_b, *, t):
    bsz, n, d = x.shape
    nt = n // t
    prev, nxt = _halo_specs(t, n)
    kern = functools.partial(_pool_kernel, t=t, nt=nt, n=n)
    vec = pl.BlockSpec((1, d), lambda b, i: (0, 0))
    return pl.pallas_call(
        kern,
        out_shape=jax.ShapeDtypeStruct((bsz, n, d), F32),
        grid=(bsz, nt),
        in_specs=[prev, pl.BlockSpec((1, t, d), lambda b, i: (b, i, 0)), nxt,
                  pl.BlockSpec((1, N_MODS, d), lambda b, i: (b, 0, 0)),
                  pl.BlockSpec(w.shape, lambda b, i: (0, 0, 0)),
                  vec, vec, vec, vec],
        out_specs=pl.BlockSpec((1, t, d), lambda b, i: (b, i, 0)),
        compiler_params=_cparams("arbitrary", "arbitrary"),
        name="pool_mixer",
    )(x, x, x, mods, w, pb, ps, ln_g, ln_b)


def _rope_tables(n):
    tpos = jnp.arange(n)
    half = HEAD_DIM // 4
    freqs = ROPE_BASE ** (-jnp.arange(half, dtype=F32) / half)
    ang_r = (tpos // GRID_W).astype(F32)[:, None] * freqs[None, :]
    ang_c = (tpos % GRID_W).astype(F32)[:, None] * freqs[None, :]
    zero = jnp.zeros_like(ang_r)
    cr, sr, cc, sc = jnp.cos(ang_r), jnp.sin(ang_r), jnp.cos(ang_c), jnp.sin(ang_c)
    cos = jnp.tile(jnp.concatenate([cr, cr, cc, cc], axis=1), (1, 2))
    sin_first = jnp.tile(jnp.concatenate([-sr, zero, -sc, zero], axis=1), (1, 2))
    sin_second = jnp.tile(jnp.concatenate([zero, sr, zero, sc], axis=1), (1, 2))
    return cos, sin_first, sin_second


def _dup_heads(chunk, lo):
    sw = pltpu.roll(chunk, HEAD_DIM, 1)
    return jnp.where(lo, chunk, sw), jnp.where(lo, sw, chunk)


def _qkv_kernel(*refs, rope):
    if rope:
        x_ref, mod_ref, w_ref, cos_ref, sa_ref, sb_ref, q_ref, kd_ref, vd_ref = refs
    else:
        x_ref, mod_ref, w_ref, q_ref, kd_ref, vd_ref = refs
    m = mod_ref[0]
    h = (x_ref[0] * (1.0 + m[1:2]) + m[0:1]).astype(BF16)
    qkv = _dot(h, w_ref[...])
    t = qkv.shape[0]
    lo = lax.broadcasted_iota(jnp.int32, (t, LANES), 1) < HEAD_DIM
    scale = HEAD_DIM ** -0.5

    def rot(v):
        if not rope:
            return v
        quarter = HEAD_DIM // 4
        return (v * cos_ref[...] + pltpu.roll(v, LANES - quarter, 1) * sa_ref[...]
                + pltpu.roll(v, quarter, 1) * sb_ref[...])

    for c in range(Q_WIDTH // LANES):
        q_ref[0, :, c * LANES:(c + 1) * LANES] = (rot(qkv[:, c * LANES:(c + 1) * LANES]) * scale).astype(BF16)
    for c in range(KV_WIDTH // LANES):
        k0, k1 = _dup_heads(rot(qkv[:, Q_WIDTH + c * LANES:Q_WIDTH + (c + 1) * LANES]), lo)
        kd_ref[0, :, (2 * c) * LANES:(2 * c + 1) * LANES] = k0.astype(BF16)
        kd_ref[0, :, (2 * c + 1) * LANES:(2 * c + 2) * LANES] = k1.astype(BF16)
        v0, v1 = _dup_heads(qkv[:, Q_WIDTH + KV_WIDTH + c * LANES:Q_WIDTH + KV_WIDTH + (c + 1) * LANES], lo)
        vd_ref[0, :, (2 * c) * LANES:(2 * c + 1) * LANES] = v0.astype(BF16)
        vd_ref[0, :, (2 * c + 1) * LANES:(2 * c + 2) * LANES] = v1.astype(BF16)


def _qkv(x, mods, w_qkv, *, t, rope):
    bsz, n, d = x.shape
    kdw = N_KV_HEADS * LANES
    ins = [x, mods, w_qkv]
    specs = [pl.BlockSpec((1, t, d), lambda b, i: (b, i, 0)),
             pl.BlockSpec((1, N_MODS, d), lambda b, i: (b, 0, 0)),
             pl.BlockSpec(w_qkv.shape, lambda b, i: (0, 0))]
    if rope:
        ins += list(_rope_tables(n))
        specs += [pl.BlockSpec((t, LANES), lambda b, i: (i, 0))] * 3
    return pl.pallas_call(
        functools.partial(_qkv_kernel, rope=rope),
        out_shape=(jax.ShapeDtypeStruct((bsz, n, Q_WIDTH), BF16),
                   jax.ShapeDtypeStruct((bsz, n, kdw), BF16),
                   jax.ShapeDtypeStruct((bsz, n, kdw), BF16)),
        grid=(bsz, n // t),
        in_specs=specs,
        out_specs=(pl.BlockSpec((1, t, Q_WIDTH), lambda b, i: (b, i, 0)),
                   pl.BlockSpec((1, t, kdw), lambda b, i: (b, i, 0)),
                   pl.BlockSpec((1, t, kdw), lambda b, i: (b, i, 0))),
        compiler_params=_cparams("arbitrary", "arbitrary"),
        name="qkv_rope" if rope else "qkv_ctx",
    )(*ins)


def _attn_kernel(sink_ref, q_ref, *refs, nb, local):
    if local:
        kp_ref, kc_ref, kn_ref, kx_ref, vp_ref, vc_ref, vn_ref, vx_ref, o_ref = refs
        k_refs, v_refs = (kp_ref, kc_ref, kn_ref, kx_ref), (vp_ref, vc_ref, vn_ref, vx_ref)
    else:
        kx_ref, vx_ref, o_ref = refs
        k_refs, v_refs = (kx_ref,), (vx_ref,)
    blk = ATTN_BLOCK
    group = N_Q_HEADS // N_KV_HEADS
    rows = group * blk
    n_keys = sum(r.shape[1] for r in k_refs)
    nblk = pl.program_id(1)
    if local:
        qi = lax.broadcasted_iota(jnp.int32, (rows, n_keys), 0) % blk
        kj = lax.broadcasted_iota(jnp.int32, (rows, n_keys), 1)
        rel = kj - blk - qi
        valid = (jnp.abs(rel) <= WINDOW) & ((kj >= blk) | (nblk > 0)) & ((kj < 2 * blk) | (nblk < nb - 1))
        valid = valid | (kj >= 3 * blk)
    lo = lax.broadcasted_iota(jnp.int32, (blk, LANES), 1) < HEAD_DIM
    head_of_row = lax.broadcasted_iota(jnp.int32, (rows, 1), 0) // blk
    for hk in range(N_KV_HEADS):
        kx = jnp.concatenate([r[0, :, hk * LANES:(hk + 1) * LANES] for r in k_refs], axis=0)
        vx = jnp.concatenate([r[0, :, hk * LANES:(hk + 1) * LANES] for r in v_refs], axis=0)
        parts = []
        for c in range(group // 2):
            qc = q_ref[0, :, (hk * group // 2 + c) * LANES:(hk * group // 2 + c + 1) * LANES]
            zero = jnp.zeros_like(qc)
            parts += [jnp.where(lo, qc, zero), jnp.where(lo, zero, qc)]
        q4 = jnp.concatenate(parts, axis=0)
        s = _dot_nt(q4, kx)
        if local:
            s = jnp.where(valid, s, NEG_INF)
        sink = jnp.zeros((rows, 1), F32)
        for g in range(group):
            sink = jnp.where(head_of_row == g, sink_ref[hk * group + g], sink)
        mx = jnp.maximum(jnp.max(s, axis=-1, keepdims=True), sink)
        p = jnp.exp(s - mx)
        den = jnp.sum(p, axis=-1, keepdims=True) + jnp.exp(sink - mx)
        o4 = _dot(p.astype(BF16), vx) / den
        for c in range(group // 2):
            oc = jnp.where(lo, o4[(2 * c) * blk:(2 * c + 1) * blk], o4[(2 * c + 1) * blk:(2 * c + 2) * blk])
            o_ref[0, :, (hk * group // 2 + c) * LANES:(hk * group // 2 + c + 1) * LANES] = oc.astype(BF16)


def _attn(sink, q, kd, vd, kd_ctx, vd_ctx, *, local):
    bsz, n, _ = q.shape
    blk = ATTN_BLOCK
    nb = n // blk
    lctx = kd_ctx.shape[1]
    kdw = kd_ctx.shape[2]
    smem = pl.BlockSpec(memory_space=pltpu.SMEM)
    qspec = pl.BlockSpec((1, blk, Q_WIDTH), lambda b, i: (b, i, 0))
    ctx_spec = pl.BlockSpec((1, lctx, kdw), lambda b, i: (b, 0, 0))
    if local:
        band = [pl.BlockSpec((1, blk, kdw), lambda b, i: (b, jnp.maximum(i - 1, 0), 0)),
                pl.BlockSpec((1, blk, kdw), lambda b, i: (b, i, 0)),
                pl.BlockSpec((1, blk, kdw), lambda b, i: (b, jnp.minimum(i + 1, nb - 1), 0))]
        specs = [smem, qspec] + band + [ctx_spec] + band + [ctx_spec]
        args = (sink, q, kd, kd, kd, kd_ctx, vd, vd, vd, vd_ctx)
    else:
        specs = [smem, qspec, ctx_spec, ctx_spec]
        args = (sink, q, kd_ctx, vd_ctx)
    return pl.pallas_call(
        functools.partial(_attn_kernel, nb=nb, local=local),
        out_shape=jax.ShapeDtypeStruct((bsz, n, Q_WIDTH), BF16),
        grid=(bsz, nb),
        in_specs=specs,
        out_specs=pl.BlockSpec((1, blk, Q_WIDTH), lambda b, i: (b, i, 0)),
        compiler_params=_cparams("arbitrary", "arbitrary"),
        name="banded_attn" if local else "ctx_attn",
    )(*args)


def _proj_norm_kernel(a_ref, x_ref, mod_ref, w_ref, g_ref, b_ref, o_ref):
    y = _dot(a_ref[0], w_ref[...])
    o_ref[0] = _post_norm(x_ref[0], y, mod_ref[0][2:3], g_ref[...], b_ref[...])


def _proj_norm(a, x, mods, w, ln_g, ln_b, *, t):
    bsz, n, d = x.shape
    ka = a.shape[2]
    vec = pl.BlockSpec((1, d), lambda b, i: (0, 0))
    return pl.pallas_call(
        _proj_norm_kernel,
        out_shape=jax.ShapeDtypeStruct((bsz, n, d), F32),
        grid=(bsz, n // t),
        in_specs=[pl.BlockSpec((1, t, ka), lambda b, i: (b, i, 0)),
                  pl.BlockSpec((1, t, d), lambda b, i: (b, i, 0)),
                  pl.BlockSpec((1, N_MODS, d), lambda b, i: (b, 0, 0)),
                  pl.BlockSpec(w.shape, lambda b, i: (0, 0)),
                  vec, vec],
        out_specs=pl.BlockSpec((1, t, d), lambda b, i: (b, i, 0)),
        compiler_params=_cparams("arbitrary", "arbitrary"),
        name="attn_out_norm",
    )(a, x, mods, w, ln_g, ln_b)


def _glu_norm_kernel(y_ref, x_ref, mod_ref, wa_ref, wb_ref, g_ref, b_ref, o_ref):
    gl = jax.nn.gelu(y_ref[0]).astype(BF16)
    out = _dot(gl, wa_ref[...]) * jax.nn.sigmoid(_dot(gl, wb_ref[...]))
    o_ref[0] = _post_norm(x_ref[0], out, mod_ref[0][2:3], g_ref[...], b_ref[...])


def _glu_norm(y, x, mods, wa, wb, ln_g, ln_b, *, t):
    bsz, n, d = x.shape
    tile = pl.BlockSpec((1, t, d), lambda b, i: (b, i, 0))
    vec = pl.BlockSpec((1, d), lambda b, i: (0, 0))
    wspec = pl.BlockSpec((d, d), lambda b, i: (0, 0))
    return pl.pallas_call(
        _glu_norm_kernel,
        out_shape=jax.ShapeDtypeStruct((bsz, n, d), F32),
        grid=(bsz, n // t),
        in_specs=[tile, tile, pl.BlockSpec((1, N_MODS, d), lambda b, i: (b, 0, 0)), wspec, wspec, vec, vec],
        out_specs=tile,
        compiler_params=_cparams("arbitrary", "arbitrary"),
        name="ssm_glu_norm",
    )(y, x, mods, wa, wb, ln_g, ln_b)


SSM_QG = LANES // SSM_GROUP
SSM_NQ = D_MODEL // LANES
SSM_ROWS = 8
SSM_W = SSM_CHUNK * LANES
SSM_HALF = SSM_QG * SSM_STATE
SSM_TAP = SSM_CHUNK * SSM_GROUP
SSM_VR = 2 * SSM_HALF // LANES


def _s5_prep_kernel(lr_ref, li_ref, ldt_ref, btr_ref, bti_ref, cr_ref, ci_ref, wb_ref, wc_ref, bd_ref, a_ref):
    c = SSM_CHUNK
    q = pl.program_id(0)
    lag = lax.broadcasted_iota(jnp.int32, (c + 1, LANES), 0).astype(F32)
    lane = lax.broadcasted_iota(jnp.int32, (1, LANES), 1)
    taps = [[[] for _ in range(c)] for _ in range(2)]
    for g8 in range(SSM_QG):
        own = jnp.where((lane < SSM_STATE) == (g8 % 2 == 0), 1.0, 0.0)
        wb_cols, wc_cols, a_cols = [], [], []
        for d in range(2):
            lr, li = lr_ref[g8, d], li_ref[g8, d]
            dt = jnp.exp(jnp.full((1, LANES), ldt_ref[q * SSM_QG + g8, d], F32))
            mag = jnp.exp(lag * (lr * dt))
            ang = lag * (li * dt)
            pw_r, pw_i = mag * jnp.cos(ang), mag * jnp.sin(ang)
            lbr, lbi = pw_r[1:2], pw_i[1:2]
            den = lr * lr + li * li
            qr = ((lbr - 1.0) * lr + lbi * li) / den
            qi = (lbi * lr - (lbr - 1.0) * li) / den
            btr, bti = btr_ref[g8, d], bti_ref[g8, d]
            bbr = qr * btr - qi * bti
            bbi = qr * bti + qi * btr
            cr, ci = cr_ref[g8, d], ci_ref[g8, d]

            def cl(j):
                return cr * pw_r[j:j + 1] - ci * pw_i[j:j + 1], -(cr * pw_i[j:j + 1] + ci * pw_r[j:j + 1])

            def bl(j):
                return bbr * pw_r[j:j + 1] - bbi * pw_i[j:j + 1], bbr * pw_i[j:j + 1] + bbi * pw_r[j:j + 1]

            e = jnp.concatenate([jnp.concatenate(cl(j), axis=1) for j in range(c)], axis=0)
            pieces = [jnp.concatenate([bbr * own, bbi * own], axis=1)]
            if g8 > 0:
                pieces.insert(0, jnp.zeros((g8 * SSM_GROUP, 2 * LANES), F32))
            if g8 < SSM_QG - 1:
                pieces.append(jnp.zeros(((SSM_QG - 1 - g8) * SSM_GROUP, 2 * LANES), F32))
            kt = lax.dot_general(e, jnp.concatenate(pieces, axis=0), (((1,), (1,)), ((), ())),
                                 preferred_element_type=F32, precision=lax.Precision.HIGHEST)
            for j in range(c):
                taps[d][j].append(kt[j * SSM_GROUP:(j + 1) * SSM_GROUP])
            wbl = [bl(c - 1 - k) if d == 0 else bl(k) for k in range(c)]
            wb_cols += [jnp.concatenate([w[0] for w in wbl], axis=0) * own,
                        jnp.concatenate([w[1] for w in wbl], axis=0) * own]
            wcl = [cl(k + 1) if d == 0 else cl(c - k) for k in range(c)]
            wc_cols += [jnp.concatenate([w[0] for w in wcl], axis=0) * own,
                        jnp.concatenate([w[1] for w in wcl], axis=0) * own]
            a_cols += [pw_r[c:c + 1], pw_i[c:c + 1]]
        wb_ref[g8] = jnp.concatenate(wb_cols, axis=1).astype(BF16)
        wc_ref[g8] = jnp.concatenate(wc_cols, axis=1).astype(BF16)
        a_ref[g8] = jnp.concatenate(a_cols, axis=1)
    for d in range(2):
        for j in range(c):
            blk = jnp.concatenate(taps[d][j], axis=0)
            if d == 0 and j == 0:
                blk = blk + jnp.concatenate(taps[1][0], axis=0)
            bd_ref[0, d, j] = blk.astype(BF16)


def _s5_prep(lam_re, lam_im, log_dt, b_re, b_im, c_re, c_im):
    g, nq, qg = SSM_N_GROUPS, SSM_NQ, SSM_QG
    per_g = lambda a: jnp.swapaxes(a, 0, 1)
    dup = lambda a: jnp.concatenate([a, a], axis=-1)
    lam_spec = pl.BlockSpec((qg, 2, 1, LANES), lambda i: (i, 0, 0, 0))
    mat_spec = pl.BlockSpec((qg, 2, SSM_GROUP, LANES), lambda i: (i, 0, 0, 0))
    w_spec = pl.BlockSpec((qg, SSM_TAP, 4 * LANES), lambda i: (i, 0, 0))
    return pl.pallas_call(
        _s5_prep_kernel,
        out_shape=(jax.ShapeDtypeStruct((g, SSM_TAP, 4 * LANES), BF16),
                   jax.ShapeDtypeStruct((g, SSM_TAP, 4 * LANES), BF16),
                   jax.ShapeDtypeStruct((nq, 2, SSM_CHUNK, LANES, LANES), BF16),
                   jax.ShapeDtypeStruct((g, 1, 4 * LANES), F32)),
        grid=(nq,),
        in_specs=[lam_spec, lam_spec, pl.BlockSpec(memory_space=pltpu.SMEM), mat_spec, mat_spec, mat_spec, mat_spec],
        out_specs=(w_spec, w_spec, pl.BlockSpec((1, 2, SSM_CHUNK, LANES, LANES), lambda i: (i, 0, 0, 0, 0)),
                   pl.BlockSpec((qg, 1, 4 * LANES), lambda i: (i, 0, 0))),
        compiler_params=_cparams("arbitrary"),
        name="s5_prep",
    )(dup(per_g(lam_re))[:, :, None, :], dup(per_g(lam_im))[:, :, None, :], per_g(log_dt),
      dup(jnp.swapaxes(per_g(b_re), 2, 3)), dup(jnp.swapaxes(per_g(b_im), 2, 3)), dup(per_g(c_re)), dup(per_g(c_im)))


def _s5_expand(dst_ref, src_ref):
    dst_ref[...] = jnp.zeros_like(dst_ref)
    for g8 in range(SSM_QG):
        for k in range(SSM_CHUNK):
            for c4 in range(4):
                r0 = k * LANES + g8 * SSM_GROUP
                c0 = c4 * SSM_HALF + (g8 // 2) * LANES
                dst_ref[r0:r0 + SSM_GROUP, c0:c0 + LANES] = src_ref[g8, k * SSM_GROUP:(k + 1) * SSM_GROUP,
                                                                    c4 * LANES:(c4 + 1) * LANES]


def _s5_modulate(x_ref, mod_ref, u_ref, bsz):
    for b in range(bsz):
        m = mod_ref[b]
        u_ref[b] = x_ref[b] * (1.0 + m[1:2]) + m[0:1]


def _s5_gather(u_ref, bsz, nchunk):
    rows = [jnp.concatenate([u_ref[b, pl.ds(pos, nchunk, stride=SSM_CHUNK), :] for pos in range(SSM_CHUNK)],
                            axis=1).astype(BF16) for b in range(bsz)]
    return jnp.concatenate(rows, axis=0)


def _s5_states_kernel(x_ref, mod_ref, wbc_ref, o_ref, wb_ref, u_ref, *, bsz, nchunk):
    @pl.when(pl.program_id(1) == 0)
    def _():
        _s5_expand(wb_ref, wbc_ref)

    _s5_modulate(x_ref, mod_ref, u_ref, bsz)
    s = _dot(_s5_gather(u_ref, bsz, nchunk), wb_ref[...])
    for b in range(bsz):
        for v in range(SSM_VR):
            for d in range(2):
                o_ref[v, pl.ds(d * bsz + b, nchunk, stride=SSM_ROWS), :] = (
                    s[b * nchunk:(b + 1) * nchunk, (d * SSM_VR + v) * LANES:(d * SSM_VR + v + 1) * LANES])


def _s5_states(x, mods, wbc, *, t):
    bsz, n, _ = x.shape
    nchunk = t // SSM_CHUNK
    return pl.pallas_call(
        functools.partial(_s5_states_kernel, bsz=bsz, nchunk=nchunk),
        out_shape=jax.ShapeDtypeStruct((SSM_NQ * SSM_VR, n // SSM_CHUNK * SSM_ROWS, LANES), F32),
        grid=(SSM_NQ, n // t),
        in_specs=[pl.BlockSpec((bsz, t, LANES), lambda q, i: (0, i, q)),
                  pl.BlockSpec((bsz, N_MODS, LANES), lambda q, i: (0, 0, q)),
                  pl.BlockSpec((SSM_QG, SSM_TAP, 4 * LANES), lambda q, i: (q, 0, 0))],
        out_specs=pl.BlockSpec((SSM_VR, nchunk * SSM_ROWS, LANES), lambda q, i: (q, i, 0)),
        scratch_shapes=[pltpu.VMEM((SSM_W, 4 * SSM_HALF), BF16), pltpu.VMEM((bsz, t, LANES), F32)],
        compiler_params=_cparams("arbitrary", "arbitrary"),
        name="s5_chunk_states",
    )(x, mods, wbc)


def _s5_scan_kernel(xc_ref, xl_ref, a_ref, o_ref):
    h = SSM_VR // 2
    ar, ai = a_ref[:h], a_ref[h:]
    nc, nl = xc_ref.shape[1] // SSM_ROWS, xl_ref.shape[1] // SSM_ROWS
    fwd_rows = lax.broadcasted_iota(jnp.int32, (SSM_VR, SSM_ROWS, LANES), 1) < SSM_ROWS // 2

    def step(ref, i, s):
        x = ref[:, pl.ds(pl.multiple_of(i * SSM_ROWS, SSM_ROWS), SSM_ROWS), :]
        return ar * s[0] - ai * s[1] + x[:h], ar * s[1] + ai * s[0] + x[h:]

    def fwd(i, s):
        o_ref[:, pl.ds(pl.multiple_of(i * SSM_ROWS, SSM_ROWS), SSM_ROWS), :] = jnp.concatenate(s, axis=0)
        return step(xl_ref, i, s)

    def bwd(k, s):
        rows = pl.ds(pl.multiple_of((nl - 1 - k) * SSM_ROWS, SSM_ROWS), SSM_ROWS)
        o_ref[:, rows, :] = jnp.where(fwd_rows, o_ref[:, rows, :], jnp.concatenate(s, axis=0))
        return step(xl_ref, nl - 1 - k, s)

    zero = (jnp.zeros((h, SSM_ROWS, LANES), F32), jnp.zeros((h, SSM_ROWS, LANES), F32))
    s = lax.fori_loop(0, nc, lambda i, s: step(xc_ref, i, s), zero)
    lax.fori_loop(0, nl, fwd, s)
    s = lax.fori_loop(0, nc, lambda k, s: step(xc_ref, nc - 1 - k, s), zero)
    lax.fori_loop(0, nl, bwd, s)


def _s5_scan(s_ctx, s_lat, a_rows):
    blk = lambda rows: pl.BlockSpec((SSM_VR, rows, LANES), lambda q: (q, 0, 0))
    return pl.pallas_call(
        _s5_scan_kernel,
        out_shape=jax.ShapeDtypeStruct(s_lat.shape, F32),
        grid=(SSM_NQ,),
        in_specs=[blk(s_ctx.shape[1]), blk(s_lat.shape[1]), blk(SSM_ROWS)],
        out_specs=blk(s_lat.shape[1]),
        compiler_params=_cparams("arbitrary"),
        name="s5_chunk_scan",
    )(s_ctx, s_lat, a_rows)


def _s5_out_kernel(x_ref, mod_ref, s_ref, bd_ref, wcc_ref, d_ref, o_ref, toep_ref, wc_ref, u_ref, y_ref, *,
                   bsz, nchunk):
    c = SSM_CHUNK

    @pl.when(pl.program_id(1) == 0)
    def _():
        _s5_expand(wc_ref, wcc_ref)
        for t in range(c):
            for k in range(c):
                toep_ref[t * LANES:(t + 1) * LANES, k * LANES:(k + 1) * LANES] = (
                    bd_ref[0, 0, t - k] if t >= k else bd_ref[0, 1, k - t])

    _s5_modulate(x_ref, mod_ref, u_ref, bsz)
    lhs_s = jnp.concatenate(
        [jnp.concatenate([s_ref[v, pl.ds(d * bsz + b, nchunk, stride=SSM_ROWS), :]
                          for d in range(2) for v in range(SSM_VR)], axis=1).astype(BF16)
         for b in range(bsz)], axis=0)
    y = _dot_nt(_s5_gather(u_ref, bsz, nchunk), toep_ref[...]) + _dot_nt(lhs_s, wc_ref[...])
    for b in range(bsz):
        for t in range(c):
            y_ref[b, pl.ds(t, nchunk, stride=c), :] = y[b * nchunk:(b + 1) * nchunk, t * LANES:(t + 1) * LANES]
    for b in range(bsz):
        o_ref[b] = y_ref[b] + d_ref[...] * u_ref[b]


def _s5_out(x, mods, s_in, bd, wcc, d_row, *, t):
    bsz, n, d = x.shape
    nchunk = t // SSM_CHUNK
    tile = pl.BlockSpec((bsz, t, LANES), lambda q, i: (0, i, q))
    return pl.pallas_call(
        functools.partial(_s5_out_kernel, bsz=bsz, nchunk=nchunk),
        out_shape=jax.ShapeDtypeStruct((bsz, n, d), F32),
        grid=(SSM_NQ, n // t),
        in_specs=[tile,
                  pl.BlockSpec((bsz, N_MODS, LANES), lambda q, i: (0, 0, q)),
                  pl.BlockSpec((SSM_VR, nchunk * SSM_ROWS, LANES), lambda q, i: (q, i, 0)),
                  pl.BlockSpec((1, 2, SSM_CHUNK, LANES, LANES), lambda q, i: (q, 0, 0, 0, 0)),
                  pl.BlockSpec((SSM_QG, SSM_TAP, 4 * LANES), lambda q, i: (q, 0, 0)),
                  pl.BlockSpec((1, LANES), lambda q, i: (0, q))],
        out_specs=tile,
        scratch_shapes=[pltpu.VMEM((SSM_W, SSM_W), BF16), pltpu.VMEM((SSM_W, 4 * SSM_HALF), BF16),
                        pltpu.VMEM((bsz, t, LANES), F32), pltpu.VMEM((bsz, t, LANES), F32)],
        compiler_params=_cparams("arbitrary", "arbitrary"),
        name="s5_chunk_out",
    )(x, mods, s_in, bd, wcc, d_row)


def _s5_mixer(x_lat, x_ctx, mods_lat, mods_ctx, lam_re, lam_im, log_dt, b_re, b_im, c_re, c_im, d_skip):
    bsz, n, _ = x_lat.shape
    assert 2 * bsz == SSM_ROWS, "state rows are (direction, batch) on the 8 sublanes"
    wbc, wcc, bd, a = _s5_prep(lam_re, lam_im, log_dt, b_re, b_im, c_re, c_im)
    t_lat = _tile(n, 1024)
    s_lat = _s5_states(x_lat, mods_lat, wbc, t=t_lat)
    s_ctx = _s5_states(x_ctx, mods_ctx, wbc, t=x_ctx.shape[1])
    a4 = a.reshape(SSM_NQ, SSM_QG, 2, 2, LANES)[..., :SSM_STATE]
    a_rows = jnp.repeat(a4.transpose(2, 0, 3, 1, 4).reshape(2, -1), bsz, axis=0)
    a_rows = a_rows.reshape(SSM_ROWS, SSM_NQ * SSM_VR, LANES).transpose(1, 0, 2)
    s_in = _s5_scan(s_ctx, s_lat, a_rows)
    return _s5_out(x_lat, mods_lat, s_in, bd, wcc, d_skip.reshape(1, -1), t=t_lat)


def _gmlp_kernel(x_ref, mod_ref, win_ref, bin_ref, lg_ref, lb_ref, ws_ref, bs_ref, wout_ref, g_ref, b_ref, o_ref,
                 v_ref, gated_ref):
    m = mod_ref[0]
    x = x_ref[0]
    t = x.shape[0]
    hw = GMLP_HEAD_DIM
    h = (x * (1.0 + m[1:2]) + m[0:1]).astype(BF16)

    def z_cols(c0):
        return jax.nn.gelu(_dot(h, win_ref[:, c0:c0 + hw]) + bin_ref[:, c0:c0 + hw])

    total = jnp.zeros((t, 1), F32)
    for hd in range(GMLP_HEADS):
        zc = z_cols(GMLP_HALF + hd * hw)
        v_ref[:, hd * hw:(hd + 1) * hw] = zc
        total = total + jnp.sum(zc, axis=-1, keepdims=True)
    mu = total / GMLP_HALF
    sq = jnp.zeros((t, 1), F32)
    for hd in range(GMLP_HEADS):
        vc = v_ref[:, hd * hw:(hd + 1) * hw] - mu
        sq = sq + jnp.sum(vc * vc, axis=-1, keepdims=True)
    rstd = lax.rsqrt(sq / GMLP_HALF + LN_EPS)
    bs = bs_ref[...]
    for hd in range(GMLP_HEADS):
        cols = slice(hd * hw, (hd + 1) * hw)
        vn = ((v_ref[:, cols] - mu) * rstd * lg_ref[:, cols] + lb_ref[:, cols]).astype(BF16)
        u = z_cols(hd * hw)
        for c in range(t // GMLP_CHUNK):
            rows = slice(c * GMLP_CHUNK, (c + 1) * GMLP_CHUNK)
            gate = _dot(ws_ref[hd], vn[rows]) + bs[:, hd:hd + 1]
            gated_ref[rows, cols] = (u[rows] * gate).astype(BF16)
    y = _dot(gated_ref[...], wout_ref[...])
    o_ref[0] = _post_norm(x, y, m[2:3], g_ref[...], b_ref[...])


def _gmlp(x, mods, w_in, b_in, lg, lb, w_s, b_s_t, w_out, ln_g, ln_b, *, t):
    bsz, n, d = x.shape
    tile = pl.BlockSpec((1, t, d), lambda b, i: (b, i, 0))
    const2 = lambda a: _const_spec(a.shape)
    return pl.pallas_call(
        _gmlp_kernel,
        out_shape=jax.ShapeDtypeStruct((bsz, n, d), F32),
        grid=(bsz, n // t),
        in_specs=[tile, pl.BlockSpec((1, N_MODS, d), lambda b, i: (b, 0, 0)),
                  const2(w_in), const2(b_in), const2(lg), const2(lb), const2(w_s), const2(b_s_t), const2(w_out),
                  const2(ln_g), const2(ln_b)],
        out_specs=tile,
        scratch_shapes=[pltpu.VMEM((t, GMLP_HALF), F32), pltpu.VMEM((t, GMLP_HALF), BF16)],
        compiler_params=_cparams("arbitrary", "arbitrary"),
        name="gmlp_mixer",
    )(x, mods, w_in, b_in, lg, lb, w_s, b_s_t, w_out, ln_g, ln_b)


def _tile(n, pref):
    return pref if n % pref == 0 else n


def _layer(layer, x_lat, x_ctx, mods, p):
    bsz, _, d = x_lat.shape
    lctx = x_ctx.shape[1]
    row = lambda v: v.reshape(1, -1)
    t_lat, t_ctx = 512, _tile(lctx, 256)
    kind = MIXERS[layer % len(MIXERS)]
    j = layer // len(MIXERS)
    ctx_out = any(MIXERS[m % len(MIXERS)] in CTX_READING_MIXERS for m in range(layer + 1, DEPTH))
    m_lat = mods[layer, :bsz]
    m_ctx = jnp.broadcast_to(mods[layer, bsz:bsz + 1], (bsz, N_MODS, d))
    g1, b1 = row(p["ln1_g"][layer]), row(p["ln1_b"][layer])

    def ffn(xs, ms, t):
        return _ffn(xs, ms, p["ffn_w_up"][layer].astype(BF16), p["ffn_conv_w"][layer], row(p["ffn_conv_b"][layer]),
                    p["ffn_w_down"][layer].astype(BF16), row(p["ln2_g"][layer]), row(p["ln2_b"][layer]),
                    t=t, cc=FFN_CHUNK)

    if kind == "pool":
        args = (p["pool_w"][j].astype(BF16), row(p["pool_b"][j]), row(p["pool_scale"][j]), g1, b1)
        x_lat = _pool(x_lat, m_lat, *args, t=t_lat)
        if ctx_out:
            x_ctx = _pool(x_ctx, m_ctx, *args, t=t_ctx)
    elif kind == "attn":
        wqkv = p["attn_w_qkv"][j].astype(BF16)
        wo = p["attn_w_o"][j].astype(BF16)
        sink = p["attn_sink"][j].astype(F32)
        q, kd, vd = _qkv(x_lat, m_lat, wqkv, t=t_lat, rope=True)
        qc, kdc, vdc = _qkv(x_ctx, m_ctx, wqkv, t=t_ctx, rope=False)
        o_lat = _attn(sink, q, kd, vd, kdc, vdc, local=True)
        x_lat = _proj_norm(o_lat, x_lat, m_lat, wo, g1, b1, t=t_lat)
        if ctx_out:
            o_ctx = _attn(sink, qc, None, None, kdc, vdc, local=False)
            x_ctx = _proj_norm(o_ctx, x_ctx, m_ctx, wo, g1, b1, t=t_ctx)
    elif kind == "ssm":
        assert not ctx_out
        y = _s5_mixer(x_lat, x_ctx, m_lat, m_ctx, p["ssm_lambda_re"][j], p["ssm_lambda_im"][j], p["ssm_log_dt"][j],
                       p["ssm_b_re"][j], p["ssm_b_im"][j], p["ssm_c_re"][j], p["ssm_c_im"][j], p["ssm_d"][j])
        x_lat = _glu_norm(y, x_lat, m_lat, p["ssm_w_glu_a"][j].astype(BF16), p["ssm_w_glu_b"][j].astype(BF16),
                          g1, b1, t=t_lat)
    else:
        assert not ctx_out
        x_lat = _gmlp(x_lat, m_lat, p["gmlp_w_in"][j].astype(BF16), row(p["gmlp_b_in"][j]), row(p["gmlp_ln_g"][j]),
                      row(p["gmlp_ln_b"][j]), p["gmlp_w_s"][j].astype(BF16), p["gmlp_b_s"][j].T,
                      p["gmlp_w_out"][j].astype(BF16), g1, b1, t=t_lat)
    x_lat = ffn(x_lat, m_lat, _tile(x_lat.shape[1], 1024))
    if ctx_out:
        x_ctx = ffn(x_ctx, m_ctx, t_ctx)
    return x_lat, x_ctx


def _mods(c, c_ctx, ada_w, ada_b):
    bsz, d = c.shape
    cond = jnp.concatenate([c, c_ctx[None, :], jnp.zeros((8 - bsz - 1, d), F32)], axis=0)
    return _ada(cond, ada_w, ada_b).reshape(DEPTH, 8, N_MODS, d)


def kernel(x, c, ctx, c_ctx, ada_w, ada_b, ln1_g, ln1_b, ln2_g, ln2_b, ffn_w_up, ffn_conv_w, ffn_conv_b, ffn_w_down, pool_w, pool_b, pool_scale, attn_w_qkv, attn_w_o, attn_sink, ssm_lambda_re, ssm_lambda_im, ssm_log_dt, ssm_b_re, ssm_b_im, ssm_c_re, ssm_c_im, ssm_d, ssm_w_glu_a, ssm_w_glu_b, gmlp_w_in, gmlp_b_in, gmlp_ln_g, gmlp_ln_b, gmlp_w_s, gmlp_b_s, gmlp_w_out):
    bsz, n, d = x.shape
    assert d == D_MODEL and bsz < 8 and n % 512 == 0 and ctx.shape[1] % ATTN_BLOCK == 0
    p = dict(ln1_g=ln1_g, ln1_b=ln1_b, ln2_g=ln2_g, ln2_b=ln2_b, ffn_w_up=ffn_w_up, ffn_conv_w=ffn_conv_w,
             ffn_conv_b=ffn_conv_b, ffn_w_down=ffn_w_down, pool_w=pool_w, pool_b=pool_b, pool_scale=pool_scale,
             attn_w_qkv=attn_w_qkv, attn_w_o=attn_w_o, attn_sink=attn_sink, ssm_lambda_re=ssm_lambda_re,
             ssm_lambda_im=ssm_lambda_im, ssm_log_dt=ssm_log_dt, ssm_b_re=ssm_b_re, ssm_b_im=ssm_b_im,
             ssm_c_re=ssm_c_re, ssm_c_im=ssm_c_im, ssm_d=ssm_d, ssm_w_glu_a=ssm_w_glu_a, ssm_w_glu_b=ssm_w_glu_b,
             gmlp_w_in=gmlp_w_in, gmlp_b_in=gmlp_b_in, gmlp_ln_g=gmlp_ln_g, gmlp_ln_b=gmlp_ln_b, gmlp_w_s=gmlp_w_s,
             gmlp_b_s=gmlp_b_s, gmlp_w_out=gmlp_w_out)
    mods = _mods(c, c_ctx, ada_w, ada_b)
    x_lat, x_ctx = x, ctx
    for layer in range(DEPTH):
        x_lat, x_ctx = _layer(layer, x_lat, x_ctx, mods, p)
    return x_lat
```

```python
import functools
import math

import jax
import jax.numpy as jnp
from jax import lax
from jax.experimental import pallas as pl
from jax.experimental.pallas import tpu as pltpu

F32 = jnp.float32
BF16 = jnp.bfloat16

D_MODEL = 1024
DEPTH = 4
MIXERS = ("pool", "attn", "ssm", "gmlp")
CTX_READING_MIXERS = ("attn", "ssm")
GRID_W = 64
N_MODS = 6
DEEPNORM_ALPHA = (2.0 * DEPTH) ** 0.25
LN_EPS = 1e-5

POOL_WINDOWS = (2, 4, 8, 16)
POOL_GROUP = D_MODEL // len(POOL_WINDOWS)

HEAD_DIM = 64
N_Q_HEADS = D_MODEL // HEAD_DIM
N_KV_HEADS = N_Q_HEADS // 4
Q_WIDTH = N_Q_HEADS * HEAD_DIM
KV_WIDTH = N_KV_HEADS * HEAD_DIM
WINDOW = 128
ATTN_BLOCK = 128
ROPE_BASE = 10000.0
NEG_INF = -1e30
LOG2E = math.log2(math.e)

SSM_GROUP = 16
SSM_N_GROUPS = D_MODEL // SSM_GROUP
SSM_STATE = 64
SSM_CHUNK = 16

GMLP_CHUNK = 128
GMLP_HALF = 2 * D_MODEL
GMLP_HEADS = 8
GMLP_HEAD_DIM = GMLP_HALF // GMLP_HEADS

FFN_HIDDEN = 2816
FFN_CHUNK = 256
FFN_OUT_ROWS = 256

LANES = 128
HALO = 16
VMEM_LIMIT = 56 * 1024 * 1024


def _cparams(*sem):
    return pltpu.CompilerParams(dimension_semantics=sem, vmem_limit_bytes=VMEM_LIMIT)


def _const_spec(shape):
    nd = len(shape)
    return pl.BlockSpec(shape, lambda *_: (0,) * nd, pipeline_mode=pl.Buffered(1))


def _post_norm(x, y, gate, g, b):
    z = DEEPNORM_ALPHA * x + gate * y
    mu = jnp.mean(z, axis=-1, keepdims=True)
    zc = z - mu
    var = jnp.mean(zc * zc, axis=-1, keepdims=True)
    return zc * lax.rsqrt(var + LN_EPS) * g + b


def _dot(a, b):
    return jnp.dot(a, b, preferred_element_type=F32)


def _dot_nt(a, b):
    return lax.dot_general(a, b, (((1,), (1,)), ((), ())), preferred_element_type=F32)


def _ada_kernel(c_ref, w_ref, b_ref, o_ref):
    c = c_ref[...]
    s = (c * jax.nn.sigmoid(c)).astype(BF16)
    o_ref[0] = _dot(s, w_ref[0].astype(BF16)) + b_ref[0]


def _ada(cond, ada_w, ada_b):
    depth, d, n = ada_w.shape
    rows = cond.shape[0]
    tn = 1536
    return pl.pallas_call(
        _ada_kernel,
        out_shape=jax.ShapeDtypeStruct((depth, rows, n), F32),
        grid=(depth, n // tn),
        in_specs=[pl.BlockSpec((rows, d), lambda l, j: (0, 0)),
                  pl.BlockSpec((1, d, tn), lambda l, j: (l, 0, j)),
                  pl.BlockSpec((1, 1, tn), lambda l, j: (l, 0, j))],
        out_specs=pl.BlockSpec((1, rows, tn), lambda l, j: (l, 0, j)),
        compiler_params=_cparams("arbitrary", "arbitrary"),
        name="ada",
    )(cond, ada_w, ada_b.reshape(depth, 1, n))


def _halo_specs(t, n):
    per = t // HALO
    last = n // HALO - 1
    prev = pl.BlockSpec((1, HALO, D_MODEL), lambda b, i, *_: (b, jnp.maximum(i * per - 1, 0), 0))
    nxt = pl.BlockSpec((1, HALO, D_MODEL), lambda b, i, *_: (b, jnp.minimum((i + 1) * per, last), 0))
    return prev, nxt


def _ffn_kernel(xp_ref, x_ref, xn_ref, mod_ref, wu_ref, cw_ref, cb_ref, wd_ref, g_ref, b_ref, o_ref, h_ref, act_ref,
                *, t, nt, f, cc):
    i = pl.program_id(1)
    rows = t + 2 * HALO
    m = mod_ref[0]
    sh, sc = m[3:4], 1.0 + m[4:5]
    keep_p = jnp.where(i > 0, 1.0, 0.0)
    keep_n = jnp.where(i < nt - 1, 1.0, 0.0)
    x = x_ref[0]
    h_ref[0:HALO, :] = ((xp_ref[0] * sc + sh) * keep_p).astype(BF16)
    h_ref[HALO:HALO + t, :] = (x * sc + sh).astype(BF16)
    h_ref[HALO + t:rows, :] = ((xn_ref[0] * sc + sh) * keep_n).astype(BF16)
    h = h_ref[...]

    def conv(off):
        u = _dot(h, wu_ref[:, off:off + cc])
        cw = cw_ref[:, off:off + cc]
        a = cb_ref[:, off:off + cc] + pltpu.roll(u, 1, 0) * cw[0:1]
        a = a + u * cw[1:2]
        a = a + pltpu.roll(u, rows - 1, 0) * cw[2:3]
        return a[HALO:HALO + t]

    for c in range(f // cc):
        val = conv(c * cc)
        gate = conv(f + c * cc)
        act_ref[:, c * cc:(c + 1) * cc] = (val * (gate * jax.nn.sigmoid(gate))).astype(BF16)
    for r in range(0, t, FFN_OUT_ROWS):
        rs = slice(r, min(r + FFN_OUT_ROWS, t))
        y = _dot(act_ref[rs, :], wd_ref[...])
        o_ref[0, rs, :] = _post_norm(x_ref[0, rs, :], y, m[5:6], g_ref[...], b_ref[...])


def _layer_spec(shape, layer):
    nd = len(shape) - 1
    return pl.BlockSpec((None,) + tuple(shape[1:]), lambda *_: (layer,) + (0,) * nd, pipeline_mode=pl.Buffered(1))


def _ffn(x, mods, w_up, conv_w, conv_b, w_down, ln_g, ln_b, *, layer, t, cc):
    bsz, n, d = x.shape
    f = w_down.shape[1]
    nt = n // t
    prev, nxt = _halo_specs(t, n)
    kern = functools.partial(_ffn_kernel, t=t, nt=nt, f=f, cc=cc)
    return pl.pallas_call(
        kern,
        out_shape=jax.ShapeDtypeStruct((bsz, n, d), F32),
        grid=(bsz, nt),
        in_specs=[prev,
                  pl.BlockSpec((1, t, d), lambda b, i: (b, i, 0)),
                  nxt,
                  pl.BlockSpec((1, N_MODS, d), lambda b, i: (b, 0, 0)),
                  _layer_spec(w_up.shape, layer), _const_spec(conv_w.shape), _const_spec(conv_b.shape),
                  _layer_spec(w_down.shape, layer), _const_spec(ln_g.shape), _const_spec(ln_b.shape)],
        out_specs=pl.BlockSpec((1, t, d), lambda b, i: (b, i, 0)),
        scratch_shapes=[pltpu.VMEM((t + 2 * HALO, d), BF16), pltpu.VMEM((t, f), BF16)],
        compiler_params=_cparams("arbitrary", "arbitrary"),
        name="conv_ffn",
    )(x, x, x, mods, w_up, conv_w, conv_b, w_down, ln_g, ln_b)


def _pool_kernel(xp_ref, x_ref, xn_ref, mod_ref, w_ref, pb_ref, ps_ref, g_ref, b_ref, o_ref, *, t, nt, n):
    i = pl.program_id(1)
    rows = t + 2 * HALO
    m = mod_ref[0]
    sh, sc = m[0:1], 1.0 + m[1:2]
    keep_p = jnp.where(i > 0, 1.0, 0.0)
    keep_n = jnp.where(i < nt - 1, 1.0, 0.0)
    x = x_ref[0]
    h = jnp.concatenate([(xp_ref[0] * sc + sh) * keep_p, x * sc + sh, (xn_ref[0] * sc + sh) * keep_n], axis=0)
    pos = i * t + lax.broadcasted_iota(jnp.int32, (t, 1), 0)
    outs = []
    for gi, win in enumerate(POOL_WINDOWS):
        hg = h[:, gi * POOL_GROUP:(gi + 1) * POOL_GROUP]
        s = hg + pltpu.roll(hg, 1, 0)
        half = 1
        while 2 * half < win:
            s = pltpu.roll(s, half, 0) + pltpu.roll(s, rows - half, 0)
            half *= 2
        lo = jnp.maximum(pos - win // 2, 0)
        hi = jnp.minimum(pos - win // 2 + win, n)
        mean = s[HALO:HALO + t] / (hi - lo).astype(F32)
        mixed = (mean - hg[HALO:HALO + t]).astype(BF16)
        outs.append(_dot(mixed, w_ref[gi]))
    y = (jnp.concatenate(outs, axis=1) + pb_ref[...]) * ps_ref[...]
    o_ref[0] = _post_norm(x, y, m[2:3], g_ref[...], b_ref[...])


def _pool(x, mods, w, pb, ps, ln_g, ln_b, *, t):
    bsz, n, d = x.shape
    nt = n // t
    prev, nxt = _halo_specs(t, n)
    kern = functools.partial(_pool_kernel, t=t, nt=nt, n=n)
    vec = pl.BlockSpec((1, d), lambda b, i: (0, 0))
    return pl.pallas_call(
        kern,
        out_shape=jax.ShapeDtypeStruct((bsz, n, d), F32),
        grid=(bsz, nt),
        in_specs=[prev, pl.BlockSpec((1, t, d), lambda b, i: (b, i, 0)), nxt,
                  pl.BlockSpec((1, N_MODS, d), lambda b, i: (b, 0, 0)),
                  pl.BlockSpec(w.shape, lambda b, i: (0, 0, 0)),
                  vec, vec, vec, vec],
        out_specs=pl.BlockSpec((1, t, d), lambda b, i: (b, i, 0)),
        compiler_params=_cparams("arbitrary", "arbitrary"),
        name="pool_mixer",
    )(x, x, x, mods, w, pb, ps, ln_g, ln_b)


def _rope_tables(n):
    tpos = jnp.arange(n)
    half = HEAD_DIM // 4
    freqs = ROPE_BASE ** (-jnp.arange(half, dtype=F32) / half)
    ang_r = (tpos // GRID_W).astype(F32)[:, None] * freqs[None, :]
    ang_c = (tpos % GRID_W).astype(F32)[:, None] * freqs[None, :]
    zero = jnp.zeros_like(ang_r)
    cr, sr, cc, sc = jnp.cos(ang_r), jnp.sin(ang_r), jnp.cos(ang_c), jnp.sin(ang_c)
    cos = jnp.tile(jnp.concatenate([cr, cr, cc, cc], axis=1), (1, 2))
    sin_first = jnp.tile(jnp.concatenate([-sr, zero, -sc, zero], axis=1), (1, 2))
    sin_second = jnp.tile(jnp.concatenate([zero, sr, zero, sc], axis=1), (1, 2))
    return cos, sin_first, sin_second


def _dup_heads(chunk, lo):
    sw = pltpu.roll(chunk, HEAD_DIM, 1)
    return jnp.where(lo, chunk, sw), jnp.where(lo, sw, chunk)


def _qkv_kernel(*refs, rope):
    if rope:
        x_ref, mod_ref, w_ref, cos_ref, sa_ref, sb_ref, q_ref, kd_ref, vd_ref = refs
    else:
        x_ref, mod_ref, w_ref, q_ref, kd_ref, vd_ref = refs
    m = mod_ref[0]
    h = (x_ref[0] * (1.0 + m[1:2]) + m[0:1]).astype(BF16)
    qkv = _dot(h, w_ref[...])
    t = qkv.shape[0]
    lo = lax.broadcasted_iota(jnp.int32, (t, LANES), 1) < HEAD_DIM
    scale = HEAD_DIM ** -0.5 * LOG2E

    def rot(v):
        if not rope:
            return v
        quarter = HEAD_DIM // 4
        return (v * cos_ref[...] + pltpu.roll(v, LANES - quarter, 1) * sa_ref[...]
                + pltpu.roll(v, quarter, 1) * sb_ref[...])

    for c in range(Q_WIDTH // LANES):
        q_ref[0, :, c * LANES:(c + 1) * LANES] = (rot(qkv[:, c * LANES:(c + 1) * LANES]) * scale).astype(BF16)
    for c in range(KV_WIDTH // LANES):
        k0, k1 = _dup_heads(rot(qkv[:, Q_WIDTH + c * LANES:Q_WIDTH + (c + 1) * LANES]), lo)
        kd_ref[0, :, (2 * c) * LANES:(2 * c + 1) * LANES] = k0.astype(BF16)
        kd_ref[0, :, (2 * c + 1) * LANES:(2 * c + 2) * LANES] = k1.astype(BF16)
        v0, v1 = _dup_heads(qkv[:, Q_WIDTH + KV_WIDTH + c * LANES:Q_WIDTH + KV_WIDTH + (c + 1) * LANES], lo)
        vd_ref[0, :, (2 * c) * LANES:(2 * c + 1) * LANES] = v0.astype(BF16)
        vd_ref[0, :, (2 * c + 1) * LANES:(2 * c + 2) * LANES] = v1.astype(BF16)


def _qkv(x, mods, w_qkv, *, t, rope):
    bsz, n, d = x.shape
    kdw = N_KV_HEADS * LANES
    ins = [x, mods, w_qkv]
    specs = [pl.BlockSpec((1, t, d), lambda b, i: (b, i, 0)),
             pl.BlockSpec((1, N_MODS, d), lambda b, i: (b, 0, 0)),
             pl.BlockSpec(w_qkv.shape, lambda b, i: (0, 0))]
    if rope:
        ins += list(_rope_tables(n))
        specs += [pl.BlockSpec((t, LANES), lambda b, i: (i, 0))] * 3
    return pl.pallas_call(
        functools.partial(_qkv_kernel, rope=rope),
        out_shape=(jax.ShapeDtypeStruct((bsz, n, Q_WIDTH), BF16),
                   jax.ShapeDtypeStruct((bsz, n, kdw), BF16),
                   jax.ShapeDtypeStruct((bsz, n, kdw), BF16)),
        grid=(bsz, n // t),
        in_specs=specs,
        out_specs=(pl.BlockSpec((1, t, Q_WIDTH), lambda b, i: (b, i, 0)),
                   pl.BlockSpec((1, t, kdw), lambda b, i: (b, i, 0)),
                   pl.BlockSpec((1, t, kdw), lambda b, i: (b, i, 0))),
        compiler_params=_cparams("arbitrary", "arbitrary"),
        name="qkv_rope" if rope else "qkv_ctx",
    )(*ins)


def _attn_kernel(sink_ref, q_ref, *refs, nb, local):
    if local:
        kp_ref, kc_ref, kn_ref, kx_ref, vp_ref, vc_ref, vn_ref, vx_ref, o_ref = refs
        k_refs, v_refs = (kp_ref, kc_ref, kn_ref, kx_ref), (vp_ref, vc_ref, vn_ref, vx_ref)
    else:
        kx_ref, vx_ref, o_ref = refs
        k_refs, v_refs = (kx_ref,), (vx_ref,)
    blk = ATTN_BLOCK
    group = N_Q_HEADS // N_KV_HEADS
    rows = group * blk
    nblk = pl.program_id(1)
    if local:
        qi = lax.broadcasted_iota(jnp.int32, (rows, blk), 0) % blk
        kj = lax.broadcasted_iota(jnp.int32, (rows, blk), 1)
        valid_prev = (kj >= qi) & (nblk > 0)
        valid_next = (kj <= qi) & (nblk < nb - 1)
    lo = lax.broadcasted_iota(jnp.int32, (blk, LANES), 1) < HEAD_DIM
    head_of_row = lax.broadcasted_iota(jnp.int32, (rows, 1), 0) // blk
    for hk in range(N_KV_HEADS):
        kx = jnp.concatenate([r[0, :, hk * LANES:(hk + 1) * LANES] for r in k_refs], axis=0)
        vx = jnp.concatenate([r[0, :, hk * LANES:(hk + 1) * LANES] for r in v_refs], axis=0)
        parts = []
        for c in range(group // 2):
            qc = q_ref[0, :, (hk * group // 2 + c) * LANES:(hk * group // 2 + c + 1) * LANES]
            zero = jnp.zeros_like(qc)
            parts += [jnp.where(lo, qc, zero), jnp.where(lo, zero, qc)]
        q4 = jnp.concatenate(parts, axis=0)
        s = _dot_nt(q4, kx)
        if local:
            s = jnp.concatenate([jnp.where(valid_prev, s[:, :blk], NEG_INF), s[:, blk:2 * blk],
                                 jnp.where(valid_next, s[:, 2 * blk:3 * blk], NEG_INF), s[:, 3 * blk:]], axis=1)
        sink = jnp.zeros((rows, 1), F32)
        for g in range(group):
            sink = jnp.where(head_of_row == g, sink_ref[hk * group + g] * LOG2E, sink)
        mx = jnp.maximum(jnp.max(s, axis=-1, keepdims=True), sink)
        p = jnp.exp2(s - mx)
        den = jnp.sum(p, axis=-1, keepdims=True) + jnp.exp2(sink - mx)
        o4 = _dot(p.astype(BF16), vx) / den
        for c in range(group // 2):
            oc = jnp.where(lo, o4[(2 * c) * blk:(2 * c + 1) * blk], o4[(2 * c + 1) * blk:(2 * c + 2) * blk])
            o_ref[0, :, (hk * group // 2 + c) * LANES:(hk * group // 2 + c + 1) * LANES] = oc.astype(BF16)


def _attn(sink, q, kd, vd, kd_ctx, vd_ctx, *, local):
    bsz, n, _ = q.shape
    blk = ATTN_BLOCK
    nb = n // blk
    lctx = kd_ctx.shape[1]
    kdw = kd_ctx.shape[2]
    smem = pl.BlockSpec(memory_space=pltpu.SMEM)
    qspec = pl.BlockSpec((1, blk, Q_WIDTH), lambda b, i: (b, i, 0))
    ctx_spec = pl.BlockSpec((1, lctx, kdw), lambda b, i: (b, 0, 0))
    if local:
        band = [pl.BlockSpec((1, blk, kdw), lambda b, i: (b, jnp.maximum(i - 1, 0), 0)),
                pl.BlockSpec((1, blk, kdw), lambda b, i: (b, i, 0)),
                pl.BlockSpec((1, blk, kdw), lambda b, i: (b, jnp.minimum(i + 1, nb - 1), 0))]
        specs = [smem, qspec] + band + [ctx_spec] + band + [ctx_spec]
        args = (sink, q, kd, kd, kd, kd_ctx, vd, vd, vd, vd_ctx)
    else:
        specs = [smem, qspec, ctx_spec, ctx_spec]
        args = (sink, q, kd_ctx, vd_ctx)
    return pl.pallas_call(
        functools.partial(_attn_kernel, nb=nb, local=local),
        out_shape=jax.ShapeDtypeStruct((bsz, n, Q_WIDTH), BF16),
        grid=(bsz, nb),
        in_specs=specs,
        out_specs=pl.BlockSpec((1, blk, Q_WIDTH), lambda b, i: (b, i, 0)),
        compiler_params=_cparams("arbitrary", "arbitrary"),
        name="banded_attn" if local else "ctx_attn",
    )(*args)


def _proj_norm_kernel(a_ref, x_ref, mod_ref, w_ref, g_ref, b_ref, o_ref):
    y = _dot(a_ref[0], w_ref[...])
    o_ref[0] = _post_norm(x_ref[0], y, mod_ref[0][2:3], g_ref[...], b_ref[...])


def _proj_norm(a, x, mods, w, ln_g, ln_b, *, t):
    bsz, n, d = x.shape
    ka = a.shape[2]
    vec = pl.BlockSpec((1, d), lambda b, i: (0, 0))
    return pl.pallas_call(
        _proj_norm_kernel,
        out_shape=jax.ShapeDtypeStruct((bsz, n, d), F32),
        grid=(bsz, n // t),
        in_specs=[pl.BlockSpec((1, t, ka), lambda b, i: (b, i, 0)),
                  pl.BlockSpec((1, t, d), lambda b, i: (b, i, 0)),
                  pl.BlockSpec((1, N_MODS, d), lambda b, i: (b, 0, 0)),
                  pl.BlockSpec(w.shape, lambda b, i: (0, 0)),
                  vec, vec],
        out_specs=pl.BlockSpec((1, t, d), lambda b, i: (b, i, 0)),
        compiler_params=_cparams("arbitrary", "arbitrary"),
        name="attn_out_norm",
    )(a, x, mods, w, ln_g, ln_b)


def _glu_norm_kernel(y_ref, x_ref, mod_ref, wa_ref, wb_ref, g_ref, b_ref, o_ref):
    gl = jax.nn.gelu(y_ref[0]).astype(BF16)
    out = _dot(gl, wa_ref[...]) * jax.nn.sigmoid(_dot(gl, wb_ref[...]))
    o_ref[0] = _post_norm(x_ref[0], out, mod_ref[0][2:3], g_ref[...], b_ref[...])


def _glu_norm(y, x, mods, wa, wb, ln_g, ln_b, *, t):
    bsz, n, d = x.shape
    tile = pl.BlockSpec((1, t, d), lambda b, i: (b, i, 0))
    vec = pl.BlockSpec((1, d), lambda b, i: (0, 0))
    wspec = pl.BlockSpec((d, d), lambda b, i: (0, 0))
    return pl.pallas_call(
        _glu_norm_kernel,
        out_shape=jax.ShapeDtypeStruct((bsz, n, d), F32),
        grid=(bsz, n // t),
        in_specs=[tile, tile, pl.BlockSpec((1, N_MODS, d), lambda b, i: (b, 0, 0)), wspec, wspec, vec, vec],
        out_specs=tile,
        compiler_params=_cparams("arbitrary", "arbitrary"),
        name="ssm_glu_norm",
    )(y, x, mods, wa, wb, ln_g, ln_b)


SSM_QG = LANES // SSM_GROUP
SSM_NQ = D_MODEL // LANES
SSM_ROWS = 8
SSM_W = SSM_CHUNK * LANES
SSM_HALF = SSM_QG * SSM_STATE
SSM_TAP = SSM_CHUNK * SSM_GROUP
SSM_VR = 2 * SSM_HALF // LANES


def _s5_prep_kernel(lr_ref, li_ref, ldt_ref, btr_ref, bti_ref, cr_ref, ci_ref, wb_ref, wc_ref, bd_ref, a_ref):
    c = SSM_CHUNK
    q = pl.program_id(0)
    lag = lax.broadcasted_iota(jnp.int32, (c + 1, LANES), 0).astype(F32)
    lane = lax.broadcasted_iota(jnp.int32, (1, LANES), 1)
    taps = [[[] for _ in range(c)] for _ in range(2)]
    for g8 in range(SSM_QG):
        own = jnp.where((lane < SSM_STATE) == (g8 % 2 == 0), 1.0, 0.0)
        wb_cols, wc_cols, a_cols = [], [], []
        for d in range(2):
            lr, li = lr_ref[g8, d], li_ref[g8, d]
            dt = jnp.exp(jnp.full((1, LANES), ldt_ref[q * SSM_QG + g8, d], F32))
            mag = jnp.exp(lag * (lr * dt))
            ang = lag * (li * dt)
            pw_r, pw_i = mag * jnp.cos(ang), mag * jnp.sin(ang)
            lbr, lbi = pw_r[1:2], pw_i[1:2]
            den = lr * lr + li * li
            qr = ((lbr - 1.0) * lr + lbi * li) / den
            qi = (lbi * lr - (lbr - 1.0) * li) / den
            btr, bti = btr_ref[g8, d], bti_ref[g8, d]
            bbr = qr * btr - qi * bti
            bbi = qr * bti + qi * btr
            cr, ci = cr_ref[g8, d], ci_ref[g8, d]

            def cl(j):
                return cr * pw_r[j:j + 1] - ci * pw_i[j:j + 1], -(cr * pw_i[j:j + 1] + ci * pw_r[j:j + 1])

            def bl(j):
                return bbr * pw_r[j:j + 1] - bbi * pw_i[j:j + 1], bbr * pw_i[j:j + 1] + bbi * pw_r[j:j + 1]

            e = jnp.concatenate([jnp.concatenate(cl(j), axis=1) for j in range(c)], axis=0)
            pieces = [jnp.concatenate([bbr * own, bbi * own], axis=1)]
            if g8 > 0:
                pieces.insert(0, jnp.zeros((g8 * SSM_GROUP, 2 * LANES), F32))
            if g8 < SSM_QG - 1:
                pieces.append(jnp.zeros(((SSM_QG - 1 - g8) * SSM_GROUP, 2 * LANES), F32))
            kt = lax.dot_general(e, jnp.concatenate(pieces, axis=0), (((1,), (1,)), ((), ())),
                                 preferred_element_type=F32, precision=lax.Precision.HIGHEST)
            for j in range(c):
                taps[d][j].append(kt[j * SSM_GROUP:(j + 1) * SSM_GROUP])
            wbl = [bl(c - 1 - k) if d == 0 else bl(k) for k in range(c)]
            wb_cols += [jnp.concatenate([w[0] for w in wbl], axis=0) * own,
                        jnp.concatenate([w[1] for w in wbl], axis=0) * own]
            wcl = [cl(k + 1) if d == 0 else cl(c - k) for k in range(c)]
            wc_cols += [jnp.concatenate([w[0] for w in wcl], axis=0) * own,
                        jnp.concatenate([w[1] for w in wcl], axis=0) * own]
            a_cols += [pw_r[c:c + 1], pw_i[c:c + 1]]
        wb_ref[g8] = jnp.concatenate(wb_cols, axis=1).astype(BF16)
        wc_ref[g8] = jnp.concatenate(wc_cols, axis=1).astype(BF16)
        a_ref[g8] = jnp.concatenate(a_cols, axis=1)
    for d in range(2):
        for j in range(c):
            blk = jnp.concatenate(taps[d][j], axis=0)
            if d == 0 and j == 0:
                blk = blk + jnp.concatenate(taps[1][0], axis=0)
            bd_ref[0, d, j] = blk.astype(BF16)


def _s5_prep(lam_re, lam_im, log_dt, b_re, b_im, c_re, c_im):
    g, nq, qg = SSM_N_GROUPS, SSM_NQ, SSM_QG
    per_g = lambda a: jnp.swapaxes(a, 0, 1)
    dup = lambda a: jnp.concatenate([a, a], axis=-1)
    lam_spec = pl.BlockSpec((qg, 2, 1, LANES), lambda i: (i, 0, 0, 0))
    mat_spec = pl.BlockSpec((qg, 2, SSM_GROUP, LANES), lambda i: (i, 0, 0, 0))
    w_spec = pl.BlockSpec((qg, SSM_TAP, 4 * LANES), lambda i: (i, 0, 0))
    return pl.pallas_call(
        _s5_prep_kernel,
        out_shape=(jax.ShapeDtypeStruct((g, SSM_TAP, 4 * LANES), BF16),
                   jax.ShapeDtypeStruct((g, SSM_TAP, 4 * LANES), BF16),
                   jax.ShapeDtypeStruct((nq, 2, SSM_CHUNK, LANES, LANES), BF16),
                   jax.ShapeDtypeStruct((g, 1, 4 * LANES), F32)),
        grid=(nq,),
        in_specs=[lam_spec, lam_spec, pl.BlockSpec(memory_space=pltpu.SMEM), mat_spec, mat_spec, mat_spec, mat_spec],
        out_specs=(w_spec, w_spec, pl.BlockSpec((1, 2, SSM_CHUNK, LANES, LANES), lambda i: (i, 0, 0, 0, 0)),
                   pl.BlockSpec((qg, 1, 4 * LANES), lambda i: (i, 0, 0))),
        compiler_params=_cparams("arbitrary"),
        name="s5_prep",
    )(dup(per_g(lam_re))[:, :, None, :], dup(per_g(lam_im))[:, :, None, :], per_g(log_dt),
      dup(jnp.swapaxes(per_g(b_re), 2, 3)), dup(jnp.swapaxes(per_g(b_im), 2, 3)), dup(per_g(c_re)), dup(per_g(c_im)))


def _s5_expand(dst_ref, src_ref):
    dst_ref[...] = jnp.zeros_like(dst_ref)
    for g8 in range(SSM_QG):
        for k in range(SSM_CHUNK):
            for c4 in range(4):
                r0 = k * LANES + g8 * SSM_GROUP
                c0 = c4 * SSM_HALF + (g8 // 2) * LANES
                dst_ref[r0:r0 + SSM_GROUP, c0:c0 + LANES] = src_ref[g8, k * SSM_GROUP:(k + 1) * SSM_GROUP,
                                                                    c4 * LANES:(c4 + 1) * LANES]


def _s5_modulate(x_ref, mod_ref, u_ref, bsz):
    for b in range(bsz):
        m = mod_ref[b]
        u_ref[b] = x_ref[b] * (1.0 + m[1:2]) + m[0:1]


def _s5_gather(u_ref, bsz, nchunk):
    rows = [jnp.concatenate([u_ref[b, pl.ds(pos, nchunk, stride=SSM_CHUNK), :] for pos in range(SSM_CHUNK)],
                            axis=1).astype(BF16) for b in range(bsz)]
    return jnp.concatenate(rows, axis=0)


def _s5_states_kernel(x_ref, mod_ref, wbc_ref, o_ref, wb_ref, u_ref, *, bsz, nchunk):
    @pl.when(pl.program_id(1) == 0)
    def _():
        _s5_expand(wb_ref, wbc_ref)

    _s5_modulate(x_ref, mod_ref, u_ref, bsz)
    s = _dot(_s5_gather(u_ref, bsz, nchunk), wb_ref[...])
    for b in range(bsz):
        for v in range(SSM_VR):
            for d in range(2):
                o_ref[v, pl.ds(d * bsz + b, nchunk, stride=SSM_ROWS), :] = (
                    s[b * nchunk:(b + 1) * nchunk, (d * SSM_VR + v) * LANES:(d * SSM_VR + v + 1) * LANES])


def _s5_states(x, mods, wbc, *, t):
    bsz, n, _ = x.shape
    nchunk = t // SSM_CHUNK
    return pl.pallas_call(
        functools.partial(_s5_states_kernel, bsz=bsz, nchunk=nchunk),
        out_shape=jax.ShapeDtypeStruct((SSM_NQ * SSM_VR, n // SSM_CHUNK * SSM_ROWS, LANES), F32),
        grid=(SSM_NQ, n // t),
        in_specs=[pl.BlockSpec((bsz, t, LANES), lambda q, i: (0, i, q)),
                  pl.BlockSpec((bsz, N_MODS, LANES), lambda q, i: (0, 0, q)),
                  pl.BlockSpec((SSM_QG, SSM_TAP, 4 * LANES), lambda q, i: (q, 0, 0))],
        out_specs=pl.BlockSpec((SSM_VR, nchunk * SSM_ROWS, LANES), lambda q, i: (q, i, 0)),
        scratch_shapes=[pltpu.VMEM((SSM_W, 4 * SSM_HALF), BF16), pltpu.VMEM((bsz, t, LANES), F32)],
        compiler_params=_cparams("arbitrary", "arbitrary"),
        name="s5_chunk_states",
    )(x, mods, wbc)


def _s5_scan_kernel(xc_ref, xl_ref, a_ref, o_ref):
    h = SSM_VR // 2
    ar, ai = a_ref[:h], a_ref[h:]
    nc, nl = xc_ref.shape[1] // SSM_ROWS, xl_ref.shape[1] // SSM_ROWS
    fwd_rows = lax.broadcasted_iota(jnp.int32, (SSM_VR, SSM_ROWS, LANES), 1) < SSM_ROWS // 2

    def step(ref, i, s):
        x = ref[:, pl.ds(pl.multiple_of(i * SSM_ROWS, SSM_ROWS), SSM_ROWS), :]
        return ar * s[0] - ai * s[1] + x[:h], ar * s[1] + ai * s[0] + x[h:]

    def fwd(i, s):
        o_ref[:, pl.ds(pl.multiple_of(i * SSM_ROWS, SSM_ROWS), SSM_ROWS), :] = jnp.concatenate(s, axis=0)
        return step(xl_ref, i, s)

    def bwd(k, s):
        rows = pl.ds(pl.multiple_of((nl - 1 - k) * SSM_ROWS, SSM_ROWS), SSM_ROWS)
        o_ref[:, rows, :] = jnp.where(fwd_rows, o_ref[:, rows, :], jnp.concatenate(s, axis=0))
        return step(xl_ref, nl - 1 - k, s)

    zero = (jnp.zeros((h, SSM_ROWS, LANES), F32), jnp.zeros((h, SSM_ROWS, LANES), F32))
    s = lax.fori_loop(0, nc, lambda i, s: step(xc_ref, i, s), zero)
    lax.fori_loop(0, nl, fwd, s)
    s = lax.fori_loop(0, nc, lambda k, s: step(xc_ref, nc - 1 - k, s), zero)
    lax.fori_loop(0, nl, bwd, s)


def _s5_scan(s_ctx, s_lat, a_rows):
    blk = lambda rows: pl.BlockSpec((SSM_VR, rows, LANES), lambda q: (q, 0, 0))
    return pl.pallas_call(
        _s5_scan_kernel,
        out_shape=jax.ShapeDtypeStruct(s_lat.shape, F32),
        grid=(SSM_NQ,),
        in_specs=[blk(s_ctx.shape[1]), blk(s_lat.shape[1]), blk(SSM_ROWS)],
        out_specs=blk(s_lat.shape[1]),
        compiler_params=_cparams("arbitrary"),
        name="s5_chunk_scan",
    )(s_ctx, s_lat, a_rows)


def _s5_out_kernel(x_ref, mod_ref, s_ref, bd_ref, wcc_ref, d_ref, o_ref, toep_ref, wc_ref, u_ref, y_ref, *,
                   bsz, nchunk):
    c = SSM_CHUNK

    @pl.when(pl.program_id(1) == 0)
    def _():
        _s5_expand(wc_ref, wcc_ref)
        for t in range(c):
            for k in range(c):
                toep_ref[t * LANES:(t + 1) * LANES, k * LANES:(k + 1) * LANES] = (
                    bd_ref[0, 0, t - k] if t >= k else bd_ref[0, 1, k - t])

    _s5_modulate(x_ref, mod_ref, u_ref, bsz)
    lhs_s = jnp.concatenate(
        [jnp.concatenate([s_ref[v, pl.ds(d * bsz + b, nchunk, stride=SSM_ROWS), :]
                          for d in range(2) for v in range(SSM_VR)], axis=1).astype(BF16)
         for b in range(bsz)], axis=0)
    y = _dot_nt(_s5_gather(u_ref, bsz, nchunk), toep_ref[...]) + _dot_nt(lhs_s, wc_ref[...])
    for b in range(bsz):
        for t in range(c):
            y_ref[b, pl.ds(t, nchunk, stride=c), :] = y[b * nchunk:(b + 1) * nchunk, t * LANES:(t + 1) * LANES]
    for b in range(bsz):
        o_ref[b] = y_ref[b] + d_ref[...] * u_ref[b]


def _s5_out(x, mods, s_in, bd, wcc, d_row, *, t):
    bsz, n, d = x.shape
    nchunk = t // SSM_CHUNK
    tile = pl.BlockSpec((bsz, t, LANES), lambda q, i: (0, i, q))
    return pl.pallas_call(
        functools.partial(_s5_out_kernel, bsz=bsz, nchunk=nchunk),
        out_shape=jax.ShapeDtypeStruct((bsz, n, d), F32),
        grid=(SSM_NQ, n // t),
        in_specs=[tile,
                  pl.BlockSpec((bsz, N_MODS, LANES), lambda q, i: (0, 0, q)),
                  pl.BlockSpec((SSM_VR, nchunk * SSM_ROWS, LANES), lambda q, i: (q, i, 0)),
                  pl.BlockSpec((1, 2, SSM_CHUNK, LANES, LANES), lambda q, i: (q, 0, 0, 0, 0)),
                  pl.BlockSpec((SSM_QG, SSM_TAP, 4 * LANES), lambda q, i: (q, 0, 0)),
                  pl.BlockSpec((1, LANES), lambda q, i: (0, q))],
        out_specs=tile,
        scratch_shapes=[pltpu.VMEM((SSM_W, SSM_W), BF16), pltpu.VMEM((SSM_W, 4 * SSM_HALF), BF16),
                        pltpu.VMEM((bsz, t, LANES), F32), pltpu.VMEM((bsz, t, LANES), F32)],
        compiler_params=_cparams("arbitrary", "arbitrary"),
        name="s5_chunk_out",
    )(x, mods, s_in, bd, wcc, d_row)


def _s5_mixer(x_lat, x_ctx, mods_lat, mods_ctx, lam_re, lam_im, log_dt, b_re, b_im, c_re, c_im, d_skip):
    bsz, n, _ = x_lat.shape
    assert 2 * bsz == SSM_ROWS, "state rows are (direction, batch) on the 8 sublanes"
    wbc, wcc, bd, a = _s5_prep(lam_re, lam_im, log_dt, b_re, b_im, c_re, c_im)
    t_lat = _tile(n, 1024)
    s_lat = _s5_states(x_lat, mods_lat, wbc, t=t_lat)
    s_ctx = _s5_states(x_ctx, mods_ctx, wbc, t=x_ctx.shape[1])
    a4 = a.reshape(SSM_NQ, SSM_QG, 2, 2, LANES)[..., :SSM_STATE]
    a_rows = jnp.repeat(a4.transpose(2, 0, 3, 1, 4).reshape(2, -1), bsz, axis=0)
    a_rows = a_rows.reshape(SSM_ROWS, SSM_NQ * SSM_VR, LANES).transpose(1, 0, 2)
    s_in = _s5_scan(s_ctx, s_lat, a_rows)
    return _s5_out(x_lat, mods_lat, s_in, bd, wcc, d_skip.reshape(1, -1), t=t_lat)


def _gmlp_kernel(x_ref, mod_ref, win_ref, bin_ref, lg_ref, lb_ref, ws_ref, bs_ref, wout_ref, g_ref, b_ref, o_ref,
                 v_ref, gated_ref):
    m = mod_ref[0]
    x = x_ref[0]
    t = x.shape[0]
    hw = GMLP_HEAD_DIM
    h = (x * (1.0 + m[1:2]) + m[0:1]).astype(BF16)

    def z_cols(c0):
        return jax.nn.gelu(_dot(h, win_ref[:, c0:c0 + hw]) + bin_ref[:, c0:c0 + hw])

    total = jnp.zeros((t, 1), F32)
    for hd in range(GMLP_HEADS):
        zc = z_cols(GMLP_HALF + hd * hw)
        v_ref[:, hd * hw:(hd + 1) * hw] = zc
        total = total + jnp.sum(zc, axis=-1, keepdims=True)
    mu = total / GMLP_HALF
    sq = jnp.zeros((t, 1), F32)
    for hd in range(GMLP_HEADS):
        vc = v_ref[:, hd * hw:(hd + 1) * hw] - mu
        sq = sq + jnp.sum(vc * vc, axis=-1, keepdims=True)
    rstd = lax.rsqrt(sq / GMLP_HALF + LN_EPS)
    bs = bs_ref[...]
    for hd in range(GMLP_HEADS):
        cols = slice(hd * hw, (hd + 1) * hw)
        vn = ((v_ref[:, cols] - mu) * rstd * lg_ref[:, cols] + lb_ref[:, cols]).astype(BF16)
        u = z_cols(hd * hw)
        for c in range(t // GMLP_CHUNK):
            rows = slice(c * GMLP_CHUNK, (c + 1) * GMLP_CHUNK)
            gate = _dot(ws_ref[hd], vn[rows]) + bs[:, hd:hd + 1]
            gated_ref[rows, cols] = (u[rows] * gate).astype(BF16)
    y = _dot(gated_ref[...], wout_ref[...])
    o_ref[0] = _post_norm(x, y, m[2:3], g_ref[...], b_ref[...])


def _gmlp(x, mods, w_in, b_in, lg, lb, w_s, b_s_t, w_out, ln_g, ln_b, *, t):
    bsz, n, d = x.shape
    tile = pl.BlockSpec((1, t, d), lambda b, i: (b, i, 0))
    const2 = lambda a: _const_spec(a.shape)
    return pl.pallas_call(
        _gmlp_kernel,
        out_shape=jax.ShapeDtypeStruct((bsz, n, d), F32),
        grid=(bsz, n // t),
        in_specs=[tile, pl.BlockSpec((1, N_MODS, d), lambda b, i: (b, 0, 0)),
                  const2(w_in), const2(b_in), const2(lg), const2(lb), const2(w_s), const2(b_s_t), const2(w_out),
                  const2(ln_g), const2(ln_b)],
        out_specs=tile,
        scratch_shapes=[pltpu.VMEM((t, GMLP_HALF), F32), pltpu.VMEM((t, GMLP_HALF), BF16)],
        compiler_params=_cparams("arbitrary", "arbitrary"),
        name="gmlp_mixer",
    )(x, mods, w_in, b_in, lg, lb, w_s, b_s_t, w_out, ln_g, ln_b)


def _tile(n, pref):
    return pref if n % pref == 0 else n


def _layer(layer, x_lat, x_ctx, mods, p):
    bsz, _, d = x_lat.shape
    lctx = x_ctx.shape[1]
    row = lambda v: v.reshape(1, -1)
    t_lat, t_ctx = 1024, _tile(lctx, 256)
    kind = MIXERS[layer % len(MIXERS)]
    j = layer // len(MIXERS)
    ctx_out = any(MIXERS[m % len(MIXERS)] in CTX_READING_MIXERS for m in range(layer + 1, DEPTH))
    m_lat = mods[layer, :bsz]
    m_ctx = jnp.broadcast_to(mods[layer, bsz:bsz + 1], (bsz, N_MODS, d))
    g1, b1 = row(p["ln1_g"][layer]), row(p["ln1_b"][layer])

    def ffn(xs, ms, t):
        return _ffn(xs, ms, p["ffn_w_up_bf16"], p["ffn_conv_w"][layer], row(p["ffn_conv_b"][layer]),
                    p["ffn_w_down_bf16"], row(p["ln2_g"][layer]), row(p["ln2_b"][layer]),
                    layer=layer, t=t, cc=FFN_CHUNK)

    if kind == "pool":
        args = (p["pool_w"][j].astype(BF16), row(p["pool_b"][j]), row(p["pool_scale"][j]), g1, b1)
        x_lat = _pool(x_lat, m_lat, *args, t=t_lat)
        if ctx_out:
            x_ctx = _pool(x_ctx, m_ctx, *args, t=t_ctx)
    elif kind == "attn":
        wqkv = p["attn_w_qkv"][j].astype(BF16)
        wo = p["attn_w_o"][j].astype(BF16)
        sink = p["attn_sink"][j].astype(F32)
        q, kd, vd = _qkv(x_lat, m_lat, wqkv, t=t_lat, rope=True)
        qc, kdc, vdc = _qkv(x_ctx, m_ctx, wqkv, t=t_ctx, rope=False)
        o_lat = _attn(sink, q, kd, vd, kdc, vdc, local=True)
        x_lat = _proj_norm(o_lat, x_lat, m_lat, wo, g1, b1, t=t_lat)
        if ctx_out:
            o_ctx = _attn(sink, qc, None, None, kdc, vdc, local=False)
            x_ctx = _proj_norm(o_ctx, x_ctx, m_ctx, wo, g1, b1, t=t_ctx)
    elif kind == "ssm":
        assert not ctx_out
        y = _s5_mixer(x_lat, x_ctx, m_lat, m_ctx, p["ssm_lambda_re"][j], p["ssm_lambda_im"][j], p["ssm_log_dt"][j],
                       p["ssm_b_re"][j], p["ssm_b_im"][j], p["ssm_c_re"][j], p["ssm_c_im"][j], p["ssm_d"][j])
        x_lat = _glu_norm(y, x_lat, m_lat, p["ssm_w_glu_a"][j].astype(BF16), p["ssm_w_glu_b"][j].astype(BF16),
                          g1, b1, t=t_lat)
    else:
        assert not ctx_out
        x_lat = _gmlp(x_lat, m_lat, p["gmlp_w_in"][j].astype(BF16), row(p["gmlp_b_in"][j]), row(p["gmlp_ln_g"][j]),
                      row(p["gmlp_ln_b"][j]), p["gmlp_w_s"][j].astype(BF16), p["gmlp_b_s"][j].T,
                      p["gmlp_w_out"][j].astype(BF16), g1, b1, t=t_lat)
    x_lat = ffn(x_lat, m_lat, _tile(x_lat.shape[1], 1024))
    if ctx_out:
        x_ctx = ffn(x_ctx, m_ctx, t_ctx)
    return x_lat, x_ctx


def _mods(c, c_ctx, ada_w, ada_b):
    bsz, d = c.shape
    cond = jnp.concatenate([c, c_ctx[None, :], jnp.zeros((8 - bsz - 1, d), F32)], axis=0)
    return _ada(cond, ada_w, ada_b).reshape(DEPTH, 8, N_MODS, d)


def kernel(x, c, ctx, c_ctx, ada_w, ada_b, ln1_g, ln1_b, ln2_g, ln2_b, ffn_w_up, ffn_conv_w, ffn_conv_b, ffn_w_down, pool_w, pool_b, pool_scale, attn_w_qkv, attn_w_o, attn_sink, ssm_lambda_re, ssm_lambda_im, ssm_log_dt, ssm_b_re, ssm_b_im, ssm_c_re, ssm_c_im, ssm_d, ssm_w_glu_a, ssm_w_glu_b, gmlp_w_in, gmlp_b_in, gmlp_ln_g, gmlp_ln_b, gmlp_w_s, gmlp_b_s, gmlp_w_out):
    bsz, n, d = x.shape
    assert d == D_MODEL and bsz < 8 and n % 512 == 0 and ctx.shape[1] % ATTN_BLOCK == 0
    p = dict(ln1_g=ln1_g, ln1_b=ln1_b, ln2_g=ln2_g, ln2_b=ln2_b, ffn_w_up=ffn_w_up, ffn_conv_w=ffn_conv_w,
             ffn_conv_b=ffn_conv_b, ffn_w_down=ffn_w_down, pool_w=pool_w, pool_b=pool_b, pool_scale=pool_scale,
             attn_w_qkv=attn_w_qkv, attn_w_o=attn_w_o, attn_sink=attn_sink, ssm_lambda_re=ssm_lambda_re,
             ssm_lambda_im=ssm_lambda_im, ssm_log_dt=ssm_log_dt, ssm_b_re=ssm_b_re, ssm_b_im=ssm_b_im,
             ssm_c_re=ssm_c_re, ssm_c_im=ssm_c_im, ssm_d=ssm_d, ssm_w_glu_a=ssm_w_glu_a, ssm_w_glu_b=ssm_w_glu_b,
             gmlp_w_in=gmlp_w_in, gmlp_b_in=gmlp_b_in, gmlp_ln_g=gmlp_ln_g, gmlp_ln_b=gmlp_ln_b, gmlp_w_s=gmlp_w_s,
             gmlp_b_s=gmlp_b_s, gmlp_w_out=gmlp_w_out)
    p["ffn_w_up_bf16"] = ffn_w_up.astype(BF16)
    p["ffn_w_down_bf16"] = ffn_w_down.astype(BF16)
    mods = _mods(c, c_ctx, ada_w, ada_b)
    x_lat, x_ctx = x, ctx
    for layer in range(DEPTH):
        x_lat, x_ctx = _layer(layer, x_lat, x_ctx, mods, p)
    return x_lat
```

```python
import functools
import math

import jax
import jax.numpy as jnp
from jax import lax
from jax.experimental import pallas as pl
from jax.experimental.pallas import tpu as pltpu

F32 = jnp.float32
BF16 = jnp.bfloat16

D_MODEL = 1024
DEPTH = 4
MIXERS = ("pool", "attn", "ssm", "gmlp")
CTX_READING_MIXERS = ("attn", "ssm")
GRID_W = 64
N_MODS = 6
DEEPNORM_ALPHA = (2.0 * DEPTH) ** 0.25
LN_EPS = 1e-5

POOL_WINDOWS = (2, 4, 8, 16)
POOL_GROUP = D_MODEL // len(POOL_WINDOWS)

HEAD_DIM = 64
N_Q_HEADS = D_MODEL // HEAD_DIM
N_KV_HEADS = N_Q_HEADS // 4
Q_WIDTH = N_Q_HEADS * HEAD_DIM
KV_WIDTH = N_KV_HEADS * HEAD_DIM
WINDOW = 128
ATTN_BLOCK = 128
ROPE_BASE = 10000.0
NEG_INF = -1e30
LOG2E = math.log2(math.e)

SSM_GROUP = 16
SSM_N_GROUPS = D_MODEL // SSM_GROUP
SSM_STATE = 64
SSM_CHUNK = 16

GMLP_CHUNK = 128
GMLP_HALF = 2 * D_MODEL
GMLP_HEADS = 8
GMLP_HEAD_DIM = GMLP_HALF // GMLP_HEADS

FFN_HIDDEN = 2816
FFN_CHUNK = 256
OUT_ROWS = 256

LANES = 128
HALO = 16
VMEM_LIMIT = 56 * 1024 * 1024


def _cparams(*sem):
    return pltpu.CompilerParams(dimension_semantics=sem, vmem_limit_bytes=VMEM_LIMIT)


def _const_spec(shape):
    nd = len(shape)
    return pl.BlockSpec(shape, lambda *_: (0,) * nd, pipeline_mode=pl.Buffered(1))


def _post_norm(x, y, gate, g, b):
    z = DEEPNORM_ALPHA * x + gate * y
    mu = jnp.mean(z, axis=-1, keepdims=True)
    zc = z - mu
    var = jnp.mean(zc * zc, axis=-1, keepdims=True)
    return zc * lax.rsqrt(var + LN_EPS) * g + b


def _row_chunks(t):
    return [slice(r, min(r + OUT_ROWS, t)) for r in range(0, t, OUT_ROWS)]


def _dot(a, b):
    return jnp.dot(a, b, preferred_element_type=F32)


def _dot_nt(a, b):
    return lax.dot_general(a, b, (((1,), (1,)), ((), ())), preferred_element_type=F32)


def _ada_kernel(c_ref, w_ref, b_ref, o_ref):
    c = c_ref[...]
    s = (c * jax.nn.sigmoid(c)).astype(BF16)
    o_ref[0] = _dot(s, w_ref[0].astype(BF16)) + b_ref[0]


def _ada(cond, ada_w, ada_b):
    depth, d, n = ada_w.shape
    rows = cond.shape[0]
    tn = 1536
    return pl.pallas_call(
        _ada_kernel,
        out_shape=jax.ShapeDtypeStruct((depth, rows, n), F32),
        grid=(depth, n // tn),
        in_specs=[pl.BlockSpec((rows, d), lambda l, j: (0, 0)),
                  pl.BlockSpec((1, d, tn), lambda l, j: (l, 0, j)),
                  pl.BlockSpec((1, 1, tn), lambda l, j: (l, 0, j))],
        out_specs=pl.BlockSpec((1, rows, tn), lambda l, j: (l, 0, j)),
        compiler_params=_cparams("arbitrary", "arbitrary"),
        name="ada",
    )(cond, ada_w, ada_b.reshape(depth, 1, n))


def _halo_specs(t, n):
    per = t // HALO
    last = n // HALO - 1
    prev = pl.BlockSpec((1, HALO, D_MODEL), lambda b, i, *_: (b, jnp.maximum(i * per - 1, 0), 0))
    nxt = pl.BlockSpec((1, HALO, D_MODEL), lambda b, i, *_: (b, jnp.minimum((i + 1) * per, last), 0))
    return prev, nxt


def _ffn_kernel(xp_ref, x_ref, xn_ref, mod_ref, wu_ref, cw_ref, cb_ref, wd_ref, g_ref, b_ref, o_ref, h_ref, act_ref,
                *, t, nt, f, cc):
    i = pl.program_id(1)
    rows = t + 2 * HALO
    m = mod_ref[0]
    sh, sc = m[3:4], 1.0 + m[4:5]
    keep_p = jnp.where(i > 0, 1.0, 0.0)
    keep_n = jnp.where(i < nt - 1, 1.0, 0.0)
    x = x_ref[0]
    h_ref[0:HALO, :] = ((xp_ref[0] * sc + sh) * keep_p).astype(BF16)
    h_ref[HALO:HALO + t, :] = (x * sc + sh).astype(BF16)
    h_ref[HALO + t:rows, :] = ((xn_ref[0] * sc + sh) * keep_n).astype(BF16)
    h = h_ref[...]

    def conv(off):
        u = _dot(h, wu_ref[:, off:off + cc])
        cw = cw_ref[:, off:off + cc]
        a = cb_ref[:, off:off + cc] + pltpu.roll(u, 1, 0) * cw[0:1]
        a = a + u * cw[1:2]
        a = a + pltpu.roll(u, rows - 1, 0) * cw[2:3]
        return a[HALO:HALO + t]

    for c in range(f // cc):
        val = conv(c * cc)
        gate = conv(f + c * cc)
        act_ref[:, c * cc:(c + 1) * cc] = (val * (gate * jax.nn.sigmoid(gate))).astype(BF16)
    for rs in _row_chunks(t):
        y = _dot(act_ref[rs, :], wd_ref[...])
        o_ref[0, rs, :] = _post_norm(x_ref[0, rs, :], y, m[5:6], g_ref[...], b_ref[...])


def _layer_spec(shape, layer):
    nd = len(shape) - 1
    return pl.BlockSpec((None,) + tuple(shape[1:]), lambda *_: (layer,) + (0,) * nd, pipeline_mode=pl.Buffered(1))


def _ffn(x, mods, w_up, conv_w, conv_b, w_down, ln_g, ln_b, *, layer, t, cc):
    bsz, n, d = x.shape
    f = w_down.shape[1]
    nt = n // t
    prev, nxt = _halo_specs(t, n)
    kern = functools.partial(_ffn_kernel, t=t, nt=nt, f=f, cc=cc)
    return pl.pallas_call(
        kern,
        out_shape=jax.ShapeDtypeStruct((bsz, n, d), F32),
        grid=(bsz, nt),
        in_specs=[prev,
                  pl.BlockSpec((1, t, d), lambda b, i: (b, i, 0)),
                  nxt,
                  pl.BlockSpec((1, N_MODS, d), lambda b, i: (b, 0, 0)),
                  _layer_spec(w_up.shape, layer), _const_spec(conv_w.shape), _const_spec(conv_b.shape),
                  _layer_spec(w_down.shape, layer), _const_spec(ln_g.shape), _const_spec(ln_b.shape)],
        out_specs=pl.BlockSpec((1, t, d), lambda b, i: (b, i, 0)),
        scratch_shapes=[pltpu.VMEM((t + 2 * HALO, d), BF16), pltpu.VMEM((t, f), BF16)],
        compiler_params=_cparams("arbitrary", "arbitrary"),
        name="conv_ffn",
    )(x, x, x, mods, w_up, conv_w, conv_b, w_down, ln_g, ln_b)


def _pool_kernel(xp_ref, x_ref, xn_ref, mod_ref, w_ref, pb_ref, ps_ref, g_ref, b_ref, o_ref, *, t, nt, n):
    i = pl.program_id(1)
    rows = t + 2 * HALO
    m = mod_ref[0]
    sh, sc = m[0:1], 1.0 + m[1:2]
    keep_p = jnp.where(i > 0, 1.0, 0.0)
    keep_n = jnp.where(i < nt - 1, 1.0, 0.0)
    x = x_ref[0]
    h = jnp.concatenate([(xp_ref[0] * sc + sh) * keep_p, x * sc + sh, (xn_ref[0] * sc + sh) * keep_n], axis=0)
    pos = i * t + lax.broadcasted_iota(jnp.int32, (t, 1), 0)
    outs = []
    for gi, win in enumerate(POOL_WINDOWS):
        hg = h[:, gi * POOL_GROUP:(gi + 1) * POOL_GROUP]
        s = hg + pltpu.roll(hg, 1, 0)
        half = 1
        while 2 * half < win:
            s = pltpu.roll(s, half, 0) + pltpu.roll(s, rows - half, 0)
            half *= 2
        lo = jnp.maximum(pos - win // 2, 0)
        hi = jnp.minimum(pos - win // 2 + win, n)
        mean = s[HALO:HALO + t] / (hi - lo).astype(F32)
        mixed = (mean - hg[HALO:HALO + t]).astype(BF16)
        outs.append(_dot(mixed, w_ref[gi]))
    y = (jnp.concatenate(outs, axis=1) + pb_ref[...]) * ps_ref[...]
    o_ref[0] = _post_norm(x, y, m[2:3], g_ref[...], b_ref[...])


def _pool(x, mods, w, pb, ps, ln_g, ln_b, *, t):
    bsz, n, d = x.shape
    nt = n // t
    prev, nxt = _halo_specs(t, n)
    kern = functools.partial(_pool_kernel, t=t, nt=nt, n=n)
    vec = pl.BlockSpec((1, d), lambda b, i: (0, 0))
    return pl.pallas_call(
        kern,
        out_shape=jax.ShapeDtypeStruct((bsz, n, d), F32),
        grid=(bsz, nt),
        in_specs=[prev, pl.BlockSpec((1, t, d), lambda b, i: (b, i, 0)), nxt,
                  pl.BlockSpec((1, N_MODS, d), lambda b, i: (b, 0, 0)),
                  pl.BlockSpec(w.shape, lambda b, i: (0, 0, 0)),
                  vec, vec, vec, vec],
        out_specs=pl.BlockSpec((1, t, d), lambda b, i: (b, i, 0)),
        compiler_params=_cparams("arbitrary", "arbitrary"),
        name="pool_mixer",
    )(x, x, x, mods, w, pb, ps, ln_g, ln_b)


def _rope_tables(n):
    tpos = jnp.arange(n)
    half = HEAD_DIM // 4
    freqs = ROPE_BASE ** (-jnp.arange(half, dtype=F32) / half)
    ang_r = (tpos // GRID_W).astype(F32)[:, None] * freqs[None, :]
    ang_c = (tpos % GRID_W).astype(F32)[:, None] * freqs[None, :]
    zero = jnp.zeros_like(ang_r)
    cr, sr, cc, sc = jnp.cos(ang_r), jnp.sin(ang_r), jnp.cos(ang_c), jnp.sin(ang_c)
    cos = jnp.tile(jnp.concatenate([cr, cr, cc, cc], axis=1), (1, 2))
    sin_first = jnp.tile(jnp.concatenate([-sr, zero, -sc, zero], axis=1), (1, 2))
    sin_second = jnp.tile(jnp.concatenate([zero, sr, zero, sc], axis=1), (1, 2))
    return cos, sin_first, sin_second


def _dup_heads(chunk, lo):
    sw = pltpu.roll(chunk, HEAD_DIM, 1)
    return jnp.where(lo, chunk, sw), jnp.where(lo, sw, chunk)


def _qkv_kernel(*refs, rope):
    if rope:
        x_ref, mod_ref, w_ref, cos_ref, sa_ref, sb_ref, q_ref, kd_ref, vt_ref = refs
    else:
        x_ref, mod_ref, w_ref, q_ref, kd_ref, vt_ref = refs
    m = mod_ref[0]
    h = (x_ref[0] * (1.0 + m[1:2]) + m[0:1]).astype(BF16)
    qkv = _dot(h, w_ref[...])
    t = qkv.shape[0]
    lo = lax.broadcasted_iota(jnp.int32, (t, LANES), 1) < HEAD_DIM
    scale = HEAD_DIM ** -0.5 * LOG2E

    def rot(v):
        if not rope:
            return v
        quarter = HEAD_DIM // 4
        return (v * cos_ref[...] + pltpu.roll(v, LANES - quarter, 1) * sa_ref[...]
                + pltpu.roll(v, quarter, 1) * sb_ref[...])

    for c in range(Q_WIDTH // LANES):
        q_ref[0, :, c * LANES:(c + 1) * LANES] = (rot(qkv[:, c * LANES:(c + 1) * LANES]) * scale).astype(BF16)
    for c in range(KV_WIDTH // LANES):
        k0, k1 = _dup_heads(rot(qkv[:, Q_WIDTH + c * LANES:Q_WIDTH + (c + 1) * LANES]), lo)
        kd_ref[0, :, (2 * c) * LANES:(2 * c + 1) * LANES] = k0.astype(BF16)
        kd_ref[0, :, (2 * c + 1) * LANES:(2 * c + 2) * LANES] = k1.astype(BF16)
        v = qkv[:, Q_WIDTH + KV_WIDTH + c * LANES:Q_WIDTH + KV_WIDTH + (c + 1) * LANES]
        vt_ref[0, c * LANES:(c + 1) * LANES, :] = jnp.transpose(v).astype(BF16)


def _qkv(x, mods, w_qkv, *, t, rope):
    bsz, n, d = x.shape
    kdw = N_KV_HEADS * LANES
    ins = [x, mods, w_qkv]
    specs = [pl.BlockSpec((1, t, d), lambda b, i: (b, i, 0)),
             pl.BlockSpec((1, N_MODS, d), lambda b, i: (b, 0, 0)),
             pl.BlockSpec(w_qkv.shape, lambda b, i: (0, 0))]
    if rope:
        ins += list(_rope_tables(n))
        specs += [pl.BlockSpec((t, LANES), lambda b, i: (i, 0))] * 3
    return pl.pallas_call(
        functools.partial(_qkv_kernel, rope=rope),
        out_shape=(jax.ShapeDtypeStruct((bsz, n, Q_WIDTH), BF16),
                   jax.ShapeDtypeStruct((bsz, n, kdw), BF16),
                   jax.ShapeDtypeStruct((bsz, KV_WIDTH, n), BF16)),
        grid=(bsz, n // t),
        in_specs=specs,
        out_specs=(pl.BlockSpec((1, t, Q_WIDTH), lambda b, i: (b, i, 0)),
                   pl.BlockSpec((1, t, kdw), lambda b, i: (b, i, 0)),
                   pl.BlockSpec((1, KV_WIDTH, t), lambda b, i: (b, 0, i))),
        compiler_params=_cparams("arbitrary", "arbitrary"),
        name="qkv_rope" if rope else "qkv_ctx",
    )(*ins)


def _attn_kernel(sink_ref, q_ref, *refs, nb, local):
    if local:
        kp_ref, kc_ref, kn_ref, kx_ref, vp_ref, vc_ref, vn_ref, vx_ref, o_ref = refs
        k_refs, v_refs = (kp_ref, kc_ref, kn_ref, kx_ref), (vp_ref, vc_ref, vn_ref, vx_ref)
    else:
        kx_ref, vx_ref, o_ref = refs
        k_refs, v_refs = (kx_ref,), (vx_ref,)
    blk = ATTN_BLOCK
    group = N_Q_HEADS // N_KV_HEADS
    cols = group * blk
    nblk = pl.program_id(1)
    if local:
        kj = lax.broadcasted_iota(jnp.int32, (blk, cols), 0)
        qi = lax.broadcasted_iota(jnp.int32, (blk, cols), 1) % blk
        valid_prev = (kj >= qi) & (nblk > 0)
        valid_next = (kj <= qi) & (nblk < nb - 1)
    lo = lax.broadcasted_iota(jnp.int32, (blk, LANES), 1) < HEAD_DIM
    head_of_col = lax.broadcasted_iota(jnp.int32, (1, cols), 1) // blk
    for hk in range(N_KV_HEADS):
        kx = jnp.concatenate([r[0, :, hk * LANES:(hk + 1) * LANES] for r in k_refs], axis=0)
        vt = jnp.concatenate([r[0, hk * HEAD_DIM:(hk + 1) * HEAD_DIM, :] for r in v_refs], axis=1)
        parts = []
        for c in range(group // 2):
            qc = q_ref[0, :, (hk * group // 2 + c) * LANES:(hk * group // 2 + c + 1) * LANES]
            zero = jnp.zeros_like(qc)
            parts += [jnp.where(lo, qc, zero), jnp.where(lo, zero, qc)]
        q4 = jnp.concatenate(parts, axis=0)
        s = _dot_nt(kx, q4)
        if local:
            s = jnp.concatenate([jnp.where(valid_prev, s[:blk], NEG_INF), s[blk:2 * blk],
                                 jnp.where(valid_next, s[2 * blk:3 * blk], NEG_INF), s[3 * blk:]], axis=0)
        sink = jnp.zeros((1, cols), F32)
        for g in range(group):
            sink = jnp.where(head_of_col == g, sink_ref[hk * group + g] * LOG2E, sink)
        mx = jnp.maximum(jnp.max(s, axis=0, keepdims=True), sink)
        p = jnp.exp2(s - mx)
        den = jnp.sum(p, axis=0, keepdims=True) + jnp.exp2(sink - mx)
        ot = _dot(vt, p.astype(BF16)) / den
        for c in range(group // 2):
            pair = jnp.concatenate([ot[:, (2 * c) * blk:(2 * c + 1) * blk],
                                    ot[:, (2 * c + 1) * blk:(2 * c + 2) * blk]], axis=0)
            o_ref[0, :, (hk * group // 2 + c) * LANES:(hk * group // 2 + c + 1) * LANES] = (
                jnp.transpose(pair).astype(BF16))


def _attn(sink, q, kd, vt, kd_ctx, vt_ctx, *, local):
    bsz, n, _ = q.shape
    blk = ATTN_BLOCK
    assert WINDOW == blk, "the band is exactly the previous, own and next key block"
    nb = n // blk
    lctx = kd_ctx.shape[1]
    kdw = kd_ctx.shape[2]
    smem = pl.BlockSpec(memory_space=pltpu.SMEM)
    qspec = pl.BlockSpec((1, blk, Q_WIDTH), lambda b, i: (b, i, 0))
    kctx_spec = pl.BlockSpec((1, lctx, kdw), lambda b, i: (b, 0, 0))
    vctx_spec = pl.BlockSpec((1, KV_WIDTH, lctx), lambda b, i: (b, 0, 0))
    if local:
        prev, nxt = (lambda i: jnp.maximum(i - 1, 0)), (lambda i: jnp.minimum(i + 1, nb - 1))
        kband = [pl.BlockSpec((1, blk, kdw), lambda b, i: (b, prev(i), 0)),
                 pl.BlockSpec((1, blk, kdw), lambda b, i: (b, i, 0)),
                 pl.BlockSpec((1, blk, kdw), lambda b, i: (b, nxt(i), 0))]
        vband = [pl.BlockSpec((1, KV_WIDTH, blk), lambda b, i: (b, 0, prev(i))),
                 pl.BlockSpec((1, KV_WIDTH, blk), lambda b, i: (b, 0, i)),
                 pl.BlockSpec((1, KV_WIDTH, blk), lambda b, i: (b, 0, nxt(i)))]
        specs = [smem, qspec] + kband + [kctx_spec] + vband + [vctx_spec]
        args = (sink, q, kd, kd, kd, kd_ctx, vt, vt, vt, vt_ctx)
    else:
        specs = [smem, qspec, kctx_spec, vctx_spec]
        args = (sink, q, kd_ctx, vt_ctx)
    return pl.pallas_call(
        functools.partial(_attn_kernel, nb=nb, local=local),
        out_shape=jax.ShapeDtypeStruct((bsz, n, Q_WIDTH), BF16),
        grid=(bsz, nb),
        in_specs=specs,
        out_specs=pl.BlockSpec((1, blk, Q_WIDTH), lambda b, i: (b, i, 0)),
        compiler_params=_cparams("arbitrary", "arbitrary"),
        name="banded_attn" if local else "ctx_attn",
    )(*args)


def _proj_norm_kernel(a_ref, x_ref, mod_ref, w_ref, g_ref, b_ref, o_ref):
    for rs in _row_chunks(x_ref.shape[1]):
        y = _dot(a_ref[0, rs, :], w_ref[...])
        o_ref[0, rs, :] = _post_norm(x_ref[0, rs, :], y, mod_ref[0][2:3], g_ref[...], b_ref[...])


def _proj_norm(a, x, mods, w, ln_g, ln_b, *, t):
    bsz, n, d = x.shape
    ka = a.shape[2]
    vec = pl.BlockSpec((1, d), lambda b, i: (0, 0))
    return pl.pallas_call(
        _proj_norm_kernel,
        out_shape=jax.ShapeDtypeStruct((bsz, n, d), F32),
        grid=(bsz, n // t),
        in_specs=[pl.BlockSpec((1, t, ka), lambda b, i: (b, i, 0)),
                  pl.BlockSpec((1, t, d), lambda b, i: (b, i, 0)),
                  pl.BlockSpec((1, N_MODS, d), lambda b, i: (b, 0, 0)),
                  pl.BlockSpec(w.shape, lambda b, i: (0, 0)),
                  vec, vec],
        out_specs=pl.BlockSpec((1, t, d), lambda b, i: (b, i, 0)),
        compiler_params=_cparams("arbitrary", "arbitrary"),
        name="attn_out_norm",
    )(a, x, mods, w, ln_g, ln_b)


def _glu_norm_kernel(y_ref, x_ref, mod_ref, wa_ref, wb_ref, g_ref, b_ref, o_ref):
    gl = jax.nn.gelu(y_ref[0]).astype(BF16)
    out = _dot(gl, wa_ref[...]) * jax.nn.sigmoid(_dot(gl, wb_ref[...]))
    o_ref[0] = _post_norm(x_ref[0], out, mod_ref[0][2:3], g_ref[...], b_ref[...])


def _glu_norm(y, x, mods, wa, wb, ln_g, ln_b, *, t):
    bsz, n, d = x.shape
    tile = pl.BlockSpec((1, t, d), lambda b, i: (b, i, 0))
    vec = pl.BlockSpec((1, d), lambda b, i: (0, 0))
    wspec = pl.BlockSpec((d, d), lambda b, i: (0, 0))
    return pl.pallas_call(
        _glu_norm_kernel,
        out_shape=jax.ShapeDtypeStruct((bsz, n, d), F32),
        grid=(bsz, n // t),
        in_specs=[tile, tile, pl.BlockSpec((1, N_MODS, d), lambda b, i: (b, 0, 0)), wspec, wspec, vec, vec],
        out_specs=tile,
        compiler_params=_cparams("arbitrary", "arbitrary"),
        name="ssm_glu_norm",
    )(y, x, mods, wa, wb, ln_g, ln_b)


SSM_QG = LANES // SSM_GROUP
SSM_NQ = D_MODEL // LANES
SSM_ROWS = 8
SSM_W = SSM_CHUNK * LANES
SSM_HALF = SSM_QG * SSM_STATE
SSM_TAP = SSM_CHUNK * SSM_GROUP
SSM_VR = 2 * SSM_HALF // LANES


def _s5_prep_kernel(lr_ref, li_ref, ldt_ref, btr_ref, bti_ref, cr_ref, ci_ref, wb_ref, wc_ref, bd_ref, a_ref):
    c = SSM_CHUNK
    q = pl.program_id(0)
    lag = lax.broadcasted_iota(jnp.int32, (c + 1, LANES), 0).astype(F32)
    lane = lax.broadcasted_iota(jnp.int32, (1, LANES), 1)
    taps = [[[] for _ in range(c)] for _ in range(2)]
    for g8 in range(SSM_QG):
        own = jnp.where((lane < SSM_STATE) == (g8 % 2 == 0), 1.0, 0.0)
        wb_cols, wc_cols, a_cols = [], [], []
        for d in range(2):
            lr, li = lr_ref[g8, d], li_ref[g8, d]
            dt = jnp.exp(jnp.full((1, LANES), ldt_ref[q * SSM_QG + g8, d], F32))
            mag = jnp.exp(lag * (lr * dt))
            ang = lag * (li * dt)
            pw_r, pw_i = mag * jnp.cos(ang), mag * jnp.sin(ang)
            lbr, lbi = pw_r[1:2], pw_i[1:2]
            den = lr * lr + li * li
            qr = ((lbr - 1.0) * lr + lbi * li) / den
            qi = (lbi * lr - (lbr - 1.0) * li) / den
            btr, bti = btr_ref[g8, d], bti_ref[g8, d]
            bbr = qr * btr - qi * bti
            bbi = qr * bti + qi * btr
            cr, ci = cr_ref[g8, d], ci_ref[g8, d]

            def cl(j):
                return cr * pw_r[j:j + 1] - ci * pw_i[j:j + 1], -(cr * pw_i[j:j + 1] + ci * pw_r[j:j + 1])

            def bl(j):
                return bbr * pw_r[j:j + 1] - bbi * pw_i[j:j + 1], bbr * pw_i[j:j + 1] + bbi * pw_r[j:j + 1]

            e = jnp.concatenate([jnp.concatenate(cl(j), axis=1) for j in range(c)], axis=0)
            pieces = [jnp.concatenate([bbr * own, bbi * own], axis=1)]
            if g8 > 0:
                pieces.insert(0, jnp.zeros((g8 * SSM_GROUP, 2 * LANES), F32))
            if g8 < SSM_QG - 1:
                pieces.append(jnp.zeros(((SSM_QG - 1 - g8) * SSM_GROUP, 2 * LANES), F32))
            kt = lax.dot_general(e, jnp.concatenate(pieces, axis=0), (((1,), (1,)), ((), ())),
                                 preferred_element_type=F32, precision=lax.Precision.HIGHEST)
            for j in range(c):
                taps[d][j].append(kt[j * SSM_GROUP:(j + 1) * SSM_GROUP])
            wbl = [bl(c - 1 - k) if d == 0 else bl(k) for k in range(c)]
            wb_cols += [jnp.concatenate([w[0] for w in wbl], axis=0) * own,
                        jnp.concatenate([w[1] for w in wbl], axis=0) * own]
            wcl = [cl(k + 1) if d == 0 else cl(c - k) for k in range(c)]
            wc_cols += [jnp.concatenate([w[0] for w in wcl], axis=0) * own,
                        jnp.concatenate([w[1] for w in wcl], axis=0) * own]
            a_cols += [pw_r[c:c + 1], pw_i[c:c + 1]]
        wb_ref[g8] = jnp.concatenate(wb_cols, axis=1).astype(BF16)
        wc_ref[g8] = jnp.concatenate(wc_cols, axis=1).astype(BF16)
        a_ref[g8] = jnp.concatenate(a_cols, axis=1)
    for d in range(2):
        for j in range(c):
            blk = jnp.concatenate(taps[d][j], axis=0)
            if d == 0 and j == 0:
                blk = blk + jnp.concatenate(taps[1][0], axis=0)
            bd_ref[0, d, j] = blk.astype(BF16)


def _s5_prep(lam_re, lam_im, log_dt, b_re, b_im, c_re, c_im):
    g, nq, qg = SSM_N_GROUPS, SSM_NQ, SSM_QG
    per_g = lambda a: jnp.swapaxes(a, 0, 1)
    dup = lambda a: jnp.concatenate([a, a], axis=-1)
    lam_spec = pl.BlockSpec((qg, 2, 1, LANES), lambda i: (i, 0, 0, 0))
    mat_spec = pl.BlockSpec((qg, 2, SSM_GROUP, LANES), lambda i: (i, 0, 0, 0))
    w_spec = pl.BlockSpec((qg, SSM_TAP, 4 * LANES), lambda i: (i, 0, 0))
    return pl.pallas_call(
        _s5_prep_kernel,
        out_shape=(jax.ShapeDtypeStruct((g, SSM_TAP, 4 * LANES), BF16),
                   jax.ShapeDtypeStruct((g, SSM_TAP, 4 * LANES), BF16),
                   jax.ShapeDtypeStruct((nq, 2, SSM_CHUNK, LANES, LANES), BF16),
                   jax.ShapeDtypeStruct((g, 1, 4 * LANES), F32)),
        grid=(nq,),
        in_specs=[lam_spec, lam_spec, pl.BlockSpec(memory_space=pltpu.SMEM), mat_spec, mat_spec, mat_spec, mat_spec],
        out_specs=(w_spec, w_spec, pl.BlockSpec((1, 2, SSM_CHUNK, LANES, LANES), lambda i: (i, 0, 0, 0, 0)),
                   pl.BlockSpec((qg, 1, 4 * LANES), lambda i: (i, 0, 0))),
        compiler_params=_cparams("arbitrary"),
        name="s5_prep",
    )(dup(per_g(lam_re))[:, :, None, :], dup(per_g(lam_im))[:, :, None, :], per_g(log_dt),
      dup(jnp.swapaxes(per_g(b_re), 2, 3)), dup(jnp.swapaxes(per_g(b_im), 2, 3)), dup(per_g(c_re)), dup(per_g(c_im)))


def _s5_expand(dst_ref, src_ref):
    dst_ref[...] = jnp.zeros_like(dst_ref)
    for g8 in range(SSM_QG):
        for k in range(SSM_CHUNK):
            for c4 in range(4):
                r0 = k * LANES + g8 * SSM_GROUP
                c0 = c4 * SSM_HALF + (g8 // 2) * LANES
                dst_ref[r0:r0 + SSM_GROUP, c0:c0 + LANES] = src_ref[g8, k * SSM_GROUP:(k + 1) * SSM_GROUP,
                                                                    c4 * LANES:(c4 + 1) * LANES]


def _s5_modulate(x_ref, mod_ref, u_ref, bsz):
    for b in range(bsz):
        m = mod_ref[b]
        u_ref[b] = x_ref[b] * (1.0 + m[1:2]) + m[0:1]


def _s5_gather(u_ref, bsz, nchunk):
    rows = [jnp.concatenate([u_ref[b, pl.ds(pos, nchunk, stride=SSM_CHUNK), :] for pos in range(SSM_CHUNK)],
                            axis=1).astype(BF16) for b in range(bsz)]
    return jnp.concatenate(rows, axis=0)


def _s5_states_kernel(x_ref, mod_ref, wbc_ref, o_ref, wb_ref, u_ref, *, bsz, nchunk):
    @pl.when(pl.program_id(1) == 0)
    def _():
        _s5_expand(wb_ref, wbc_ref)

    _s5_modulate(x_ref, mod_ref, u_ref, bsz)
    s = _dot(_s5_gather(u_ref, bsz, nchunk), wb_ref[...])
    for b in range(bsz):
        for v in range(SSM_VR):
            for d in range(2):
                o_ref[v, pl.ds(d * bsz + b, nchunk, stride=SSM_ROWS), :] = (
                    s[b * nchunk:(b + 1) * nchunk, (d * SSM_VR + v) * LANES:(d * SSM_VR + v + 1) * LANES])


def _s5_states(x, mods, wbc, *, t):
    bsz, n, _ = x.shape
    nchunk = t // SSM_CHUNK
    return pl.pallas_call(
        functools.partial(_s5_states_kernel, bsz=bsz, nchunk=nchunk),
        out_shape=jax.ShapeDtypeStruct((SSM_NQ * SSM_VR, n // SSM_CHUNK * SSM_ROWS, LANES), F32),
        grid=(SSM_NQ, n // t),
        in_specs=[pl.BlockSpec((bsz, t, LANES), lambda q, i: (0, i, q)),
                  pl.BlockSpec((bsz, N_MODS, LANES), lambda q, i: (0, 0, q)),
                  pl.BlockSpec((SSM_QG, SSM_TAP, 4 * LANES), lambda q, i: (q, 0, 0))],
        out_specs=pl.BlockSpec((SSM_VR, nchunk * SSM_ROWS, LANES), lambda q, i: (q, i, 0)),
        scratch_shapes=[pltpu.VMEM((SSM_W, 4 * SSM_HALF), BF16), pltpu.VMEM((bsz, t, LANES), F32)],
        compiler_params=_cparams("arbitrary", "arbitrary"),
        name="s5_chunk_states",
    )(x, mods, wbc)


def _s5_scan_kernel(xc_ref, xl_ref, a_ref, o_ref):
    h = SSM_VR // 2
    ar, ai = a_ref[:h], a_ref[h:]
    nc, nl = xc_ref.shape[1] // SSM_ROWS, xl_ref.shape[1] // SSM_ROWS
    fwd_rows = lax.broadcasted_iota(jnp.int32, (SSM_VR, SSM_ROWS, LANES), 1) < SSM_ROWS // 2

    assert nl % 2 == 0
    chunk = lambda i: pl.ds(pl.multiple_of(i * SSM_ROWS, SSM_ROWS), SSM_ROWS)

    def step(ref, n, k, s):
        x = jnp.where(fwd_rows, ref[:, chunk(k), :], ref[:, chunk(n - 1 - k), :])
        return ar * s[0] - ai * s[1] + x[:h], ar * s[1] + ai * s[0] + x[h:]

    def first_touch(k, s):
        full = jnp.concatenate(s, axis=0)
        o_ref[:, chunk(k), :] = full
        o_ref[:, chunk(nl - 1 - k), :] = full
        return step(xl_ref, nl, k, s)

    def second_touch(k, s):
        full = jnp.concatenate(s, axis=0)
        o_ref[:, chunk(k), :] = jnp.where(fwd_rows, full, o_ref[:, chunk(k), :])
        o_ref[:, chunk(nl - 1 - k), :] = jnp.where(fwd_rows, o_ref[:, chunk(nl - 1 - k), :], full)
        return step(xl_ref, nl, k, s)

    zero = (jnp.zeros((h, SSM_ROWS, LANES), F32), jnp.zeros((h, SSM_ROWS, LANES), F32))
    s = lax.fori_loop(0, nc, lambda k, s: step(xc_ref, nc, k, s), zero)
    s = lax.fori_loop(0, nl // 2, first_touch, s)
    lax.fori_loop(nl // 2, nl, second_touch, s)


def _s5_scan(s_ctx, s_lat, a_rows):
    blk = lambda rows: pl.BlockSpec((SSM_VR, rows, LANES), lambda q: (q, 0, 0))
    return pl.pallas_call(
        _s5_scan_kernel,
        out_shape=jax.ShapeDtypeStruct(s_lat.shape, F32),
        grid=(SSM_NQ,),
        in_specs=[blk(s_ctx.shape[1]), blk(s_lat.shape[1]), blk(SSM_ROWS)],
        out_specs=blk(s_lat.shape[1]),
        compiler_params=_cparams("arbitrary"),
        name="s5_chunk_scan",
    )(s_ctx, s_lat, a_rows)


def _s5_out_kernel(x_ref, mod_ref, s_ref, bd_ref, wcc_ref, d_ref, o_ref, toep_ref, wc_ref, u_ref, y_ref, *,
                   bsz, nchunk):
    c = SSM_CHUNK

    @pl.when(pl.program_id(1) == 0)
    def _():
        _s5_expand(wc_ref, wcc_ref)
        for t in range(c):
            for k in range(c):
                toep_ref[t * LANES:(t + 1) * LANES, k * LANES:(k + 1) * LANES] = (
                    bd_ref[0, 0, t - k] if t >= k else bd_ref[0, 1, k - t])

    _s5_modulate(x_ref, mod_ref, u_ref, bsz)
    lhs_s = jnp.concatenate(
        [jnp.concatenate([s_ref[v, pl.ds(d * bsz + b, nchunk, stride=SSM_ROWS), :]
                          for d in range(2) for v in range(SSM_VR)], axis=1).astype(BF16)
         for b in range(bsz)], axis=0)
    y = _dot_nt(_s5_gather(u_ref, bsz, nchunk), toep_ref[...]) + _dot_nt(lhs_s, wc_ref[...])
    for b in range(bsz):
        for t in range(c):
            y_ref[b, pl.ds(t, nchunk, stride=c), :] = y[b * nchunk:(b + 1) * nchunk, t * LANES:(t + 1) * LANES]
    for b in range(bsz):
        o_ref[b] = y_ref[b] + d_ref[...] * u_ref[b]


def _s5_out(x, mods, s_in, bd, wcc, d_row, *, t):
    bsz, n, d = x.shape
    nchunk = t // SSM_CHUNK
    tile = pl.BlockSpec((bsz, t, LANES), lambda q, i: (0, i, q))
    return pl.pallas_call(
        functools.partial(_s5_out_kernel, bsz=bsz, nchunk=nchunk),
        out_shape=jax.ShapeDtypeStruct((bsz, n, d), F32),
        grid=(SSM_NQ, n // t),
        in_specs=[tile,
                  pl.BlockSpec((bsz, N_MODS, LANES), lambda q, i: (0, 0, q)),
                  pl.BlockSpec((SSM_VR, nchunk * SSM_ROWS, LANES), lambda q, i: (q, i, 0)),
                  pl.BlockSpec((1, 2, SSM_CHUNK, LANES, LANES), lambda q, i: (q, 0, 0, 0, 0)),
                  pl.BlockSpec((SSM_QG, SSM_TAP, 4 * LANES), lambda q, i: (q, 0, 0)),
                  pl.BlockSpec((1, LANES), lambda q, i: (0, q))],
        out_specs=tile,
        scratch_shapes=[pltpu.VMEM((SSM_W, SSM_W), BF16), pltpu.VMEM((SSM_W, 4 * SSM_HALF), BF16),
                        pltpu.VMEM((bsz, t, LANES), F32), pltpu.VMEM((bsz, t, LANES), F32)],
        compiler_params=_cparams("arbitrary", "arbitrary"),
        name="s5_chunk_out",
    )(x, mods, s_in, bd, wcc, d_row)


def _s5_mixer(x_lat, x_ctx, mods_lat, mods_ctx, lam_re, lam_im, log_dt, b_re, b_im, c_re, c_im, d_skip):
    bsz, n, _ = x_lat.shape
    assert 2 * bsz == SSM_ROWS, "state rows are (direction, batch) on the 8 sublanes"
    wbc, wcc, bd, a = _s5_prep(lam_re, lam_im, log_dt, b_re, b_im, c_re, c_im)
    t_lat = _tile(n, 1024)
    s_lat = _s5_states(x_lat, mods_lat, wbc, t=t_lat)
    s_ctx = _s5_states(x_ctx, mods_ctx, wbc, t=x_ctx.shape[1])
    a4 = a.reshape(SSM_NQ, SSM_QG, 2, 2, LANES)[..., :SSM_STATE]
    a_rows = jnp.repeat(a4.transpose(2, 0, 3, 1, 4).reshape(2, -1), bsz, axis=0)
    a_rows = a_rows.reshape(SSM_ROWS, SSM_NQ * SSM_VR, LANES).transpose(1, 0, 2)
    s_in = _s5_scan(s_ctx, s_lat, a_rows)
    return _s5_out(x_lat, mods_lat, s_in, bd, wcc, d_skip.reshape(1, -1), t=t_lat)


def _gmlp_kernel(x_ref, mod_ref, win_ref, bin_ref, lg_ref, lb_ref, ws_ref, bs_ref, wout_ref, g_ref, b_ref, o_ref,
                 v_ref, gated_ref):
    m = mod_ref[0]
    x = x_ref[0]
    t = x.shape[0]
    hw = GMLP_HEAD_DIM
    h = (x * (1.0 + m[1:2]) + m[0:1]).astype(BF16)

    def z_cols(c0):
        return jax.nn.gelu(_dot(h, win_ref[:, c0:c0 + hw]) + bin_ref[:, c0:c0 + hw])

    total = jnp.zeros((t, 1), F32)
    for hd in range(GMLP_HEADS):
        zc = z_cols(GMLP_HALF + hd * hw)
        v_ref[:, hd * hw:(hd + 1) * hw] = zc
        total = total + jnp.sum(zc, axis=-1, keepdims=True)
    mu = total / GMLP_HALF
    sq = jnp.zeros((t, 1), F32)
    for hd in range(GMLP_HEADS):
        vc = v_ref[:, hd * hw:(hd + 1) * hw] - mu
        sq = sq + jnp.sum(vc * vc, axis=-1, keepdims=True)
    rstd = lax.rsqrt(sq / GMLP_HALF + LN_EPS)
    bs = bs_ref[...]
    for hd in range(GMLP_HEADS):
        cols = slice(hd * hw, (hd + 1) * hw)
        vn = ((v_ref[:, cols] - mu) * rstd * lg_ref[:, cols] + lb_ref[:, cols]).astype(BF16)
        u = z_cols(hd * hw)
        for c in range(t // GMLP_CHUNK):
            rows = slice(c * GMLP_CHUNK, (c + 1) * GMLP_CHUNK)
            gate = _dot(ws_ref[hd], vn[rows]) + bs[:, hd:hd + 1]
            gated_ref[rows, cols] = (u[rows] * gate).astype(BF16)
    for rs in _row_chunks(t):
        y = _dot(gated_ref[rs, :], wout_ref[...])
        o_ref[0, rs, :] = _post_norm(x_ref[0, rs, :], y, m[2:3], g_ref[...], b_ref[...])


def _gmlp(x, mods, w_in, b_in, lg, lb, w_s, b_s_t, w_out, ln_g, ln_b, *, t):
    bsz, n, d = x.shape
    tile = pl.BlockSpec((1, t, d), lambda b, i: (b, i, 0))
    const2 = lambda a: _const_spec(a.shape)
    return pl.pallas_call(
        _gmlp_kernel,
        out_shape=jax.ShapeDtypeStruct((bsz, n, d), F32),
        grid=(bsz, n // t),
        in_specs=[tile, pl.BlockSpec((1, N_MODS, d), lambda b, i: (b, 0, 0)),
                  const2(w_in), const2(b_in), const2(lg), const2(lb), const2(w_s), const2(b_s_t), const2(w_out),
                  const2(ln_g), const2(ln_b)],
        out_specs=tile,
        scratch_shapes=[pltpu.VMEM((t, GMLP_HALF), F32), pltpu.VMEM((t, GMLP_HALF), BF16)],
        compiler_params=_cparams("arbitrary", "arbitrary"),
        name="gmlp_mixer",
    )(x, mods, w_in, b_in, lg, lb, w_s, b_s_t, w_out, ln_g, ln_b)


def _tile(n, pref):
    return pref if n % pref == 0 else n


def _layer(layer, x_lat, x_ctx, mods, p):
    bsz, _, d = x_lat.shape
    lctx = x_ctx.shape[1]
    row = lambda v: v.reshape(1, -1)
    t_lat, t_ctx = 1024, _tile(lctx, 256)
    kind = MIXERS[layer % len(MIXERS)]
    j = layer // len(MIXERS)
    ctx_out = any(MIXERS[m % len(MIXERS)] in CTX_READING_MIXERS for m in range(layer + 1, DEPTH))
    m_lat = mods[layer, :bsz]
    m_ctx = jnp.broadcast_to(mods[layer, bsz:bsz + 1], (bsz, N_MODS, d))
    g1, b1 = row(p["ln1_g"][layer]), row(p["ln1_b"][layer])

    def ffn(xs, ms, t):
        return _ffn(xs, ms, p["ffn_w_up_bf16"], p["ffn_conv_w"][layer], row(p["ffn_conv_b"][layer]),
                    p["ffn_w_down_bf16"], row(p["ln2_g"][layer]), row(p["ln2_b"][layer]),
                    layer=layer, t=t, cc=FFN_CHUNK)

    if kind == "pool":
        args = (p["pool_w"][j].astype(BF16), row(p["pool_b"][j]), row(p["pool_scale"][j]), g1, b1)
        x_lat = _pool(x_lat, m_lat, *args, t=t_lat)
        if ctx_out:
            x_ctx = _pool(x_ctx, m_ctx, *args, t=t_ctx)
    elif kind == "attn":
        wqkv = p["attn_w_qkv"][j].astype(BF16)
        wo = p["attn_w_o"][j].astype(BF16)
        sink = p["attn_sink"][j].astype(F32)
        q, kd, vd = _qkv(x_lat, m_lat, wqkv, t=t_lat, rope=True)
        qc, kdc, vdc = _qkv(x_ctx, m_ctx, wqkv, t=t_ctx, rope=False)
        o_lat = _attn(sink, q, kd, vd, kdc, vdc, local=True)
        x_lat = _proj_norm(o_lat, x_lat, m_lat, wo, g1, b1, t=t_lat)
        if ctx_out:
            o_ctx = _attn(sink, qc, None, None, kdc, vdc, local=False)
            x_ctx = _proj_norm(o_ctx, x_ctx, m_ctx, wo, g1, b1, t=t_ctx)
    elif kind == "ssm":
        assert not ctx_out
        y = _s5_mixer(x_lat, x_ctx, m_lat, m_ctx, p["ssm_lambda_re"][j], p["ssm_lambda_im"][j], p["ssm_log_dt"][j],
                       p["ssm_b_re"][j], p["ssm_b_im"][j], p["ssm_c_re"][j], p["ssm_c_im"][j], p["ssm_d"][j])
        x_lat = _glu_norm(y, x_lat, m_lat, p["ssm_w_glu_a"][j].astype(BF16), p["ssm_w_glu_b"][j].astype(BF16),
                          g1, b1, t=t_lat)
    else:
        assert not ctx_out
        x_lat = _gmlp(x_lat, m_lat, p["gmlp_w_in"][j].astype(BF16), row(p["gmlp_b_in"][j]), row(p["gmlp_ln_g"][j]),
                      row(p["gmlp_ln_b"][j]), p["gmlp_w_s"][j].astype(BF16), p["gmlp_b_s"][j].T,
                      p["gmlp_w_out"][j].astype(BF16), g1, b1, t=t_lat)
    x_lat = ffn(x_lat, m_lat, _tile(x_lat.shape[1], 1024))
    if ctx_out:
        x_ctx = ffn(x_ctx, m_ctx, t_ctx)
    return x_lat, x_ctx


def _mods(c, c_ctx, ada_w, ada_b):
    bsz, d = c.shape
    cond = jnp.concatenate([c, c_ctx[None, :], jnp.zeros((8 - bsz - 1, d), F32)], axis=0)
    return _ada(cond, ada_w, ada_b).reshape(DEPTH, 8, N_MODS, d)


def kernel(x, c, ctx, c_ctx, ada_w, ada_b, ln1_g, ln1_b, ln2_g, ln2_b, ffn_w_up, ffn_conv_w, ffn_conv_b, ffn_w_down, pool_w, pool_b, pool_scale, attn_w_qkv, attn_w_o, attn_sink, ssm_lambda_re, ssm_lambda_im, ssm_log_dt, ssm_b_re, ssm_b_im, ssm_c_re, ssm_c_im, ssm_d, ssm_w_glu_a, ssm_w_glu_b, gmlp_w_in, gmlp_b_in, gmlp_ln_g, gmlp_ln_b, gmlp_w_s, gmlp_b_s, gmlp_w_out):
    bsz, n, d = x.shape
    assert d == D_MODEL and bsz < 8 and n % 512 == 0 and ctx.shape[1] % ATTN_BLOCK == 0
    p = dict(ln1_g=ln1_g, ln1_b=ln1_b, ln2_g=ln2_g, ln2_b=ln2_b, ffn_w_up=ffn_w_up, ffn_conv_w=ffn_conv_w,
             ffn_conv_b=ffn_conv_b, ffn_w_down=ffn_w_down, pool_w=pool_w, pool_b=pool_b, pool_scale=pool_scale,
             attn_w_qkv=attn_w_qkv, attn_w_o=attn_w_o, attn_sink=attn_sink, ssm_lambda_re=ssm_lambda_re,
             ssm_lambda_im=ssm_lambda_im, ssm_log_dt=ssm_log_dt, ssm_b_re=ssm_b_re, ssm_b_im=ssm_b_im,
             ssm_c_re=ssm_c_re, ssm_c_im=ssm_c_im, ssm_d=ssm_d, ssm_w_glu_a=ssm_w_glu_a, ssm_w_glu_b=ssm_w_glu_b,
             gmlp_w_in=gmlp_w_in, gmlp_b_in=gmlp_b_in, gmlp_ln_g=gmlp_ln_g, gmlp_ln_b=gmlp_ln_b, gmlp_w_s=gmlp_w_s,
             gmlp_b_s=gmlp_b_s, gmlp_w_out=gmlp_w_out)
    p["ffn_w_up_bf16"] = ffn_w_up.astype(BF16)
    p["ffn_w_down_bf16"] = ffn_w_down.astype(BF16)
    mods = _mods(c, c_ctx, ada_w, ada_b)
    x_lat, x_ctx = x, ctx
    for layer in range(DEPTH):
        x_lat, x_ctx = _layer(layer, x_lat, x_ctx, mods, p)
    return x_lat
```

```python
import functools
import math

import jax
import jax.numpy as jnp
from jax import lax
from jax.experimental import pallas as pl
from jax.experimental.pallas import tpu as pltpu

F32 = jnp.float32
BF16 = jnp.bfloat16

D_MODEL = 1024
DEPTH = 4
MIXERS = ("pool", "attn", "ssm", "gmlp")
CTX_READING_MIXERS = ("attn", "ssm")
GRID_W = 64
N_MODS = 6
DEEPNORM_ALPHA = (2.0 * DEPTH) ** 0.25
LN_EPS = 1e-5

POOL_WINDOWS = (2, 4, 8, 16)
POOL_GROUP = D_MODEL // len(POOL_WINDOWS)

HEAD_DIM = 64
N_Q_HEADS = D_MODEL // HEAD_DIM
N_KV_HEADS = N_Q_HEADS // 4
Q_WIDTH = N_Q_HEADS * HEAD_DIM
KV_WIDTH = N_KV_HEADS * HEAD_DIM
WINDOW = 128
ATTN_BLOCK = 128
ROPE_BASE = 10000.0
NEG_INF = -1e30
LOG2E = math.log2(math.e)

SSM_GROUP = 16
SSM_N_GROUPS = D_MODEL // SSM_GROUP
SSM_STATE = 64
SSM_CHUNK = 16

GMLP_CHUNK = 128
GMLP_HALF = 2 * D_MODEL
GMLP_HEADS = 8
GMLP_HEAD_DIM = GMLP_HALF // GMLP_HEADS

FFN_HIDDEN = 2816
FFN_CHUNK = 256
OUT_ROWS = 256

LANES = 128
HALO = 16
VMEM_LIMIT = 56 * 1024 * 1024


def _cparams(*sem):
    return pltpu.CompilerParams(dimension_semantics=sem, vmem_limit_bytes=VMEM_LIMIT)


def _const_spec(shape):
    nd = len(shape)
    return pl.BlockSpec(shape, lambda *_: (0,) * nd, pipeline_mode=pl.Buffered(1))


def _post_norm(x, y, gate, g, b):
    z = DEEPNORM_ALPHA * x + gate * y
    mu = jnp.mean(z, axis=-1, keepdims=True)
    zc = z - mu
    var = jnp.mean(zc * zc, axis=-1, keepdims=True)
    return zc * lax.rsqrt(var + LN_EPS) * g + b


def _row_chunks(t):
    return [slice(r, min(r + OUT_ROWS, t)) for r in range(0, t, OUT_ROWS)]


def _dot(a, b):
    return jnp.dot(a, b, preferred_element_type=F32)


def _dot_nt(a, b):
    return lax.dot_general(a, b, (((1,), (1,)), ((), ())), preferred_element_type=F32)


def _ada_kernel(c_ref, w_ref, b_ref, o_ref):
    c = c_ref[...]
    s = (c * jax.nn.sigmoid(c)).astype(BF16)
    o_ref[0] = _dot(s, w_ref[0].astype(BF16)) + b_ref[0]


def _ada(cond, ada_w, ada_b):
    depth, d, n = ada_w.shape
    rows = cond.shape[0]
    tn = 1536
    return pl.pallas_call(
        _ada_kernel,
        out_shape=jax.ShapeDtypeStruct((depth, rows, n), F32),
        grid=(depth, n // tn),
        in_specs=[pl.BlockSpec((rows, d), lambda l, j: (0, 0)),
                  pl.BlockSpec((1, d, tn), lambda l, j: (l, 0, j)),
                  pl.BlockSpec((1, 1, tn), lambda l, j: (l, 0, j))],
        out_specs=pl.BlockSpec((1, rows, tn), lambda l, j: (l, 0, j)),
        compiler_params=_cparams("arbitrary", "arbitrary"),
        name="ada",
    )(cond, ada_w, ada_b.reshape(depth, 1, n))


def _halo_specs(t, n):
    per = t // HALO
    last = n // HALO - 1
    prev = pl.BlockSpec((1, HALO, D_MODEL), lambda b, i, *_: (b, jnp.maximum(i * per - 1, 0), 0))
    nxt = pl.BlockSpec((1, HALO, D_MODEL), lambda b, i, *_: (b, jnp.minimum((i + 1) * per, last), 0))
    return prev, nxt


def _ffn_kernel(xp_ref, x_ref, xn_ref, mod_ref, wu_ref, cw_ref, cb_ref, wd_ref, g_ref, b_ref, o_ref, act_ref,
                *, t, nt, f, cc):
    i = pl.program_id(1)
    rows = t + 2 * HALO
    m = mod_ref[0]
    sh, sc = m[3:4], 1.0 + m[4:5]
    keep_p = jnp.where(i > 0, 1.0, 0.0)
    keep_n = jnp.where(i < nt - 1, 1.0, 0.0)
    x = x_ref[0]
    h = jnp.concatenate([((xp_ref[0] * sc + sh) * keep_p).astype(BF16), (x * sc + sh).astype(BF16),
                         ((xn_ref[0] * sc + sh) * keep_n).astype(BF16)], axis=0)

    def conv(off):
        u = _dot(h, wu_ref[:, off:off + cc])
        cw = cw_ref[:, off:off + cc]
        a = cb_ref[:, off:off + cc] + pltpu.roll(u, 1, 0) * cw[0:1]
        a = a + u * cw[1:2]
        a = a + pltpu.roll(u, rows - 1, 0) * cw[2:3]
        return a[HALO:HALO + t]

    for c in range(f // cc):
        val = conv(c * cc)
        gate = conv(f + c * cc)
        act_ref[:, c * cc:(c + 1) * cc] = (val * (gate * jax.nn.sigmoid(gate))).astype(BF16)
    chunks = _row_chunks(t)
    ys = [_dot(act_ref[rs, :], wd_ref[...]) for rs in chunks]
    for rs, y in zip(chunks, ys):
        o_ref[0, rs, :] = _post_norm(x_ref[0, rs, :], y, m[5:6], g_ref[...], b_ref[...])


def _layer_spec(shape, layer):
    nd = len(shape) - 1
    return pl.BlockSpec((None,) + tuple(shape[1:]), lambda *_: (layer,) + (0,) * nd, pipeline_mode=pl.Buffered(1))


def _ffn(x, mods, w_up, conv_w, conv_b, w_down, ln_g, ln_b, *, layer, t, cc):
    bsz, n, d = x.shape
    f = w_down.shape[1]
    nt = n // t
    prev, nxt = _halo_specs(t, n)
    kern = functools.partial(_ffn_kernel, t=t, nt=nt, f=f, cc=cc)
    return pl.pallas_call(
        kern,
        out_shape=jax.ShapeDtypeStruct((bsz, n, d), F32),
        grid=(bsz, nt),
        in_specs=[prev,
                  pl.BlockSpec((1, t, d), lambda b, i: (b, i, 0)),
                  nxt,
                  pl.BlockSpec((1, N_MODS, d), lambda b, i: (b, 0, 0)),
                  _layer_spec(w_up.shape, layer), _const_spec(conv_w.shape), _const_spec(conv_b.shape),
                  _layer_spec(w_down.shape, layer), _const_spec(ln_g.shape), _const_spec(ln_b.shape)],
        out_specs=pl.BlockSpec((1, t, d), lambda b, i: (b, i, 0)),
        scratch_shapes=[pltpu.VMEM((t, f), BF16)],
        compiler_params=_cparams("arbitrary", "arbitrary"),
        name="conv_ffn",
    )(x, x, x, mods, w_up, conv_w, conv_b, w_down, ln_g, ln_b)


def _pool_kernel(xp_ref, x_ref, xn_ref, mod_ref, w_ref, pb_ref, ps_ref, g_ref, b_ref, o_ref, *, t, nt, n):
    i = pl.program_id(1)
    rows = t + 2 * HALO
    m = mod_ref[0]
    sh, sc = m[0:1], 1.0 + m[1:2]
    keep_p = jnp.where(i > 0, 1.0, 0.0)
    keep_n = jnp.where(i < nt - 1, 1.0, 0.0)
    x = x_ref[0]
    h = jnp.concatenate([(xp_ref[0] * sc + sh) * keep_p, x * sc + sh, (xn_ref[0] * sc + sh) * keep_n], axis=0)
    pos = i * t + lax.broadcasted_iota(jnp.int32, (t, 1), 0)
    outs = []
    for gi, win in enumerate(POOL_WINDOWS):
        hg = h[:, gi * POOL_GROUP:(gi + 1) * POOL_GROUP]
        s = hg + pltpu.roll(hg, 1, 0)
        half = 1
        while 2 * half < win:
            s = pltpu.roll(s, half, 0) + pltpu.roll(s, rows - half, 0)
            half *= 2
        lo = jnp.maximum(pos - win // 2, 0)
        hi = jnp.minimum(pos - win // 2 + win, n)
        mean = s[HALO:HALO + t] / (hi - lo).astype(F32)
        mixed = (mean - hg[HALO:HALO + t]).astype(BF16)
        outs.append(_dot(mixed, w_ref[gi]))
    y = (jnp.concatenate(outs, axis=1) + pb_ref[...]) * ps_ref[...]
    o_ref[0] = _post_norm(x, y, m[2:3], g_ref[...], b_ref[...])


def _pool(x, mods, w, pb, ps, ln_g, ln_b, *, t):
    bsz, n, d = x.shape
    nt = n // t
    prev, nxt = _halo_specs(t, n)
    kern = functools.partial(_pool_kernel, t=t, nt=nt, n=n)
    vec = pl.BlockSpec((1, d), lambda b, i: (0, 0))
    return pl.pallas_call(
        kern,
        out_shape=jax.ShapeDtypeStruct((bsz, n, d), F32),
        grid=(bsz, nt),
        in_specs=[prev, pl.BlockSpec((1, t, d), lambda b, i: (b, i, 0)), nxt,
                  pl.BlockSpec((1, N_MODS, d), lambda b, i: (b, 0, 0)),
                  pl.BlockSpec(w.shape, lambda b, i: (0, 0, 0)),
                  vec, vec, vec, vec],
        out_specs=pl.BlockSpec((1, t, d), lambda b, i: (b, i, 0)),
        compiler_params=_cparams("arbitrary", "arbitrary"),
        name="pool_mixer",
    )(x, x, x, mods, w, pb, ps, ln_g, ln_b)


def _rope_tables(n):
    tpos = jnp.arange(n)
    half = HEAD_DIM // 4
    freqs = ROPE_BASE ** (-jnp.arange(half, dtype=F32) / half)
    ang_r = (tpos // GRID_W).astype(F32)[:, None] * freqs[None, :]
    ang_c = (tpos % GRID_W).astype(F32)[:, None] * freqs[None, :]
    zero = jnp.zeros_like(ang_r)
    cr, sr, cc, sc = jnp.cos(ang_r), jnp.sin(ang_r), jnp.cos(ang_c), jnp.sin(ang_c)
    cos = jnp.tile(jnp.concatenate([cr, cr, cc, cc], axis=1), (1, 2))
    sin_first = jnp.tile(jnp.concatenate([-sr, zero, -sc, zero], axis=1), (1, 2))
    sin_second = jnp.tile(jnp.concatenate([zero, sr, zero, sc], axis=1), (1, 2))
    return cos, sin_first, sin_second


def _dup_heads(chunk, lo):
    sw = pltpu.roll(chunk, HEAD_DIM, 1)
    return jnp.where(lo, chunk, sw), jnp.where(lo, sw, chunk)


def _qkv_kernel(*refs, rope):
    if rope:
        x_ref, mod_ref, w_ref, cos_ref, sa_ref, sb_ref, q_ref, kd_ref, vt_ref = refs
    else:
        x_ref, mod_ref, w_ref, q_ref, kd_ref, vt_ref = refs
    m = mod_ref[0]
    h = (x_ref[0] * (1.0 + m[1:2]) + m[0:1]).astype(BF16)
    qkv = _dot(h, w_ref[...])
    t = qkv.shape[0]
    lo = lax.broadcasted_iota(jnp.int32, (t, LANES), 1) < HEAD_DIM

    def proj(col):
        return qkv[:, col:col + LANES]

    scale = HEAD_DIM ** -0.5 * LOG2E

    def rot(v):
        if not rope:
            return v
        quarter = HEAD_DIM // 4
        return (v * cos_ref[...] + pltpu.roll(v, LANES - quarter, 1) * sa_ref[...]
                + pltpu.roll(v, quarter, 1) * sb_ref[...])

    for c in range(Q_WIDTH // LANES):
        q_ref[0, :, c * LANES:(c + 1) * LANES] = (rot(proj(c * LANES)) * scale).astype(BF16)
    for c in range(KV_WIDTH // LANES):
        k0, k1 = _dup_heads(rot(proj(Q_WIDTH + c * LANES)), lo)
        kd_ref[0, :, (2 * c) * LANES:(2 * c + 1) * LANES] = k0.astype(BF16)
        kd_ref[0, :, (2 * c + 1) * LANES:(2 * c + 2) * LANES] = k1.astype(BF16)
        v = proj(Q_WIDTH + KV_WIDTH + c * LANES)
        vt_ref[0, c * LANES:(c + 1) * LANES, :] = jnp.transpose(v).astype(BF16)


def _qkv(x, mods, w_qkv, *, t, rope):
    bsz, n, d = x.shape
    kdw = N_KV_HEADS * LANES
    ins = [x, mods, w_qkv]
    specs = [pl.BlockSpec((1, t, d), lambda b, i: (b, i, 0)),
             pl.BlockSpec((1, N_MODS, d), lambda b, i: (b, 0, 0)),
             pl.BlockSpec(w_qkv.shape, lambda b, i: (0, 0))]
    if rope:
        ins += list(_rope_tables(n))
        specs += [pl.BlockSpec((t, LANES), lambda b, i: (i, 0))] * 3
    return pl.pallas_call(
        functools.partial(_qkv_kernel, rope=rope),
        out_shape=(jax.ShapeDtypeStruct((bsz, n, Q_WIDTH), BF16),
                   jax.ShapeDtypeStruct((bsz, n, kdw), BF16),
                   jax.ShapeDtypeStruct((bsz, KV_WIDTH, n), BF16)),
        grid=(bsz, n // t),
        in_specs=specs,
        out_specs=(pl.BlockSpec((1, t, Q_WIDTH), lambda b, i: (b, i, 0)),
                   pl.BlockSpec((1, t, kdw), lambda b, i: (b, i, 0)),
                   pl.BlockSpec((1, KV_WIDTH, t), lambda b, i: (b, 0, i))),
        compiler_params=_cparams("arbitrary", "arbitrary"),
        name="qkv_rope" if rope else "qkv_ctx",
    )(*ins)


def _attn_kernel(sink_ref, q_ref, *refs, nb, local):
    if local:
        kp_ref, kc_ref, kn_ref, kx_ref, vp_ref, vc_ref, vn_ref, vx_ref, o_ref = refs
        k_refs, v_refs = (kp_ref, kc_ref, kn_ref, kx_ref), (vp_ref, vc_ref, vn_ref, vx_ref)
    else:
        kx_ref, vx_ref, o_ref = refs
        k_refs, v_refs = (kx_ref,), (vx_ref,)
    blk = ATTN_BLOCK
    group = N_Q_HEADS // N_KV_HEADS
    cols = group * blk
    nblk = pl.program_id(1)
    if local:
        kj = lax.broadcasted_iota(jnp.int32, (blk, cols), 0)
        qi = lax.broadcasted_iota(jnp.int32, (blk, cols), 1) % blk
        valid_prev = (kj >= qi) & (nblk > 0)
        valid_next = (kj <= qi) & (nblk < nb - 1)
    lo = lax.broadcasted_iota(jnp.int32, (blk, LANES), 1) < HEAD_DIM
    head_of_col = lax.broadcasted_iota(jnp.int32, (1, cols), 1) // blk
    scores = []
    for hk in range(N_KV_HEADS):
        kx = jnp.concatenate([r[0, :, hk * LANES:(hk + 1) * LANES] for r in k_refs], axis=0)
        parts = []
        for c in range(group // 2):
            qc = q_ref[0, :, (hk * group // 2 + c) * LANES:(hk * group // 2 + c + 1) * LANES]
            zero = jnp.zeros_like(qc)
            parts += [jnp.where(lo, qc, zero), jnp.where(lo, zero, qc)]
        q4 = jnp.concatenate(parts, axis=0)
        scores.append(_dot_nt(kx, q4))
    probs = []
    for hk, s in enumerate(scores):
        if local:
            s = jnp.concatenate([jnp.where(valid_prev, s[:blk], NEG_INF), s[blk:2 * blk],
                                 jnp.where(valid_next, s[2 * blk:3 * blk], NEG_INF), s[3 * blk:]], axis=0)
        sink = jnp.zeros((1, cols), F32)
        for g in range(group):
            sink = jnp.where(head_of_col == g, sink_ref[hk * group + g] * LOG2E, sink)
        mx = jnp.maximum(jnp.max(s, axis=0, keepdims=True), sink)
        p = jnp.exp2(s - mx)
        den = jnp.sum(p, axis=0, keepdims=True) + jnp.exp2(sink - mx)
        probs.append((p.astype(BF16), den))
    for hk, (p, den) in enumerate(probs):
        vt = jnp.concatenate([r[0, hk * HEAD_DIM:(hk + 1) * HEAD_DIM, :] for r in v_refs], axis=1)
        ot = _dot(vt, p) / den
        for c in range(group // 2):
            pair = jnp.concatenate([ot[:, (2 * c) * blk:(2 * c + 1) * blk],
                                    ot[:, (2 * c + 1) * blk:(2 * c + 2) * blk]], axis=0)
            o_ref[0, :, (hk * group // 2 + c) * LANES:(hk * group // 2 + c + 1) * LANES] = (
                jnp.transpose(pair).astype(BF16))


def _attn(sink, q, kd, vt, kd_ctx, vt_ctx, *, local):
    bsz, n, _ = q.shape
    blk = ATTN_BLOCK
    assert WINDOW == blk, "the band is exactly the previous, own and next key block"
    nb = n // blk
    lctx = kd_ctx.shape[1]
    kdw = kd_ctx.shape[2]
    smem = pl.BlockSpec(memory_space=pltpu.SMEM)
    qspec = pl.BlockSpec((1, blk, Q_WIDTH), lambda b, i: (b, i, 0))
    kctx_spec = pl.BlockSpec((1, lctx, kdw), lambda b, i: (b, 0, 0))
    vctx_spec = pl.BlockSpec((1, KV_WIDTH, lctx), lambda b, i: (b, 0, 0))
    if local:
        prev, nxt = (lambda i: jnp.maximum(i - 1, 0)), (lambda i: jnp.minimum(i + 1, nb - 1))
        kband = [pl.BlockSpec((1, blk, kdw), lambda b, i: (b, prev(i), 0)),
                 pl.BlockSpec((1, blk, kdw), lambda b, i: (b, i, 0)),
                 pl.BlockSpec((1, blk, kdw), lambda b, i: (b, nxt(i), 0))]
        vband = [pl.BlockSpec((1, KV_WIDTH, blk), lambda b, i: (b, 0, prev(i))),
                 pl.BlockSpec((1, KV_WIDTH, blk), lambda b, i: (b, 0, i)),
                 pl.BlockSpec((1, KV_WIDTH, blk), lambda b, i: (b, 0, nxt(i)))]
        specs = [smem, qspec] + kband + [kctx_spec] + vband + [vctx_spec]
        args = (sink, q, kd, kd, kd, kd_ctx, vt, vt, vt, vt_ctx)
    else:
        specs = [smem, qspec, kctx_spec, vctx_spec]
        args = (sink, q, kd_ctx, vt_ctx)
    return pl.pallas_call(
        functools.partial(_attn_kernel, nb=nb, local=local),
        out_shape=jax.ShapeDtypeStruct((bsz, n, Q_WIDTH), BF16),
        grid=(bsz, nb),
        in_specs=specs,
        out_specs=pl.BlockSpec((1, blk, Q_WIDTH), lambda b, i: (b, i, 0)),
        compiler_params=_cparams("arbitrary", "arbitrary"),
        name="banded_attn" if local else "ctx_attn",
    )(*args)


def _proj_norm_kernel(a_ref, x_ref, mod_ref, w_ref, g_ref, b_ref, o_ref):
    chunks = _row_chunks(x_ref.shape[1])
    ys = [_dot(a_ref[0, rs, :], w_ref[...]) for rs in chunks]
    for rs, y in zip(chunks, ys):
        o_ref[0, rs, :] = _post_norm(x_ref[0, rs, :], y, mod_ref[0][2:3], g_ref[...], b_ref[...])


def _proj_norm(a, x, mods, w, ln_g, ln_b, *, t):
    bsz, n, d = x.shape
    ka = a.shape[2]
    vec = pl.BlockSpec((1, d), lambda b, i: (0, 0))
    return pl.pallas_call(
        _proj_norm_kernel,
        out_shape=jax.ShapeDtypeStruct((bsz, n, d), F32),
        grid=(bsz, n // t),
        in_specs=[pl.BlockSpec((1, t, ka), lambda b, i: (b, i, 0)),
                  pl.BlockSpec((1, t, d), lambda b, i: (b, i, 0)),
                  pl.BlockSpec((1, N_MODS, d), lambda b, i: (b, 0, 0)),
                  pl.BlockSpec(w.shape, lambda b, i: (0, 0)),
                  vec, vec],
        out_specs=pl.BlockSpec((1, t, d), lambda b, i: (b, i, 0)),
        compiler_params=_cparams("arbitrary", "arbitrary"),
        name="attn_out_norm",
    )(a, x, mods, w, ln_g, ln_b)


def _glu_norm_kernel(y_ref, x_ref, mod_ref, wa_ref, wb_ref, g_ref, b_ref, o_ref):
    gl = jax.nn.gelu(y_ref[0]).astype(BF16)
    out = _dot(gl, wa_ref[...]) * jax.nn.sigmoid(_dot(gl, wb_ref[...]))
    o_ref[0] = _post_norm(x_ref[0], out, mod_ref[0][2:3], g_ref[...], b_ref[...])


def _glu_norm(y, x, mods, wa, wb, ln_g, ln_b, *, t):
    bsz, n, d = x.shape
    tile = pl.BlockSpec((1, t, d), lambda b, i: (b, i, 0))
    vec = pl.BlockSpec((1, d), lambda b, i: (0, 0))
    wspec = pl.BlockSpec((d, d), lambda b, i: (0, 0))
    return pl.pallas_call(
        _glu_norm_kernel,
        out_shape=jax.ShapeDtypeStruct((bsz, n, d), F32),
        grid=(bsz, n // t),
        in_specs=[tile, tile, pl.BlockSpec((1, N_MODS, d), lambda b, i: (b, 0, 0)), wspec, wspec, vec, vec],
        out_specs=tile,
        compiler_params=_cparams("arbitrary", "arbitrary"),
        name="ssm_glu_norm",
    )(y, x, mods, wa, wb, ln_g, ln_b)


SSM_QG = LANES // SSM_GROUP
SSM_NQ = D_MODEL // LANES
SSM_ROWS = 8
SSM_W = SSM_CHUNK * LANES
SSM_HALF = SSM_QG * SSM_STATE
SSM_TAP = SSM_CHUNK * SSM_GROUP
SSM_VR = 2 * SSM_HALF // LANES


def _s5_prep_kernel(lr_ref, li_ref, ldt_ref, btr_ref, bti_ref, cr_ref, ci_ref, wb_ref, wc_ref, bd_ref, a_ref):
    c = SSM_CHUNK
    q = pl.program_id(0)
    lag = lax.broadcasted_iota(jnp.int32, (c + 1, LANES), 0).astype(F32)
    lane = lax.broadcasted_iota(jnp.int32, (1, LANES), 1)
    taps = [[[] for _ in range(c)] for _ in range(2)]
    for g8 in range(SSM_QG):
        own = jnp.where((lane < SSM_STATE) == (g8 % 2 == 0), 1.0, 0.0)
        wb_cols, wc_cols, a_cols = [], [], []
        for d in range(2):
            lr, li = lr_ref[g8, d], li_ref[g8, d]
            dt = jnp.exp(jnp.full((1, LANES), ldt_ref[q * SSM_QG + g8, d], F32))
            mag = jnp.exp(lag * (lr * dt))
            ang = lag * (li * dt)
            pw_r, pw_i = mag * jnp.cos(ang), mag * jnp.sin(ang)
            lbr, lbi = pw_r[1:2], pw_i[1:2]
            den = lr * lr + li * li
            qr = ((lbr - 1.0) * lr + lbi * li) / den
            qi = (lbi * lr - (lbr - 1.0) * li) / den
            btr, bti = btr_ref[g8, d], bti_ref[g8, d]
            bbr = qr * btr - qi * bti
            bbi = qr * bti + qi * btr
            cr, ci = cr_ref[g8, d], ci_ref[g8, d]

            def cl(j):
                return cr * pw_r[j:j + 1] - ci * pw_i[j:j + 1], -(cr * pw_i[j:j + 1] + ci * pw_r[j:j + 1])

            def bl(j):
                return bbr * pw_r[j:j + 1] - bbi * pw_i[j:j + 1], bbr * pw_i[j:j + 1] + bbi * pw_r[j:j + 1]

            e = jnp.concatenate([jnp.concatenate(cl(j), axis=1) for j in range(c)], axis=0)
            pieces = [jnp.concatenate([bbr * own, bbi * own], axis=1)]
            if g8 > 0:
                pieces.insert(0, jnp.zeros((g8 * SSM_GROUP, 2 * LANES), F32))
            if g8 < SSM_QG - 1:
                pieces.append(jnp.zeros(((SSM_QG - 1 - g8) * SSM_GROUP, 2 * LANES), F32))
            kt = lax.dot_general(e, jnp.concatenate(pieces, axis=0), (((1,), (1,)), ((), ())),
                                 preferred_element_type=F32, precision=lax.Precision.HIGHEST)
            for j in range(c):
                taps[d][j].append(kt[j * SSM_GROUP:(j + 1) * SSM_GROUP])
            wbl = [bl(c - 1 - k) if d == 0 else bl(k) for k in range(c)]
            wb_cols += [jnp.concatenate([w[0] for w in wbl], axis=0) * own,
                        jnp.concatenate([w[1] for w in wbl], axis=0) * own]
            wcl = [cl(k + 1) if d == 0 else cl(c - k) for k in range(c)]
            wc_cols += [jnp.concatenate([w[0] for w in wcl], axis=0) * own,
                        jnp.concatenate([w[1] for w in wcl], axis=0) * own]
            a_cols += [pw_r[c:c + 1], pw_i[c:c + 1]]
        wb_ref[g8] = jnp.concatenate(wb_cols, axis=1).astype(BF16)
        wc_ref[g8] = jnp.concatenate(wc_cols, axis=1).astype(BF16)
        a_ref[g8] = jnp.concatenate(a_cols, axis=1)
    for d in range(2):
        for j in range(c):
            blk = jnp.concatenate(taps[d][j], axis=0)
            if d == 0 and j == 0:
                blk = blk + jnp.concatenate(taps[1][0], axis=0)
            bd_ref[0, d, j] = blk.astype(BF16)


def _s5_prep(lam_re, lam_im, log_dt, b_re, b_im, c_re, c_im):
    g, nq, qg = SSM_N_GROUPS, SSM_NQ, SSM_QG
    per_g = lambda a: jnp.swapaxes(a, 0, 1)
    dup = lambda a: jnp.concatenate([a, a], axis=-1)
    lam_spec = pl.BlockSpec((qg, 2, 1, LANES), lambda i: (i, 0, 0, 0))
    mat_spec = pl.BlockSpec((qg, 2, SSM_GROUP, LANES), lambda i: (i, 0, 0, 0))
    w_spec = pl.BlockSpec((qg, SSM_TAP, 4 * LANES), lambda i: (i, 0, 0))
    return pl.pallas_call(
        _s5_prep_kernel,
        out_shape=(jax.ShapeDtypeStruct((g, SSM_TAP, 4 * LANES), BF16),
                   jax.ShapeDtypeStruct((g, SSM_TAP, 4 * LANES), BF16),
                   jax.ShapeDtypeStruct((nq, 2, SSM_CHUNK, LANES, LANES), BF16),
                   jax.ShapeDtypeStruct((g, 1, 4 * LANES), F32)),
        grid=(nq,),
        in_specs=[lam_spec, lam_spec, pl.BlockSpec(memory_space=pltpu.SMEM), mat_spec, mat_spec, mat_spec, mat_spec],
        out_specs=(w_spec, w_spec, pl.BlockSpec((1, 2, SSM_CHUNK, LANES, LANES), lambda i: (i, 0, 0, 0, 0)),
                   pl.BlockSpec((qg, 1, 4 * LANES), lambda i: (i, 0, 0))),
        compiler_params=_cparams("arbitrary"),
        name="s5_prep",
    )(dup(per_g(lam_re))[:, :, None, :], dup(per_g(lam_im))[:, :, None, :], per_g(log_dt),
      dup(jnp.swapaxes(per_g(b_re), 2, 3)), dup(jnp.swapaxes(per_g(b_im), 2, 3)), dup(per_g(c_re)), dup(per_g(c_im)))


def _s5_expand(dst_ref, src_ref):
    dst_ref[...] = jnp.zeros_like(dst_ref)
    for g8 in range(SSM_QG):
        for k in range(SSM_CHUNK):
            for c4 in range(4):
                r0 = k * LANES + g8 * SSM_GROUP
                c0 = c4 * SSM_HALF + (g8 // 2) * LANES
                dst_ref[r0:r0 + SSM_GROUP, c0:c0 + LANES] = src_ref[g8, k * SSM_GROUP:(k + 1) * SSM_GROUP,
                                                                    c4 * LANES:(c4 + 1) * LANES]


def _s5_modulate(x_ref, mod_ref, u_ref, bsz):
    for b in range(bsz):
        m = mod_ref[b]
        u_ref[b] = x_ref[b] * (1.0 + m[1:2]) + m[0:1]


def _s5_gather(u_ref, bsz, nchunk):
    rows = [jnp.concatenate([u_ref[b, pl.ds(pos, nchunk, stride=SSM_CHUNK), :] for pos in range(SSM_CHUNK)],
                            axis=1).astype(BF16) for b in range(bsz)]
    return jnp.concatenate(rows, axis=0)


def _s5_states_kernel(x_ref, mod_ref, wbc_ref, o_ref, wb_ref, u_ref, *, bsz, nchunk):
    @pl.when(pl.program_id(1) == 0)
    def _():
        _s5_expand(wb_ref, wbc_ref)

    _s5_modulate(x_ref, mod_ref, u_ref, bsz)
    s = _dot(_s5_gather(u_ref, bsz, nchunk), wb_ref[...])
    for b in range(bsz):
        for v in range(SSM_VR):
            for d in range(2):
                o_ref[v, pl.ds(d * bsz + b, nchunk, stride=SSM_ROWS), :] = (
                    s[b * nchunk:(b + 1) * nchunk, (d * SSM_VR + v) * LANES:(d * SSM_VR + v + 1) * LANES])


def _s5_states(x, mods, wbc, *, t):
    bsz, n, _ = x.shape
    nchunk = t // SSM_CHUNK
    return pl.pallas_call(
        functools.partial(_s5_states_kernel, bsz=bsz, nchunk=nchunk),
        out_shape=jax.ShapeDtypeStruct((SSM_NQ * SSM_VR, n // SSM_CHUNK * SSM_ROWS, LANES), F32),
        grid=(SSM_NQ, n // t),
        in_specs=[pl.BlockSpec((bsz, t, LANES), lambda q, i: (0, i, q)),
                  pl.BlockSpec((bsz, N_MODS, LANES), lambda q, i: (0, 0, q)),
                  pl.BlockSpec((SSM_QG, SSM_TAP, 4 * LANES), lambda q, i: (q, 0, 0))],
        out_specs=pl.BlockSpec((SSM_VR, nchunk * SSM_ROWS, LANES), lambda q, i: (q, i, 0)),
        scratch_shapes=[pltpu.VMEM((SSM_W, 4 * SSM_HALF), BF16), pltpu.VMEM((bsz, t, LANES), F32)],
        compiler_params=_cparams("arbitrary", "arbitrary"),
        name="s5_chunk_states",
    )(x, mods, wbc)


def _s5_scan_kernel(xc_ref, xl_ref, a_ref, o_ref):
    h = SSM_VR // 2
    ar, ai = a_ref[:h], a_ref[h:]
    nc, nl = xc_ref.shape[1] // SSM_ROWS, xl_ref.shape[1] // SSM_ROWS
    fwd_rows = lax.broadcasted_iota(jnp.int32, (SSM_VR, SSM_ROWS, LANES), 1) < SSM_ROWS // 2

    assert nl % 2 == 0
    chunk = lambda i: pl.ds(pl.multiple_of(i * SSM_ROWS, SSM_ROWS), SSM_ROWS)

    def step(ref, n, k, s):
        x = jnp.where(fwd_rows, ref[:, chunk(k), :], ref[:, chunk(n - 1 - k), :])
        return ar * s[0] - ai * s[1] + x[:h], ar * s[1] + ai * s[0] + x[h:]

    def first_touch(k, s):
        full = jnp.concatenate(s, axis=0)
        o_ref[:, chunk(k), :] = full
        o_ref[:, chunk(nl - 1 - k), :] = full
        return step(xl_ref, nl, k, s)

    def second_touch(k, s):
        full = jnp.concatenate(s, axis=0)
        o_ref[:, chunk(k), :] = jnp.where(fwd_rows, full, o_ref[:, chunk(k), :])
        o_ref[:, chunk(nl - 1 - k), :] = jnp.where(fwd_rows, o_ref[:, chunk(nl - 1 - k), :], full)
        return step(xl_ref, nl, k, s)

    zero = (jnp.zeros((h, SSM_ROWS, LANES), F32), jnp.zeros((h, SSM_ROWS, LANES), F32))
    s = lax.fori_loop(0, nc, lambda k, s: step(xc_ref, nc, k, s), zero)
    s = lax.fori_loop(0, nl // 2, first_touch, s)
    lax.fori_loop(nl // 2, nl, second_touch, s)


def _s5_scan(s_ctx, s_lat, a_rows):
    blk = lambda rows: pl.BlockSpec((SSM_VR, rows, LANES), lambda q: (q, 0, 0))
    return pl.pallas_call(
        _s5_scan_kernel,
        out_shape=jax.ShapeDtypeStruct(s_lat.shape, F32),
        grid=(SSM_NQ,),
        in_specs=[blk(s_ctx.shape[1]), blk(s_lat.shape[1]), blk(SSM_ROWS)],
        out_specs=blk(s_lat.shape[1]),
        compiler_params=_cparams("arbitrary"),
        name="s5_chunk_scan",
    )(s_ctx, s_lat, a_rows)


def _s5_out_kernel(x_ref, mod_ref, s_ref, bd_ref, wcc_ref, d_ref, o_ref, toep_ref, wc_ref, u_ref, y_ref, *,
                   bsz, nchunk):
    c = SSM_CHUNK

    @pl.when(pl.program_id(1) == 0)
    def _():
        _s5_expand(wc_ref, wcc_ref)
        for t in range(c):
            for k in range(c):
                toep_ref[t * LANES:(t + 1) * LANES, k * LANES:(k + 1) * LANES] = (
                    bd_ref[0, 0, t - k] if t >= k else bd_ref[0, 1, k - t])

    _s5_modulate(x_ref, mod_ref, u_ref, bsz)
    lhs_s = jnp.concatenate(
        [jnp.concatenate([s_ref[v, pl.ds(d * bsz + b, nchunk, stride=SSM_ROWS), :]
                          for d in range(2) for v in range(SSM_VR)], axis=1).astype(BF16)
         for b in range(bsz)], axis=0)
    y = _dot_nt(_s5_gather(u_ref, bsz, nchunk), toep_ref[...]) + _dot_nt(lhs_s, wc_ref[...])
    for b in range(bsz):
        for t in range(c):
            y_ref[b, pl.ds(t, nchunk, stride=c), :] = y[b * nchunk:(b + 1) * nchunk, t * LANES:(t + 1) * LANES]
    for b in range(bsz):
        o_ref[b] = y_ref[b] + d_ref[...] * u_ref[b]


def _s5_out(x, mods, s_in, bd, wcc, d_row, *, t):
    bsz, n, d = x.shape
    nchunk = t // SSM_CHUNK
    tile = pl.BlockSpec((bsz, t, LANES), lambda q, i: (0, i, q))
    return pl.pallas_call(
        functools.partial(_s5_out_kernel, bsz=bsz, nchunk=nchunk),
        out_shape=jax.ShapeDtypeStruct((bsz, n, d), F32),
        grid=(SSM_NQ, n // t),
        in_specs=[tile,
                  pl.BlockSpec((bsz, N_MODS, LANES), lambda q, i: (0, 0, q)),
                  pl.BlockSpec((SSM_VR, nchunk * SSM_ROWS, LANES), lambda q, i: (q, i, 0)),
                  pl.BlockSpec((1, 2, SSM_CHUNK, LANES, LANES), lambda q, i: (q, 0, 0, 0, 0)),
                  pl.BlockSpec((SSM_QG, SSM_TAP, 4 * LANES), lambda q, i: (q, 0, 0)),
                  pl.BlockSpec((1, LANES), lambda q, i: (0, q))],
        out_specs=tile,
        scratch_shapes=[pltpu.VMEM((SSM_W, SSM_W), BF16), pltpu.VMEM((SSM_W, 4 * SSM_HALF), BF16),
                        pltpu.VMEM((bsz, t, LANES), F32), pltpu.VMEM((bsz, t, LANES), F32)],
        compiler_params=_cparams("arbitrary", "arbitrary"),
        name="s5_chunk_out",
    )(x, mods, s_in, bd, wcc, d_row)


def _s5_mixer(x_lat, x_ctx, mods_lat, mods_ctx, lam_re, lam_im, log_dt, b_re, b_im, c_re, c_im, d_skip):
    bsz, n, _ = x_lat.shape
    assert 2 * bsz == SSM_ROWS, "state rows are (direction, batch) on the 8 sublanes"
    wbc, wcc, bd, a = _s5_prep(lam_re, lam_im, log_dt, b_re, b_im, c_re, c_im)
    t_lat = _tile(n, 1024)
    s_lat = _s5_states(x_lat, mods_lat, wbc, t=t_lat)
    s_ctx = _s5_states(x_ctx, mods_ctx, wbc, t=x_ctx.shape[1])
    a4 = a.reshape(SSM_NQ, SSM_QG, 2, 2, LANES)[..., :SSM_STATE]
    a_rows = jnp.repeat(a4.transpose(2, 0, 3, 1, 4).reshape(2, -1), bsz, axis=0)
    a_rows = a_rows.reshape(SSM_ROWS, SSM_NQ * SSM_VR, LANES).transpose(1, 0, 2)
    s_in = _s5_scan(s_ctx, s_lat, a_rows)
    return _s5_out(x_lat, mods_lat, s_in, bd, wcc, d_skip.reshape(1, -1), t=t_lat)


def _gmlp_kernel(x_ref, mod_ref, win_ref, bin_ref, lg_ref, lb_ref, ws_ref, bs_ref, wout_ref, g_ref, b_ref, o_ref,
                 v_ref, gated_ref):
    m = mod_ref[0]
    x = x_ref[0]
    t = x.shape[0]
    hw = GMLP_HEAD_DIM
    h = (x * (1.0 + m[1:2]) + m[0:1]).astype(BF16)

    def z_cols(c0):
        return jax.nn.gelu(_dot(h, win_ref[:, c0:c0 + hw]) + bin_ref[:, c0:c0 + hw])

    total = jnp.zeros((t, 1), F32)
    for hd in range(GMLP_HEADS):
        zc = z_cols(GMLP_HALF + hd * hw)
        v_ref[:, hd * hw:(hd + 1) * hw] = zc
        total = total + jnp.sum(zc, axis=-1, keepdims=True)
    mu = total / GMLP_HALF
    u_next = z_cols(0)
    sq = jnp.zeros((t, 1), F32)
    for hd in range(GMLP_HEADS):
        vc = v_ref[:, hd * hw:(hd + 1) * hw] - mu
        sq = sq + jnp.sum(vc * vc, axis=-1, keepdims=True)
    rstd = lax.rsqrt(sq / GMLP_HALF + LN_EPS)
    bs = bs_ref[...]
    for hd in range(GMLP_HEADS):
        cols = slice(hd * hw, (hd + 1) * hw)
        u = u_next
        if hd + 1 < GMLP_HEADS:
            u_next = z_cols((hd + 1) * hw)
        vn = ((v_ref[:, cols] - mu) * rstd * lg_ref[:, cols] + lb_ref[:, cols]).astype(BF16)
        for c in range(t // GMLP_CHUNK):
            rows = slice(c * GMLP_CHUNK, (c + 1) * GMLP_CHUNK)
            gate = _dot(ws_ref[hd], vn[rows]) + bs[:, hd:hd + 1]
            gated_ref[rows, cols] = (u[rows] * gate).astype(BF16)
    chunks = _row_chunks(t)
    ys = [_dot(gated_ref[rs, :], wout_ref[...]) for rs in chunks]
    for rs, y in zip(chunks, ys):
        o_ref[0, rs, :] = _post_norm(x_ref[0, rs, :], y, m[2:3], g_ref[...], b_ref[...])


def _gmlp(x, mods, w_in, b_in, lg, lb, w_s, b_s_t, w_out, ln_g, ln_b, *, t):
    bsz, n, d = x.shape
    tile = pl.BlockSpec((1, t, d), lambda b, i: (b, i, 0))
    const2 = lambda a: _const_spec(a.shape)
    return pl.pallas_call(
        _gmlp_kernel,
        out_shape=jax.ShapeDtypeStruct((bsz, n, d), F32),
        grid=(bsz, n // t),
        in_specs=[tile, pl.BlockSpec((1, N_MODS, d), lambda b, i: (b, 0, 0)),
                  const2(w_in), const2(b_in), const2(lg), const2(lb), const2(w_s), const2(b_s_t), const2(w_out),
                  const2(ln_g), const2(ln_b)],
        out_specs=tile,
        scratch_shapes=[pltpu.VMEM((t, GMLP_HALF), F32), pltpu.VMEM((t, GMLP_HALF), BF16)],
        compiler_params=_cparams("arbitrary", "arbitrary"),
        name="gmlp_mixer",
    )(x, mods, w_in, b_in, lg, lb, w_s, b_s_t, w_out, ln_g, ln_b)


def _tile(n, pref):
    return pref if n % pref == 0 else n


def _layer(layer, x_lat, x_ctx, mods, p):
    bsz, _, d = x_lat.shape
    lctx = x_ctx.shape[1]
    row = lambda v: v.reshape(1, -1)
    t_lat, t_ctx = 1024, _tile(lctx, 256)
    kind = MIXERS[layer % len(MIXERS)]
    j = layer // len(MIXERS)
    ctx_out = any(MIXERS[m % len(MIXERS)] in CTX_READING_MIXERS for m in range(layer + 1, DEPTH))
    m_lat = mods[layer, :bsz]
    m_ctx = jnp.broadcast_to(mods[layer, bsz:bsz + 1], (bsz, N_MODS, d))
    g1, b1 = row(p["ln1_g"][layer]), row(p["ln1_b"][layer])

    def ffn(xs, ms, t):
        return _ffn(xs, ms, p["ffn_w_up_bf16"], p["ffn_conv_w"][layer], row(p["ffn_conv_b"][layer]),
                    p["ffn_w_down_bf16"], row(p["ln2_g"][layer]), row(p["ln2_b"][layer]),
                    layer=layer, t=t, cc=FFN_CHUNK)

    if kind == "pool":
        args = (p["pool_w"][j].astype(BF16), row(p["pool_b"][j]), row(p["pool_scale"][j]), g1, b1)
        x_lat = _pool(x_lat, m_lat, *args, t=t_lat)
        if ctx_out:
            x_ctx = _pool(x_ctx, m_ctx, *args, t=t_ctx)
    elif kind == "attn":
        wqkv = p["attn_w_qkv"][j].astype(BF16)
        wo = p["attn_w_o"][j].astype(BF16)
        sink = p["attn_sink"][j].astype(F32)
        q, kd, vd = _qkv(x_lat, m_lat, wqkv, t=t_lat, rope=True)
        qc, kdc, vdc = _qkv(x_ctx, m_ctx, wqkv, t=t_ctx, rope=False)
        o_lat = _attn(sink, q, kd, vd, kdc, vdc, local=True)
        x_lat = _proj_norm(o_lat, x_lat, m_lat, wo, g1, b1, t=t_lat)
        if ctx_out:
            o_ctx = _attn(sink, qc, None, None, kdc, vdc, local=False)
            x_ctx = _proj_norm(o_ctx, x_ctx, m_ctx, wo, g1, b1, t=t_ctx)
    elif kind == "ssm":
        assert not ctx_out
        y = _s5_mixer(x_lat, x_ctx, m_lat, m_ctx, p["ssm_lambda_re"][j], p["ssm_lambda_im"][j], p["ssm_log_dt"][j],
                       p["ssm_b_re"][j], p["ssm_b_im"][j], p["ssm_c_re"][j], p["ssm_c_im"][j], p["ssm_d"][j])
        x_lat = _glu_norm(y, x_lat, m_lat, p["ssm_w_glu_a"][j].astype(BF16), p["ssm_w_glu_b"][j].astype(BF16),
                          g1, b1, t=t_lat)
    else:
        assert not ctx_out
        x_lat = _gmlp(x_lat, m_lat, p["gmlp_w_in"][j].astype(BF16), row(p["gmlp_b_in"][j]), row(p["gmlp_ln_g"][j]),
                      row(p["gmlp_ln_b"][j]), p["gmlp_w_s"][j].astype(BF16), p["gmlp_b_s"][j].T,
                      p["gmlp_w_out"][j].astype(BF16), g1, b1, t=t_lat)
    x_lat = ffn(x_lat, m_lat, _tile(x_lat.shape[1], 1024))
    if ctx_out:
        x_ctx = ffn(x_ctx, m_ctx, t_ctx)
    return x_lat, x_ctx


def _mods(c, c_ctx, ada_w, ada_b):
    bsz, d = c.shape
    cond = jnp.concatenate([c, c_ctx[None, :], jnp.zeros((8 - bsz - 1, d), F32)], axis=0)
    return _ada(cond, ada_w, ada_b).reshape(DEPTH, 8, N_MODS, d)


def kernel(x, c, ctx, c_ctx, ada_w, ada_b, ln1_g, ln1_b, ln2_g, ln2_b, ffn_w_up, ffn_conv_w, ffn_conv_b, ffn_w_down, pool_w, pool_b, pool_scale, attn_w_qkv, attn_w_o, attn_sink, ssm_lambda_re, ssm_lambda_im, ssm_log_dt, ssm_b_re, ssm_b_im, ssm_c_re, ssm_c_im, ssm_d, ssm_w_glu_a, ssm_w_glu_b, gmlp_w_in, gmlp_b_in, gmlp_ln_g, gmlp_ln_b, gmlp_w_s, gmlp_b_s, gmlp_w_out):
    bsz, n, d = x.shape
    assert d == D_MODEL and bsz < 8 and n % 512 == 0 and ctx.shape[1] % ATTN_BLOCK == 0
    p = dict(ln1_g=ln1_g, ln1_b=ln1_b, ln2_g=ln2_g, ln2_b=ln2_b, ffn_w_up=ffn_w_up, ffn_conv_w=ffn_conv_w,
             ffn_conv_b=ffn_conv_b, ffn_w_down=ffn_w_down, pool_w=pool_w, pool_b=pool_b, pool_scale=pool_scale,
             attn_w_qkv=attn_w_qkv, attn_w_o=attn_w_o, attn_sink=attn_sink, ssm_lambda_re=ssm_lambda_re,
             ssm_lambda_im=ssm_lambda_im, ssm_log_dt=ssm_log_dt, ssm_b_re=ssm_b_re, ssm_b_im=ssm_b_im,
             ssm_c_re=ssm_c_re, ssm_c_im=ssm_c_im, ssm_d=ssm_d, ssm_w_glu_a=ssm_w_glu_a, ssm_w_glu_b=ssm_w_glu_b,
             gmlp_w_in=gmlp_w_in, gmlp_b_in=gmlp_b_in, gmlp_ln_g=gmlp_ln_g, gmlp_ln_b=gmlp_ln_b, gmlp_w_s=gmlp_w_s,
             gmlp_b_s=gmlp_b_s, gmlp_w_out=gmlp_w_out)
    p["ffn_w_up_bf16"] = ffn_w_up.astype(BF16)
    p["ffn_w_down_bf16"] = ffn_w_down.astype(BF16)
    mods = _mods(c, c_ctx, ada_w, ada_b)
    x_lat, x_ctx = x, ctx
    for layer in range(DEPTH):
        x_lat, x_ctx = _layer(layer, x_lat, x_ctx, mods, p)
    return x_lat
```

```python
import functools
import math

import jax
import jax.numpy as jnp
from jax import lax
from jax.experimental import pallas as pl
from jax.experimental.pallas import tpu as pltpu

F32 = jnp.float32
BF16 = jnp.bfloat16

D_MODEL = 1024
DEPTH = 4
MIXERS = ("pool", "attn", "ssm", "gmlp")
CTX_READING_MIXERS = ("attn", "ssm")
GRID_W = 64
N_MODS = 6
DEEPNORM_ALPHA = (2.0 * DEPTH) ** 0.25
LN_EPS = 1e-5

POOL_WINDOWS = (2, 4, 8, 16)
POOL_GROUP = D_MODEL // len(POOL_WINDOWS)

HEAD_DIM = 64
N_Q_HEADS = D_MODEL // HEAD_DIM
N_KV_HEADS = N_Q_HEADS // 4
Q_WIDTH = N_Q_HEADS * HEAD_DIM
KV_WIDTH = N_KV_HEADS * HEAD_DIM
WINDOW = 128
ATTN_BLOCK = 128
ATTN_QBLOCKS = 4
ROPE_BASE = 10000.0
NEG_INF = -1e30
LOG2E = math.log2(math.e)

SSM_GROUP = 16
SSM_N_GROUPS = D_MODEL // SSM_GROUP
SSM_STATE = 64
SSM_CHUNK = 16

GMLP_CHUNK = 128
GMLP_HALF = 2 * D_MODEL
GMLP_HEADS = 8
GMLP_HEAD_DIM = GMLP_HALF // GMLP_HEADS

FFN_HIDDEN = 2816
FFN_CHUNK = 256
OUT_ROWS = 256

LANES = 128
HALO = 16
VMEM_LIMIT = 56 * 1024 * 1024


def _cparams(*sem):
    return pltpu.CompilerParams(dimension_semantics=sem, vmem_limit_bytes=VMEM_LIMIT)


def _const_spec(shape):
    nd = len(shape)
    return pl.BlockSpec(shape, lambda *_: (0,) * nd, pipeline_mode=pl.Buffered(1))


def _post_norm(x, y, gate, g, b):
    z = DEEPNORM_ALPHA * x + gate * y
    mu = jnp.mean(z, axis=-1, keepdims=True)
    zc = z - mu
    var = jnp.mean(zc * zc, axis=-1, keepdims=True)
    return zc * lax.rsqrt(var + LN_EPS) * g + b


def _row_chunks(t):
    return [slice(r, min(r + OUT_ROWS, t)) for r in range(0, t, OUT_ROWS)]


def _dot(a, b):
    return jnp.dot(a, b, preferred_element_type=F32)


def _dot_nt(a, b):
    return lax.dot_general(a, b, (((1,), (1,)), ((), ())), preferred_element_type=F32)


def _ada_kernel(c_ref, w_ref, b_ref, o_ref):
    c = c_ref[...]
    s = (c * jax.nn.sigmoid(c)).astype(BF16)
    o_ref[0] = _dot(s, w_ref[0].astype(BF16)) + b_ref[0]


def _ada(cond, ada_w, ada_b):
    depth, d, n = ada_w.shape
    rows = cond.shape[0]
    tn = 1536
    return pl.pallas_call(
        _ada_kernel,
        out_shape=jax.ShapeDtypeStruct((depth, rows, n), F32),
        grid=(depth, n // tn),
        in_specs=[pl.BlockSpec((rows, d), lambda l, j: (0, 0)),
                  pl.BlockSpec((1, d, tn), lambda l, j: (l, 0, j)),
                  pl.BlockSpec((1, 1, tn), lambda l, j: (l, 0, j))],
        out_specs=pl.BlockSpec((1, rows, tn), lambda l, j: (l, 0, j)),
        compiler_params=_cparams("arbitrary", "arbitrary"),
        name="ada",
    )(cond, ada_w, ada_b.reshape(depth, 1, n))


def _halo_specs(t, n):
    per = t // HALO
    last = n // HALO - 1
    prev = pl.BlockSpec((1, HALO, D_MODEL), lambda b, i, *_: (b, jnp.maximum(i * per - 1, 0), 0))
    nxt = pl.BlockSpec((1, HALO, D_MODEL), lambda b, i, *_: (b, jnp.minimum((i + 1) * per, last), 0))
    return prev, nxt


def _ffn_kernel(xp_ref, x_ref, xn_ref, mod_ref, wu_ref, cw_ref, cb_ref, wd_ref, g_ref, b_ref, o_ref, act_ref,
                *, t, nt, f, cc):
    i = pl.program_id(1)
    rows = t + 2 * HALO
    m = mod_ref[0]
    sh, sc = m[3:4], 1.0 + m[4:5]
    keep_p = jnp.where(i > 0, 1.0, 0.0)
    keep_n = jnp.where(i < nt - 1, 1.0, 0.0)
    x = x_ref[0]
    h = jnp.concatenate([((xp_ref[0] * sc + sh) * keep_p).astype(BF16), (x * sc + sh).astype(BF16),
                         ((xn_ref[0] * sc + sh) * keep_n).astype(BF16)], axis=0)

    def conv(off):
        u = _dot(h, wu_ref[:, off:off + cc])
        cw = cw_ref[:, off:off + cc]
        a = cb_ref[:, off:off + cc] + pltpu.roll(u, 1, 0) * cw[0:1]
        a = a + u * cw[1:2]
        a = a + pltpu.roll(u, rows - 1, 0) * cw[2:3]
        return a[HALO:HALO + t]

    for c in range(f // cc):
        val = conv(c * cc)
        gate = conv(f + c * cc)
        act_ref[:, c * cc:(c + 1) * cc] = (val * (gate * jax.nn.sigmoid(gate))).astype(BF16)
    chunks = _row_chunks(t)
    ys = [_dot(act_ref[rs, :], wd_ref[...]) for rs in chunks]
    for rs, y in zip(chunks, ys):
        o_ref[0, rs, :] = _post_norm(x_ref[0, rs, :], y, m[5:6], g_ref[...], b_ref[...])


def _layer_spec(shape, layer):
    nd = len(shape) - 1
    return pl.BlockSpec((None,) + tuple(shape[1:]), lambda *_: (layer,) + (0,) * nd, pipeline_mode=pl.Buffered(1))


def _ffn(x, mods, w_up, conv_w, conv_b, w_down, ln_g, ln_b, *, layer, t, cc):
    bsz, n, d = x.shape
    f = w_down.shape[1]
    nt = n // t
    prev, nxt = _halo_specs(t, n)
    kern = functools.partial(_ffn_kernel, t=t, nt=nt, f=f, cc=cc)
    return pl.pallas_call(
        kern,
        out_shape=jax.ShapeDtypeStruct((bsz, n, d), F32),
        grid=(bsz, nt),
        in_specs=[prev,
                  pl.BlockSpec((1, t, d), lambda b, i: (b, i, 0)),
                  nxt,
                  pl.BlockSpec((1, N_MODS, d), lambda b, i: (b, 0, 0)),
                  _layer_spec(w_up.shape, layer), _const_spec(conv_w.shape), _const_spec(conv_b.shape),
                  _layer_spec(w_down.shape, layer), _const_spec(ln_g.shape), _const_spec(ln_b.shape)],
        out_specs=pl.BlockSpec((1, t, d), lambda b, i: (b, i, 0)),
        scratch_shapes=[pltpu.VMEM((t, f), BF16)],
        compiler_params=_cparams("arbitrary", "arbitrary"),
        name="conv_ffn",
    )(x, x, x, mods, w_up, conv_w, conv_b, w_down, ln_g, ln_b)


def _pool_kernel(xp_ref, x_ref, xn_ref, mod_ref, w_ref, pb_ref, ps_ref, g_ref, b_ref, o_ref, *, t, nt, n):
    i = pl.program_id(1)
    rows = t + 2 * HALO
    m = mod_ref[0]
    sh, sc = m[0:1], 1.0 + m[1:2]
    keep_p = jnp.where(i > 0, 1.0, 0.0)
    keep_n = jnp.where(i < nt - 1, 1.0, 0.0)
    x = x_ref[0]
    h = jnp.concatenate([(xp_ref[0] * sc + sh) * keep_p, x * sc + sh, (xn_ref[0] * sc + sh) * keep_n], axis=0)
    pos = i * t + lax.broadcasted_iota(jnp.int32, (t, 1), 0)
    outs = []
    for gi, win in enumerate(POOL_WINDOWS):
        hg = h[:, gi * POOL_GROUP:(gi + 1) * POOL_GROUP]
        s = hg + pltpu.roll(hg, 1, 0)
        half = 1
        while 2 * half < win:
            s = pltpu.roll(s, half, 0) + pltpu.roll(s, rows - half, 0)
            half *= 2
        lo = jnp.maximum(pos - win // 2, 0)
        hi = jnp.minimum(pos - win // 2 + win, n)
        mean = s[HALO:HALO + t] / (hi - lo).astype(F32)
        mixed = (mean - hg[HALO:HALO + t]).astype(BF16)
        outs.append(_dot(mixed, w_ref[gi]))
    y = (jnp.concatenate(outs, axis=1) + pb_ref[...]) * ps_ref[...]
    o_ref[0] = _post_norm(x, y, m[2:3], g_ref[...], b_ref[...])


def _pool(x, mods, w, pb, ps, ln_g, ln_b, *, t):
    bsz, n, d = x.shape
    nt = n // t
    prev, nxt = _halo_specs(t, n)
    kern = functools.partial(_pool_kernel, t=t, nt=nt, n=n)
    vec = pl.BlockSpec((1, d), lambda b, i: (0, 0))
    return pl.pallas_call(
        kern,
        out_shape=jax.ShapeDtypeStruct((bsz, n, d), F32),
        grid=(bsz, nt),
        in_specs=[prev, pl.BlockSpec((1, t, d), lambda b, i: (b, i, 0)), nxt,
                  pl.BlockSpec((1, N_MODS, d), lambda b, i: (b, 0, 0)),
                  pl.BlockSpec(w.shape, lambda b, i: (0, 0, 0)),
                  vec, vec, vec, vec],
        out_specs=pl.BlockSpec((1, t, d), lambda b, i: (b, i, 0)),
        compiler_params=_cparams("arbitrary", "arbitrary"),
        name="pool_mixer",
    )(x, x, x, mods, w, pb, ps, ln_g, ln_b)


def _rope_tables(n):
    tpos = jnp.arange(n)
    half = HEAD_DIM // 4
    freqs = ROPE_BASE ** (-jnp.arange(half, dtype=F32) / half)
    ang_r = (tpos // GRID_W).astype(F32)[:, None] * freqs[None, :]
    ang_c = (tpos % GRID_W).astype(F32)[:, None] * freqs[None, :]
    zero = jnp.zeros_like(ang_r)
    cr, sr, cc, sc = jnp.cos(ang_r), jnp.sin(ang_r), jnp.cos(ang_c), jnp.sin(ang_c)
    cos = jnp.tile(jnp.concatenate([cr, cr, cc, cc], axis=1), (1, 2))
    sin_first = jnp.tile(jnp.concatenate([-sr, zero, -sc, zero], axis=1), (1, 2))
    sin_second = jnp.tile(jnp.concatenate([zero, sr, zero, sc], axis=1), (1, 2))
    return cos, sin_first, sin_second


def _dup_heads(chunk, lo):
    sw = pltpu.roll(chunk, HEAD_DIM, 1)
    return jnp.where(lo, chunk, sw), jnp.where(lo, sw, chunk)


def _qkv_kernel(*refs, rope):
    if rope:
        x_ref, mod_ref, w_ref, cos_ref, sa_ref, sb_ref, q_ref, kd_ref, vt_ref = refs
    else:
        x_ref, mod_ref, w_ref, q_ref, kd_ref, vt_ref = refs
    m = mod_ref[0]
    h = (x_ref[0] * (1.0 + m[1:2]) + m[0:1]).astype(BF16)
    qkv = _dot(h, w_ref[...])
    t = qkv.shape[0]
    lo = lax.broadcasted_iota(jnp.int32, (t, LANES), 1) < HEAD_DIM

    def proj(col):
        return qkv[:, col:col + LANES]

    scale = HEAD_DIM ** -0.5 * LOG2E

    def rot(v):
        if not rope:
            return v
        quarter = HEAD_DIM // 4
        return (v * cos_ref[...] + pltpu.roll(v, LANES - quarter, 1) * sa_ref[...]
                + pltpu.roll(v, quarter, 1) * sb_ref[...])

    for c in range(Q_WIDTH // LANES):
        q_ref[0, :, c * LANES:(c + 1) * LANES] = (rot(proj(c * LANES)) * scale).astype(BF16)
    for c in range(KV_WIDTH // LANES):
        k0, k1 = _dup_heads(rot(proj(Q_WIDTH + c * LANES)), lo)
        kd_ref[0, :, (2 * c) * LANES:(2 * c + 1) * LANES] = k0.astype(BF16)
        kd_ref[0, :, (2 * c + 1) * LANES:(2 * c + 2) * LANES] = k1.astype(BF16)
        v = proj(Q_WIDTH + KV_WIDTH + c * LANES)
        vt_ref[0, c * LANES:(c + 1) * LANES, :] = jnp.transpose(v).astype(BF16)


def _qkv(x, mods, w_qkv, *, t, rope):
    bsz, n, d = x.shape
    kdw = N_KV_HEADS * LANES
    ins = [x, mods, w_qkv]
    specs = [pl.BlockSpec((1, t, d), lambda b, i: (b, i, 0)),
             pl.BlockSpec((1, N_MODS, d), lambda b, i: (b, 0, 0)),
             pl.BlockSpec(w_qkv.shape, lambda b, i: (0, 0))]
    if rope:
        ins += list(_rope_tables(n))
        specs += [pl.BlockSpec((t, LANES), lambda b, i: (i, 0))] * 3
    return pl.pallas_call(
        functools.partial(_qkv_kernel, rope=rope),
        out_shape=(jax.ShapeDtypeStruct((bsz, n, Q_WIDTH), BF16),
                   jax.ShapeDtypeStruct((bsz, n, kdw), BF16),
                   jax.ShapeDtypeStruct((bsz, KV_WIDTH, n), BF16)),
        grid=(bsz, n // t),
        in_specs=specs,
        out_specs=(pl.BlockSpec((1, t, Q_WIDTH), lambda b, i: (b, i, 0)),
                   pl.BlockSpec((1, t, kdw), lambda b, i: (b, i, 0)),
                   pl.BlockSpec((1, KV_WIDTH, t), lambda b, i: (b, 0, i))),
        compiler_params=_cparams("arbitrary", "arbitrary"),
        name="qkv_rope" if rope else "qkv_ctx",
    )(*ins)


def _attn_kernel(sink_ref, q_ref, *refs, nsteps, qb, local):
    blk = ATTN_BLOCK
    if local:
        kp_ref, kc_ref, kn_ref, kx_ref, vp_ref, vc_ref, vn_ref, vx_ref, o_ref = refs
        k_band = [kp_ref[0]] + [kc_ref[0, j * blk:(j + 1) * blk, :] for j in range(qb)] + [kn_ref[0]]
        v_band = [vp_ref[0]] + [vc_ref[0, :, j * blk:(j + 1) * blk] for j in range(qb)] + [vn_ref[0]]
    else:
        kx_ref, vx_ref, o_ref = refs
    group = N_Q_HEADS // N_KV_HEADS
    cols = group * blk
    step = pl.program_id(1)
    if local:
        kj = lax.broadcasted_iota(jnp.int32, (blk, cols), 0)
        qi = lax.broadcasted_iota(jnp.int32, (blk, cols), 1) % blk
        behind, ahead = kj >= qi, kj <= qi
    lo = lax.broadcasted_iota(jnp.int32, (blk, LANES), 1) < HEAD_DIM
    head_of_col = lax.broadcasted_iota(jnp.int32, (1, cols), 1) // blk
    units = [(j, hk) for j in range(qb) for hk in range(N_KV_HEADS)]

    scores = []
    for j, hk in units:
        ks = (k_band[j:j + 3] if local else []) + [kx_ref[0]]
        kx = jnp.concatenate([k[:, hk * LANES:(hk + 1) * LANES] for k in ks], axis=0)
        parts = []
        for c in range(group // 2):
            qc = q_ref[0, j * blk:(j + 1) * blk, (hk * group // 2 + c) * LANES:(hk * group // 2 + c + 1) * LANES]
            zero = jnp.zeros_like(qc)
            parts += [jnp.where(lo, qc, zero), jnp.where(lo, zero, qc)]
        q4 = jnp.concatenate(parts, axis=0)
        scores.append(_dot_nt(kx, q4))
    probs = []
    for (j, hk), s in zip(units, scores):
        if local:
            valid_prev = behind & (step > 0) if j == 0 else behind
            valid_next = ahead & (step < nsteps - 1) if j == qb - 1 else ahead
            s = jnp.concatenate([jnp.where(valid_prev, s[:blk], NEG_INF), s[blk:2 * blk],
                                 jnp.where(valid_next, s[2 * blk:3 * blk], NEG_INF), s[3 * blk:]], axis=0)
        sink = jnp.zeros((1, cols), F32)
        for g in range(group):
            sink = jnp.where(head_of_col == g, sink_ref[hk * group + g] * LOG2E, sink)
        mx = jnp.maximum(jnp.max(s, axis=0, keepdims=True), sink)
        p = jnp.exp2(s - mx)
        den = jnp.sum(p, axis=0, keepdims=True) + jnp.exp2(sink - mx)
        probs.append((p.astype(BF16), den))
    for (j, hk), (p, den) in zip(units, probs):
        vs = (v_band[j:j + 3] if local else []) + [vx_ref[0]]
        vt = jnp.concatenate([v[hk * HEAD_DIM:(hk + 1) * HEAD_DIM, :] for v in vs], axis=1)
        ot = _dot(vt, p) / den
        for c in range(group // 2):
            pair = jnp.concatenate([ot[:, (2 * c) * blk:(2 * c + 1) * blk],
                                    ot[:, (2 * c + 1) * blk:(2 * c + 2) * blk]], axis=0)
            o_ref[0, j * blk:(j + 1) * blk, (hk * group // 2 + c) * LANES:(hk * group // 2 + c + 1) * LANES] = (
                jnp.transpose(pair).astype(BF16))


def _attn(sink, q, kd, vt, kd_ctx, vt_ctx, *, local, qb):
    bsz, n, _ = q.shape
    blk = ATTN_BLOCK
    assert WINDOW == blk, "the band is exactly the previous, own and next key block"
    nb = n // blk
    qb = math.gcd(qb, nb)
    nsteps = nb // qb
    lctx = kd_ctx.shape[1]
    kdw = kd_ctx.shape[2]
    smem = pl.BlockSpec(memory_space=pltpu.SMEM)
    qspec = pl.BlockSpec((1, qb * blk, Q_WIDTH), lambda b, i: (b, i, 0))
    kctx_spec = pl.BlockSpec((1, lctx, kdw), lambda b, i: (b, 0, 0))
    vctx_spec = pl.BlockSpec((1, KV_WIDTH, lctx), lambda b, i: (b, 0, 0))
    if local:
        prev, nxt = (lambda i: jnp.maximum(i * qb - 1, 0)), (lambda i: jnp.minimum((i + 1) * qb, nb - 1))
        kband = [pl.BlockSpec((1, blk, kdw), lambda b, i: (b, prev(i), 0)),
                 pl.BlockSpec((1, qb * blk, kdw), lambda b, i: (b, i, 0)),
                 pl.BlockSpec((1, blk, kdw), lambda b, i: (b, nxt(i), 0))]
        vband = [pl.BlockSpec((1, KV_WIDTH, blk), lambda b, i: (b, 0, prev(i))),
                 pl.BlockSpec((1, KV_WIDTH, qb * blk), lambda b, i: (b, 0, i)),
                 pl.BlockSpec((1, KV_WIDTH, blk), lambda b, i: (b, 0, nxt(i)))]
        specs = [smem, qspec] + kband + [kctx_spec] + vband + [vctx_spec]
        args = (sink, q, kd, kd, kd, kd_ctx, vt, vt, vt, vt_ctx)
    else:
        specs = [smem, qspec, kctx_spec, vctx_spec]
        args = (sink, q, kd_ctx, vt_ctx)
    return pl.pallas_call(
        functools.partial(_attn_kernel, nsteps=nsteps, qb=qb, local=local),
        out_shape=jax.ShapeDtypeStruct((bsz, n, Q_WIDTH), BF16),
        grid=(bsz, nsteps),
        in_specs=specs,
        out_specs=pl.BlockSpec((1, qb * blk, Q_WIDTH), lambda b, i: (b, i, 0)),
        compiler_params=_cparams("arbitrary", "arbitrary"),
        name="banded_attn" if local else "ctx_attn",
    )(*args)


def _proj_norm_kernel(a_ref, x_ref, mod_ref, w_ref, g_ref, b_ref, o_ref):
    chunks = _row_chunks(x_ref.shape[1])
    ys = [_dot(a_ref[0, rs, :], w_ref[...]) for rs in chunks]
    for rs, y in zip(chunks, ys):
        o_ref[0, rs, :] = _post_norm(x_ref[0, rs, :], y, mod_ref[0][2:3], g_ref[...], b_ref[...])


def _proj_norm(a, x, mods, w, ln_g, ln_b, *, t):
    bsz, n, d = x.shape
    ka = a.shape[2]
    vec = pl.BlockSpec((1, d), lambda b, i: (0, 0))
    return pl.pallas_call(
        _proj_norm_kernel,
        out_shape=jax.ShapeDtypeStruct((bsz, n, d), F32),
        grid=(bsz, n // t),
        in_specs=[pl.BlockSpec((1, t, ka), lambda b, i: (b, i, 0)),
                  pl.BlockSpec((1, t, d), lambda b, i: (b, i, 0)),
                  pl.BlockSpec((1, N_MODS, d), lambda b, i: (b, 0, 0)),
                  pl.BlockSpec(w.shape, lambda b, i: (0, 0)),
                  vec, vec],
        out_specs=pl.BlockSpec((1, t, d), lambda b, i: (b, i, 0)),
        compiler_params=_cparams("arbitrary", "arbitrary"),
        name="attn_out_norm",
    )(a, x, mods, w, ln_g, ln_b)


def _glu_norm_kernel(y_ref, x_ref, mod_ref, wa_ref, wb_ref, g_ref, b_ref, o_ref):
    gl = jax.nn.gelu(y_ref[0]).astype(BF16)
    out = _dot(gl, wa_ref[...]) * jax.nn.sigmoid(_dot(gl, wb_ref[...]))
    o_ref[0] = _post_norm(x_ref[0], out, mod_ref[0][2:3], g_ref[...], b_ref[...])


def _glu_norm(y, x, mods, wa, wb, ln_g, ln_b, *, t):
    bsz, n, d = x.shape
    tile = pl.BlockSpec((1, t, d), lambda b, i: (b, i, 0))
    vec = pl.BlockSpec((1, d), lambda b, i: (0, 0))
    wspec = pl.BlockSpec((d, d), lambda b, i: (0, 0))
    return pl.pallas_call(
        _glu_norm_kernel,
        out_shape=jax.ShapeDtypeStruct((bsz, n, d), F32),
        grid=(bsz, n // t),
        in_specs=[tile, tile, pl.BlockSpec((1, N_MODS, d), lambda b, i: (b, 0, 0)), wspec, wspec, vec, vec],
        out_specs=tile,
        compiler_params=_cparams("arbitrary", "arbitrary"),
        name="ssm_glu_norm",
    )(y, x, mods, wa, wb, ln_g, ln_b)


SSM_QG = LANES // SSM_GROUP
SSM_NQ = D_MODEL // LANES
SSM_ROWS = 8
SSM_W = SSM_CHUNK * LANES
SSM_HALF = SSM_QG * SSM_STATE
SSM_TAP = SSM_CHUNK * SSM_GROUP
SSM_VR = 2 * SSM_HALF // LANES


def _s5_prep_kernel(lr_ref, li_ref, ldt_ref, btr_ref, bti_ref, cr_ref, ci_ref, wb_ref, wc_ref, bd_ref, a_ref):
    c = SSM_CHUNK
    q = pl.program_id(0)
    lag = lax.broadcasted_iota(jnp.int32, (c + 1, LANES), 0).astype(F32)
    lane = lax.broadcasted_iota(jnp.int32, (1, LANES), 1)
    taps = [[[] for _ in range(c)] for _ in range(2)]
    for g8 in range(SSM_QG):
        own = jnp.where((lane < SSM_STATE) == (g8 % 2 == 0), 1.0, 0.0)
        wb_cols, wc_cols, a_cols = [], [], []
        for d in range(2):
            lr, li = lr_ref[g8, d], li_ref[g8, d]
            dt = jnp.exp(jnp.full((1, LANES), ldt_ref[q * SSM_QG + g8, d], F32))
            mag = jnp.exp(lag * (lr * dt))
            ang = lag * (li * dt)
            pw_r, pw_i = mag * jnp.cos(ang), mag * jnp.sin(ang)
            lbr, lbi = pw_r[1:2], pw_i[1:2]
            den = lr * lr + li * li
            qr = ((lbr - 1.0) * lr + lbi * li) / den
            qi = (lbi * lr - (lbr - 1.0) * li) / den
            btr, bti = btr_ref[g8, d], bti_ref[g8, d]
            bbr = qr * btr - qi * bti
            bbi = qr * bti + qi * btr
            cr, ci = cr_ref[g8, d], ci_ref[g8, d]

            def cl(j):
                return cr * pw_r[j:j + 1] - ci * pw_i[j:j + 1], -(cr * pw_i[j:j + 1] + ci * pw_r[j:j + 1])

            def bl(j):
                return bbr * pw_r[j:j + 1] - bbi * pw_i[j:j + 1], bbr * pw_i[j:j + 1] + bbi * pw_r[j:j + 1]

            e = jnp.concatenate([jnp.concatenate(cl(j), axis=1) for j in range(c)], axis=0)
            pieces = [jnp.concatenate([bbr * own, bbi * own], axis=1)]
            if g8 > 0:
                pieces.insert(0, jnp.zeros((g8 * SSM_GROUP, 2 * LANES), F32))
            if g8 < SSM_QG - 1:
                pieces.append(jnp.zeros(((SSM_QG - 1 - g8) * SSM_GROUP, 2 * LANES), F32))
            kt = lax.dot_general(e, jnp.concatenate(pieces, axis=0), (((1,), (1,)), ((), ())),
                                 preferred_element_type=F32, precision=lax.Precision.HIGHEST)
            for j in range(c):
                taps[d][j].append(kt[j * SSM_GROUP:(j + 1) * SSM_GROUP])
            wbl = [bl(c - 1 - k) if d == 0 else bl(k) for k in range(c)]
            wb_cols += [jnp.concatenate([w[0] for w in wbl], axis=0) * own,
                        jnp.concatenate([w[1] for w in wbl], axis=0) * own]
            wcl = [cl(k + 1) if d == 0 else cl(c - k) for k in range(c)]
            wc_cols += [jnp.concatenate([w[0] for w in wcl], axis=0) * own,
                        jnp.concatenate([w[1] for w in wcl], axis=0) * own]
            a_cols += [pw_r[c:c + 1], pw_i[c:c + 1]]
        wb_ref[g8] = jnp.concatenate(wb_cols, axis=1).astype(BF16)
        wc_ref[g8] = jnp.concatenate(wc_cols, axis=1).astype(BF16)
        a_ref[g8] = jnp.concatenate(a_cols, axis=1)
    for d in range(2):
        for j in range(c):
            blk = jnp.concatenate(taps[d][j], axis=0)
            if d == 0 and j == 0:
                blk = blk + jnp.concatenate(taps[1][0], axis=0)
            bd_ref[0, d, j] = blk.astype(BF16)


def _s5_prep(lam_re, lam_im, log_dt, b_re, b_im, c_re, c_im):
    g, nq, qg = SSM_N_GROUPS, SSM_NQ, SSM_QG
    per_g = lambda a: jnp.swapaxes(a, 0, 1)
    dup = lambda a: jnp.concatenate([a, a], axis=-1)
    lam_spec = pl.BlockSpec((qg, 2, 1, LANES), lambda i: (i, 0, 0, 0))
    mat_spec = pl.BlockSpec((qg, 2, SSM_GROUP, LANES), lambda i: (i, 0, 0, 0))
    w_spec = pl.BlockSpec((qg, SSM_TAP, 4 * LANES), lambda i: (i, 0, 0))
    return pl.pallas_call(
        _s5_prep_kernel,
        out_shape=(jax.ShapeDtypeStruct((g, SSM_TAP, 4 * LANES), BF16),
                   jax.ShapeDtypeStruct((g, SSM_TAP, 4 * LANES), BF16),
                   jax.ShapeDtypeStruct((nq, 2, SSM_CHUNK, LANES, LANES), BF16),
                   jax.ShapeDtypeStruct((g, 1, 4 * LANES), F32)),
        grid=(nq,),
        in_specs=[lam_spec, lam_spec, pl.BlockSpec(memory_space=pltpu.SMEM), mat_spec, mat_spec, mat_spec, mat_spec],
        out_specs=(w_spec, w_spec, pl.BlockSpec((1, 2, SSM_CHUNK, LANES, LANES), lambda i: (i, 0, 0, 0, 0)),
                   pl.BlockSpec((qg, 1, 4 * LANES), lambda i: (i, 0, 0))),
        compiler_params=_cparams("arbitrary"),
        name="s5_prep",
    )(dup(per_g(lam_re))[:, :, None, :], dup(per_g(lam_im))[:, :, None, :], per_g(log_dt),
      dup(jnp.swapaxes(per_g(b_re), 2, 3)), dup(jnp.swapaxes(per_g(b_im), 2, 3)), dup(per_g(c_re)), dup(per_g(c_im)))


def _s5_expand(dst_ref, src_ref):
    dst_ref[...] = jnp.zeros_like(dst_ref)
    for g8 in range(SSM_QG):
        for k in range(SSM_CHUNK):
            for c4 in range(4):
                r0 = k * LANES + g8 * SSM_GROUP
                c0 = c4 * SSM_HALF + (g8 // 2) * LANES
                dst_ref[r0:r0 + SSM_GROUP, c0:c0 + LANES] = src_ref[g8, k * SSM_GROUP:(k + 1) * SSM_GROUP,
                                                                    c4 * LANES:(c4 + 1) * LANES]


def _s5_modulate(x_ref, mod_ref, u_ref, bsz):
    for b in range(bsz):
        m = mod_ref[b]
        u_ref[b] = x_ref[b] * (1.0 + m[1:2]) + m[0:1]


def _s5_gather(u_ref, bsz, nchunk):
    rows = [jnp.concatenate([u_ref[b, pl.ds(pos, nchunk, stride=SSM_CHUNK), :] for pos in range(SSM_CHUNK)],
                            axis=1).astype(BF16) for b in range(bsz)]
    return jnp.concatenate(rows, axis=0)


def _s5_states_kernel(x_ref, mod_ref, wbc_ref, o_ref, wb_ref, u_ref, *, bsz, nchunk):
    @pl.when(pl.program_id(1) == 0)
    def _():
        _s5_expand(wb_ref, wbc_ref)

    _s5_modulate(x_ref, mod_ref, u_ref, bsz)
    s = _dot(_s5_gather(u_ref, bsz, nchunk), wb_ref[...])
    for b in range(bsz):
        for v in range(SSM_VR):
            for d in range(2):
                o_ref[v, pl.ds(d * bsz + b, nchunk, stride=SSM_ROWS), :] = (
                    s[b * nchunk:(b + 1) * nchunk, (d * SSM_VR + v) * LANES:(d * SSM_VR + v + 1) * LANES])


def _s5_states(x, mods, wbc, *, t):
    bsz, n, _ = x.shape
    nchunk = t // SSM_CHUNK
    return pl.pallas_call(
        functools.partial(_s5_states_kernel, bsz=bsz, nchunk=nchunk),
        out_shape=jax.ShapeDtypeStruct((SSM_NQ * SSM_VR, n // SSM_CHUNK * SSM_ROWS, LANES), F32),
        grid=(SSM_NQ, n // t),
        in_specs=[pl.BlockSpec((bsz, t, LANES), lambda q, i: (0, i, q)),
                  pl.BlockSpec((bsz, N_MODS, LANES), lambda q, i: (0, 0, q)),
                  pl.BlockSpec((SSM_QG, SSM_TAP, 4 * LANES), lambda q, i: (q, 0, 0))],
        out_specs=pl.BlockSpec((SSM_VR, nchunk * SSM_ROWS, LANES), lambda q, i: (q, i, 0)),
        scratch_shapes=[pltpu.VMEM((SSM_W, 4 * SSM_HALF), BF16), pltpu.VMEM((bsz, t, LANES), F32)],
        compiler_params=_cparams("arbitrary", "arbitrary"),
        name="s5_chunk_states",
    )(x, mods, wbc)


def _s5_scan_kernel(xc_ref, xl_ref, a_ref, o_ref):
    h = SSM_VR // 2
    ar, ai = a_ref[:h], a_ref[h:]
    nc, nl = xc_ref.shape[1] // SSM_ROWS, xl_ref.shape[1] // SSM_ROWS
    fwd_rows = lax.broadcasted_iota(jnp.int32, (SSM_VR, SSM_ROWS, LANES), 1) < SSM_ROWS // 2

    assert nl % 2 == 0
    chunk = lambda i: pl.ds(pl.multiple_of(i * SSM_ROWS, SSM_ROWS), SSM_ROWS)

    def step(ref, n, k, s):
        x = jnp.where(fwd_rows, ref[:, chunk(k), :], ref[:, chunk(n - 1 - k), :])
        return ar * s[0] - ai * s[1] + x[:h], ar * s[1] + ai * s[0] + x[h:]

    def first_touch(k, s):
        full = jnp.concatenate(s, axis=0)
        o_ref[:, chunk(k), :] = full
        o_ref[:, chunk(nl - 1 - k), :] = full
        return step(xl_ref, nl, k, s)

    def second_touch(k, s):
        full = jnp.concatenate(s, axis=0)
        o_ref[:, chunk(k), :] = jnp.where(fwd_rows, full, o_ref[:, chunk(k), :])
        o_ref[:, chunk(nl - 1 - k), :] = jnp.where(fwd_rows, o_ref[:, chunk(nl - 1 - k), :], full)
        return step(xl_ref, nl, k, s)

    zero = (jnp.zeros((h, SSM_ROWS, LANES), F32), jnp.zeros((h, SSM_ROWS, LANES), F32))
    s = lax.fori_loop(0, nc, lambda k, s: step(xc_ref, nc, k, s), zero)
    s = lax.fori_loop(0, nl // 2, first_touch, s)
    lax.fori_loop(nl // 2, nl, second_touch, s)


def _s5_scan(s_ctx, s_lat, a_rows):
    blk = lambda rows: pl.BlockSpec((SSM_VR, rows, LANES), lambda q: (q, 0, 0))
    return pl.pallas_call(
        _s5_scan_kernel,
        out_shape=jax.ShapeDtypeStruct(s_lat.shape, F32),
        grid=(SSM_NQ,),
        in_specs=[blk(s_ctx.shape[1]), blk(s_lat.shape[1]), blk(SSM_ROWS)],
        out_specs=blk(s_lat.shape[1]),
        compiler_params=_cparams("arbitrary"),
        name="s5_chunk_scan",
    )(s_ctx, s_lat, a_rows)


def _s5_out_kernel(x_ref, mod_ref, s_ref, bd_ref, wcc_ref, d_ref, o_ref, toep_ref, wc_ref, u_ref, y_ref, *,
                   bsz, nchunk):
    c = SSM_CHUNK

    @pl.when(pl.program_id(1) == 0)
    def _():
        _s5_expand(wc_ref, wcc_ref)
        for t in range(c):
            for k in range(c):
                toep_ref[t * LANES:(t + 1) * LANES, k * LANES:(k + 1) * LANES] = (
                    bd_ref[0, 0, t - k] if t >= k else bd_ref[0, 1, k - t])

    _s5_modulate(x_ref, mod_ref, u_ref, bsz)
    lhs_s = jnp.concatenate(
        [jnp.concatenate([s_ref[v, pl.ds(d * bsz + b, nchunk, stride=SSM_ROWS), :]
                          for d in range(2) for v in range(SSM_VR)], axis=1).astype(BF16)
         for b in range(bsz)], axis=0)
    y = _dot_nt(_s5_gather(u_ref, bsz, nchunk), toep_ref[...]) + _dot_nt(lhs_s, wc_ref[...])
    for b in range(bsz):
        for t in range(c):
            y_ref[b, pl.ds(t, nchunk, stride=c), :] = y[b * nchunk:(b + 1) * nchunk, t * LANES:(t + 1) * LANES]
    for b in range(bsz):
        o_ref[b] = y_ref[b] + d_ref[...] * u_ref[b]


def _s5_out(x, mods, s_in, bd, wcc, d_row, *, t):
    bsz, n, d = x.shape
    nchunk = t // SSM_CHUNK
    tile = pl.BlockSpec((bsz, t, LANES), lambda q, i: (0, i, q))
    return pl.pallas_call(
        functools.partial(_s5_out_kernel, bsz=bsz, nchunk=nchunk),
        out_shape=jax.ShapeDtypeStruct((bsz, n, d), F32),
        grid=(SSM_NQ, n // t),
        in_specs=[tile,
                  pl.BlockSpec((bsz, N_MODS, LANES), lambda q, i: (0, 0, q)),
                  pl.BlockSpec((SSM_VR, nchunk * SSM_ROWS, LANES), lambda q, i: (q, i, 0)),
                  pl.BlockSpec((1, 2, SSM_CHUNK, LANES, LANES), lambda q, i: (q, 0, 0, 0, 0)),
                  pl.BlockSpec((SSM_QG, SSM_TAP, 4 * LANES), lambda q, i: (q, 0, 0)),
                  pl.BlockSpec((1, LANES), lambda q, i: (0, q))],
        out_specs=tile,
        scratch_shapes=[pltpu.VMEM((SSM_W, SSM_W), BF16), pltpu.VMEM((SSM_W, 4 * SSM_HALF), BF16),
                        pltpu.VMEM((bsz, t, LANES), F32), pltpu.VMEM((bsz, t, LANES), F32)],
        compiler_params=_cparams("arbitrary", "arbitrary"),
        name="s5_chunk_out",
    )(x, mods, s_in, bd, wcc, d_row)


def _s5_mixer(x_lat, x_ctx, mods_lat, mods_ctx, lam_re, lam_im, log_dt, b_re, b_im, c_re, c_im, d_skip):
    bsz, n, _ = x_lat.shape
    assert 2 * bsz == SSM_ROWS, "state rows are (direction, batch) on the 8 sublanes"
    wbc, wcc, bd, a = _s5_prep(lam_re, lam_im, log_dt, b_re, b_im, c_re, c_im)
    t_lat = _tile(n, 1024)
    s_lat = _s5_states(x_lat, mods_lat, wbc, t=t_lat)
    s_ctx = _s5_states(x_ctx, mods_ctx, wbc, t=x_ctx.shape[1])
    a4 = a.reshape(SSM_NQ, SSM_QG, 2, 2, LANES)[..., :SSM_STATE]
    a_rows = jnp.repeat(a4.transpose(2, 0, 3, 1, 4).reshape(2, -1), bsz, axis=0)
    a_rows = a_rows.reshape(SSM_ROWS, SSM_NQ * SSM_VR, LANES).transpose(1, 0, 2)
    s_in = _s5_scan(s_ctx, s_lat, a_rows)
    return _s5_out(x_lat, mods_lat, s_in, bd, wcc, d_skip.reshape(1, -1), t=t_lat)


def _gmlp_kernel(x_ref, mod_ref, win_ref, bin_ref, lg_ref, lb_ref, ws_ref, bs_ref, wout_ref, g_ref, b_ref, o_ref,
                 v_ref, gated_ref):
    m = mod_ref[0]
    x = x_ref[0]
    t = x.shape[0]
    hw = GMLP_HEAD_DIM
    h = (x * (1.0 + m[1:2]) + m[0:1]).astype(BF16)

    def z_cols(c0):
        return jax.nn.gelu(_dot(h, win_ref[:, c0:c0 + hw]) + bin_ref[:, c0:c0 + hw])

    total = jnp.zeros((t, 1), F32)
    for hd in range(GMLP_HEADS):
        zc = z_cols(GMLP_HALF + hd * hw)
        v_ref[:, hd * hw:(hd + 1) * hw] = zc
        total = total + jnp.sum(zc, axis=-1, keepdims=True)
    mu = total / GMLP_HALF
    u_next = z_cols(0)
    sq = jnp.zeros((t, 1), F32)
    for hd in range(GMLP_HEADS):
        vc = v_ref[:, hd * hw:(hd + 1) * hw] - mu
        sq = sq + jnp.sum(vc * vc, axis=-1, keepdims=True)
    rstd = lax.rsqrt(sq / GMLP_HALF + LN_EPS)
    bs = bs_ref[...]
    for hd in range(GMLP_HEADS):
        cols = slice(hd * hw, (hd + 1) * hw)
        u = u_next
        if hd + 1 < GMLP_HEADS:
            u_next = z_cols((hd + 1) * hw)
        vn = ((v_ref[:, cols] - mu) * rstd * lg_ref[:, cols] + lb_ref[:, cols]).astype(BF16)
        for c in range(t // GMLP_CHUNK):
            rows = slice(c * GMLP_CHUNK, (c + 1) * GMLP_CHUNK)
            gate = _dot(ws_ref[hd], vn[rows]) + bs[:, hd:hd + 1]
            gated_ref[rows, cols] = (u[rows] * gate).astype(BF16)
    chunks = _row_chunks(t)
    ys = [_dot(gated_ref[rs, :], wout_ref[...]) for rs in chunks]
    for rs, y in zip(chunks, ys):
        o_ref[0, rs, :] = _post_norm(x_ref[0, rs, :], y, m[2:3], g_ref[...], b_ref[...])


def _gmlp(x, mods, w_in, b_in, lg, lb, w_s, b_s_t, w_out, ln_g, ln_b, *, t):
    bsz, n, d = x.shape
    tile = pl.BlockSpec((1, t, d), lambda b, i: (b, i, 0))
    const2 = lambda a: _const_spec(a.shape)
    return pl.pallas_call(
        _gmlp_kernel,
        out_shape=jax.ShapeDtypeStruct((bsz, n, d), F32),
        grid=(bsz, n // t),
        in_specs=[tile, pl.BlockSpec((1, N_MODS, d), lambda b, i: (b, 0, 0)),
                  const2(w_in), const2(b_in), const2(lg), const2(lb), const2(w_s), const2(b_s_t), const2(w_out),
                  const2(ln_g), const2(ln_b)],
        out_specs=tile,
        scratch_shapes=[pltpu.VMEM((t, GMLP_HALF), F32), pltpu.VMEM((t, GMLP_HALF), BF16)],
        compiler_params=_cparams("arbitrary", "arbitrary"),
        name="gmlp_mixer",
    )(x, mods, w_in, b_in, lg, lb, w_s, b_s_t, w_out, ln_g, ln_b)


def _tile(n, pref):
    return pref if n % pref == 0 else n


def _layer(layer, x_lat, x_ctx, mods, p):
    bsz, _, d = x_lat.shape
    lctx = x_ctx.shape[1]
    row = lambda v: v.reshape(1, -1)
    t_lat, t_ctx = 1024, _tile(lctx, 256)
    kind = MIXERS[layer % len(MIXERS)]
    j = layer // len(MIXERS)
    ctx_out = any(MIXERS[m % len(MIXERS)] in CTX_READING_MIXERS for m in range(layer + 1, DEPTH))
    m_lat = mods[layer, :bsz]
    m_ctx = jnp.broadcast_to(mods[layer, bsz:bsz + 1], (bsz, N_MODS, d))
    g1, b1 = row(p["ln1_g"][layer]), row(p["ln1_b"][layer])

    def ffn(xs, ms, t):
        return _ffn(xs, ms, p["ffn_w_up_bf16"], p["ffn_conv_w"][layer], row(p["ffn_conv_b"][layer]),
                    p["ffn_w_down_bf16"], row(p["ln2_g"][layer]), row(p["ln2_b"][layer]),
                    layer=layer, t=t, cc=FFN_CHUNK)

    if kind == "pool":
        args = (p["pool_w"][j].astype(BF16), row(p["pool_b"][j]), row(p["pool_scale"][j]), g1, b1)
        x_lat = _pool(x_lat, m_lat, *args, t=t_lat)
        if ctx_out:
            x_ctx = _pool(x_ctx, m_ctx, *args, t=t_ctx)
    elif kind == "attn":
        wqkv = p["attn_w_qkv"][j].astype(BF16)
        wo = p["attn_w_o"][j].astype(BF16)
        sink = p["attn_sink"][j].astype(F32)
        q, kd, vd = _qkv(x_lat, m_lat, wqkv, t=t_lat, rope=True)
        qc, kdc, vdc = _qkv(x_ctx, m_ctx, wqkv, t=t_ctx, rope=False)
        o_lat = _attn(sink, q, kd, vd, kdc, vdc, local=True, qb=ATTN_QBLOCKS)
        x_lat = _proj_norm(o_lat, x_lat, m_lat, wo, g1, b1, t=t_lat)
        if ctx_out:
            o_ctx = _attn(sink, qc, None, None, kdc, vdc, local=False, qb=ATTN_QBLOCKS)
            x_ctx = _proj_norm(o_ctx, x_ctx, m_ctx, wo, g1, b1, t=t_ctx)
    elif kind == "ssm":
        assert not ctx_out
        y = _s5_mixer(x_lat, x_ctx, m_lat, m_ctx, p["ssm_lambda_re"][j], p["ssm_lambda_im"][j], p["ssm_log_dt"][j],
                       p["ssm_b_re"][j], p["ssm_b_im"][j], p["ssm_c_re"][j], p["ssm_c_im"][j], p["ssm_d"][j])
        x_lat = _glu_norm(y, x_lat, m_lat, p["ssm_w_glu_a"][j].astype(BF16), p["ssm_w_glu_b"][j].astype(BF16),
                          g1, b1, t=t_lat)
    else:
        assert not ctx_out
        x_lat = _gmlp(x_lat, m_lat, p["gmlp_w_in"][j].astype(BF16), row(p["gmlp_b_in"][j]), row(p["gmlp_ln_g"][j]),
                      row(p["gmlp_ln_b"][j]), p["gmlp_w_s"][j].astype(BF16), p["gmlp_b_s"][j].T,
                      p["gmlp_w_out"][j].astype(BF16), g1, b1, t=t_lat)
    x_lat = ffn(x_lat, m_lat, _tile(x_lat.shape[1], 1024))
    if ctx_out:
        x_ctx = ffn(x_ctx, m_ctx, t_ctx)
    return x_lat, x_ctx


def _mods(c, c_ctx, ada_w, ada_b):
    bsz, d = c.shape
    cond = jnp.concatenate([c, c_ctx[None, :], jnp.zeros((8 - bsz - 1, d), F32)], axis=0)
    return _ada(cond, ada_w, ada_b).reshape(DEPTH, 8, N_MODS, d)


def kernel(x, c, ctx, c_ctx, ada_w, ada_b, ln1_g, ln1_b, ln2_g, ln2_b, ffn_w_up, ffn_conv_w, ffn_conv_b, ffn_w_down, pool_w, pool_b, pool_scale, attn_w_qkv, attn_w_o, attn_sink, ssm_lambda_re, ssm_lambda_im, ssm_log_dt, ssm_b_re, ssm_b_im, ssm_c_re, ssm_c_im, ssm_d, ssm_w_glu_a, ssm_w_glu_b, gmlp_w_in, gmlp_b_in, gmlp_ln_g, gmlp_ln_b, gmlp_w_s, gmlp_b_s, gmlp_w_out):
    bsz, n, d = x.shape
    assert d == D_MODEL and bsz < 8 and n % 512 == 0 and ctx.shape[1] % ATTN_BLOCK == 0
    p = dict(ln1_g=ln1_g, ln1_b=ln1_b, ln2_g=ln2_g, ln2_b=ln2_b, ffn_w_up=ffn_w_up, ffn_conv_w=ffn_conv_w,
             ffn_conv_b=ffn_conv_b, ffn_w_down=ffn_w_down, pool_w=pool_w, pool_b=pool_b, pool_scale=pool_scale,
             attn_w_qkv=attn_w_qkv, attn_w_o=attn_w_o, attn_sink=attn_sink, ssm_lambda_re=ssm_lambda_re,
             ssm_lambda_im=ssm_lambda_im, ssm_log_dt=ssm_log_dt, ssm_b_re=ssm_b_re, ssm_b_im=ssm_b_im,
             ssm_c_re=ssm_c_re, ssm_c_im=ssm_c_im, ssm_d=ssm_d, ssm_w_glu_a=ssm_w_glu_a, ssm_w_glu_b=ssm_w_glu_b,
             gmlp_w_in=gmlp_w_in, gmlp_b_in=gmlp_b_in, gmlp_ln_g=gmlp_ln_g, gmlp_ln_b=gmlp_ln_b, gmlp_w_s=gmlp_w_s,
             gmlp_b_s=gmlp_b_s, gmlp_w_out=gmlp_w_out)
    p["ffn_w_up_bf16"] = ffn_w_up.astype(BF16)
    p["ffn_w_down_bf16"] = ffn_w_down.astype(BF16)
    mods = _mods(c, c_ctx, ada_w, ada_b)
    x_lat, x_ctx = x, ctx
    for layer in range(DEPTH):
        x_lat, x_ctx = _layer(layer, x_lat, x_ctx, mods, p)
    return x_lat
```

```python
import functools
import math

import jax
import jax.numpy as jnp
from jax import lax
from jax.experimental import pallas as pl
from jax.experimental.pallas import tpu as pltpu

F32 = jnp.float32
BF16 = jnp.bfloat16

D_MODEL = 1024
DEPTH = 4
MIXERS = ("pool", "attn", "ssm", "gmlp")
CTX_READING_MIXERS = ("attn", "ssm")
GRID_W = 64
N_MODS = 6
DEEPNORM_ALPHA = (2.0 * DEPTH) ** 0.25
LN_EPS = 1e-5

POOL_WINDOWS = (2, 4, 8, 16)
POOL_GROUP = D_MODEL // len(POOL_WINDOWS)

HEAD_DIM = 64
N_Q_HEADS = D_MODEL // HEAD_DIM
N_KV_HEADS = N_Q_HEADS // 4
Q_WIDTH = N_Q_HEADS * HEAD_DIM
KV_WIDTH = N_KV_HEADS * HEAD_DIM
WINDOW = 128
ATTN_BLOCK = 128
ATTN_QBLOCKS = 4
ROPE_BASE = 10000.0
NEG_INF = -1e30
LOG2E = math.log2(math.e)

SSM_GROUP = 16
SSM_N_GROUPS = D_MODEL // SSM_GROUP
SSM_STATE = 64
SSM_CHUNK = 16

GMLP_CHUNK = 128
GMLP_HALF = 2 * D_MODEL
GMLP_HEADS = 8
GMLP_HEAD_DIM = GMLP_HALF // GMLP_HEADS

FFN_HIDDEN = 2816
FFN_CHUNK = 256
OUT_ROWS = 256

LANES = 128
HALO = 16
VMEM_LIMIT = 56 * 1024 * 1024


def _cparams(*sem):
    return pltpu.CompilerParams(dimension_semantics=sem, vmem_limit_bytes=VMEM_LIMIT)


def _const_spec(shape):
    nd = len(shape)
    return pl.BlockSpec(shape, lambda *_: (0,) * nd, pipeline_mode=pl.Buffered(1))


def _post_norm(x, y, gate, g, b):
    z = DEEPNORM_ALPHA * x + gate * y
    mu = jnp.mean(z, axis=-1, keepdims=True)
    zc = z - mu
    var = jnp.mean(zc * zc, axis=-1, keepdims=True)
    return zc * lax.rsqrt(var + LN_EPS) * g + b


def _row_chunks(t):
    return [slice(r, min(r + OUT_ROWS, t)) for r in range(0, t, OUT_ROWS)]


def _dot(a, b):
    return jnp.dot(a, b, preferred_element_type=F32)


def _dot_nt(a, b):
    return lax.dot_general(a, b, (((1,), (1,)), ((), ())), preferred_element_type=F32)


def _ada_kernel(c_ref, w_ref, b_ref, o_ref):
    c = c_ref[...]
    s = (c * jax.nn.sigmoid(c)).astype(BF16)
    o_ref[0] = _dot(s, w_ref[0].astype(BF16)) + b_ref[0]


def _ada(cond, ada_w, ada_b):
    depth, d, n = ada_w.shape
    rows = cond.shape[0]
    tn = 1536
    return pl.pallas_call(
        _ada_kernel,
        out_shape=jax.ShapeDtypeStruct((depth, rows, n), F32),
        grid=(depth, n // tn),
        in_specs=[pl.BlockSpec((rows, d), lambda l, j: (0, 0)),
                  pl.BlockSpec((1, d, tn), lambda l, j: (l, 0, j)),
                  pl.BlockSpec((1, 1, tn), lambda l, j: (l, 0, j))],
        out_specs=pl.BlockSpec((1, rows, tn), lambda l, j: (l, 0, j)),
        compiler_params=_cparams("arbitrary", "arbitrary"),
        name="ada",
    )(cond, ada_w, ada_b.reshape(depth, 1, n))


def _halo_specs(t, n):
    per = t // HALO
    last = n // HALO - 1
    prev = pl.BlockSpec((1, HALO, D_MODEL), lambda b, i, *_: (b, jnp.maximum(i * per - 1, 0), 0))
    nxt = pl.BlockSpec((1, HALO, D_MODEL), lambda b, i, *_: (b, jnp.minimum((i + 1) * per, last), 0))
    return prev, nxt


def _ffn_kernel(xp_ref, x_ref, xn_ref, mod_ref, wu_ref, cw_ref, cb_ref, wd_ref, g_ref, b_ref, o_ref, act_ref,
                *, t, nt, f, cc, seg):
    i = pl.program_id(1)
    rows = t + 2 * HALO
    m = mod_ref[0]
    sh, sc = m[3:4], 1.0 + m[4:5]
    keep_p = jnp.where(i > 0, 1.0, 0.0)
    keep_n = jnp.where(i < nt - 1, 1.0, 0.0)
    x = x_ref[0]
    h = jnp.concatenate([((xp_ref[0] * sc + sh) * keep_p).astype(BF16), (x * sc + sh).astype(BF16),
                         ((xn_ref[0] * sc + sh) * keep_n).astype(BF16)], axis=0)

    if seg:
        p = lax.broadcasted_iota(jnp.int32, (rows, 1), 0) - HALO
        seq_first, seq_last = p % seg == 0, p % seg == seg - 1

    def conv(off):
        u = _dot(h, wu_ref[:, off:off + cc])
        cw = cw_ref[:, off:off + cc]
        before, after = pltpu.roll(u, 1, 0), pltpu.roll(u, rows - 1, 0)
        if seg:
            before, after = jnp.where(seq_first, 0.0, before), jnp.where(seq_last, 0.0, after)
        a = cb_ref[:, off:off + cc] + before * cw[0:1]
        a = a + u * cw[1:2]
        a = a + after * cw[2:3]
        return a[HALO:HALO + t]

    for c in range(f // cc):
        val = conv(c * cc)
        gate = conv(f + c * cc)
        act_ref[:, c * cc:(c + 1) * cc] = (val * (gate * jax.nn.sigmoid(gate))).astype(BF16)
    chunks = _row_chunks(t)
    ys = [_dot(act_ref[rs, :], wd_ref[...]) for rs in chunks]
    for rs, y in zip(chunks, ys):
        o_ref[0, rs, :] = _post_norm(x_ref[0, rs, :], y, m[5:6], g_ref[...], b_ref[...])


def _layer_spec(shape, layer):
    nd = len(shape) - 1
    return pl.BlockSpec((None,) + tuple(shape[1:]), lambda *_: (layer,) + (0,) * nd, pipeline_mode=pl.Buffered(1))


def _ffn(x, mods, w_up, conv_w, conv_b, w_down, ln_g, ln_b, *, layer, t, cc, seg=None):
    bsz, n, d = x.shape
    f = w_down.shape[1]
    nt = n // t
    assert seg is None or (nt == 1 and t % seg == 0)
    prev, nxt = _halo_specs(t, n)
    kern = functools.partial(_ffn_kernel, t=t, nt=nt, f=f, cc=cc, seg=seg)
    return pl.pallas_call(
        kern,
        out_shape=jax.ShapeDtypeStruct((bsz, n, d), F32),
        grid=(bsz, nt),
        in_specs=[prev,
                  pl.BlockSpec((1, t, d), lambda b, i: (b, i, 0)),
                  nxt,
                  pl.BlockSpec((1, N_MODS, d), lambda b, i: (b, 0, 0)),
                  _layer_spec(w_up.shape, layer), _const_spec(conv_w.shape), _const_spec(conv_b.shape),
                  _layer_spec(w_down.shape, layer), _const_spec(ln_g.shape), _const_spec(ln_b.shape)],
        out_specs=pl.BlockSpec((1, t, d), lambda b, i: (b, i, 0)),
        scratch_shapes=[pltpu.VMEM((t, f), BF16)],
        compiler_params=_cparams("arbitrary", "arbitrary"),
        name="conv_ffn",
    )(x, x, x, mods, w_up, conv_w, conv_b, w_down, ln_g, ln_b)


def _pool_kernel(xp_ref, x_ref, xn_ref, mod_ref, w_ref, pb_ref, ps_ref, g_ref, b_ref, o_ref, *, t, nt, n):
    i = pl.program_id(1)
    rows = t + 2 * HALO
    m = mod_ref[0]
    sh, sc = m[0:1], 1.0 + m[1:2]
    keep_p = jnp.where(i > 0, 1.0, 0.0)
    keep_n = jnp.where(i < nt - 1, 1.0, 0.0)
    x = x_ref[0]
    h = jnp.concatenate([(xp_ref[0] * sc + sh) * keep_p, x * sc + sh, (xn_ref[0] * sc + sh) * keep_n], axis=0)
    pos = i * t + lax.broadcasted_iota(jnp.int32, (t, 1), 0)
    outs = []
    for gi, win in enumerate(POOL_WINDOWS):
        hg = h[:, gi * POOL_GROUP:(gi + 1) * POOL_GROUP]
        s = hg + pltpu.roll(hg, 1, 0)
        span = 2
        while span < win:
            s = s + pltpu.roll(s, span, 0)
            span *= 2
        if win > 2:
            s = pltpu.roll(s, rows - (win // 2 - 1), 0)
        lo = jnp.maximum(pos - win // 2, 0)
        hi = jnp.minimum(pos - win // 2 + win, n)
        mean = s[HALO:HALO + t] / (hi - lo).astype(F32)
        mixed = (mean - hg[HALO:HALO + t]).astype(BF16)
        outs.append(_dot(mixed, w_ref[gi]))
    y = (jnp.concatenate(outs, axis=1) + pb_ref[...]) * ps_ref[...]
    o_ref[0] = _post_norm(x, y, m[2:3], g_ref[...], b_ref[...])


def _pool(x, mods, w, pb, ps, ln_g, ln_b, *, t):
    bsz, n, d = x.shape
    nt = n // t
    prev, nxt = _halo_specs(t, n)
    kern = functools.partial(_pool_kernel, t=t, nt=nt, n=n)
    vec = pl.BlockSpec((1, d), lambda b, i: (0, 0))
    return pl.pallas_call(
        kern,
        out_shape=jax.ShapeDtypeStruct((bsz, n, d), F32),
        grid=(bsz, nt),
        in_specs=[prev, pl.BlockSpec((1, t, d), lambda b, i: (b, i, 0)), nxt,
                  pl.BlockSpec((1, N_MODS, d), lambda b, i: (b, 0, 0)),
                  pl.BlockSpec(w.shape, lambda b, i: (0, 0, 0)),
                  vec, vec, vec, vec],
        out_specs=pl.BlockSpec((1, t, d), lambda b, i: (b, i, 0)),
        compiler_params=_cparams("arbitrary", "arbitrary"),
        name="pool_mixer",
    )(x, x, x, mods, w, pb, ps, ln_g, ln_b)


def _rope_tables(n):
    tpos = jnp.arange(n)
    half = HEAD_DIM // 4
    freqs = ROPE_BASE ** (-jnp.arange(half, dtype=F32) / half)
    ang_r = (tpos // GRID_W).astype(F32)[:, None] * freqs[None, :]
    ang_c = (tpos % GRID_W).astype(F32)[:, None] * freqs[None, :]
    zero = jnp.zeros_like(ang_r)
    cr, sr, cc, sc = jnp.cos(ang_r), jnp.sin(ang_r), jnp.cos(ang_c), jnp.sin(ang_c)
    cos = jnp.tile(jnp.concatenate([cr, cr, cc, cc], axis=1), (1, 2))
    sin_first = jnp.tile(jnp.concatenate([-sr, zero, -sc, zero], axis=1), (1, 2))
    sin_second = jnp.tile(jnp.concatenate([zero, sr, zero, sc], axis=1), (1, 2))
    return cos, sin_first, sin_second


def _dup_heads(chunk, lo):
    sw = pltpu.roll(chunk, HEAD_DIM, 1)
    return jnp.where(lo, chunk, sw), jnp.where(lo, sw, chunk)


def _qkv_kernel(*refs, rope):
    if rope:
        x_ref, mod_ref, w_ref, cos_ref, sa_ref, sb_ref, q_ref, kd_ref, vt_ref = refs
    else:
        x_ref, mod_ref, w_ref, q_ref, kd_ref, vt_ref = refs
    m = mod_ref[0]
    h = (x_ref[0] * (1.0 + m[1:2]) + m[0:1]).astype(BF16)
    qkv = _dot(h, w_ref[...])
    t = qkv.shape[0]
    lo = lax.broadcasted_iota(jnp.int32, (t, LANES), 1) < HEAD_DIM

    def proj(col):
        return qkv[:, col:col + LANES]

    scale = HEAD_DIM ** -0.5 * LOG2E

    def rot(v):
        if not rope:
            return v
        quarter = HEAD_DIM // 4
        return (v * cos_ref[...] + pltpu.roll(v, LANES - quarter, 1) * sa_ref[...]
                + pltpu.roll(v, quarter, 1) * sb_ref[...])

    for c in range(Q_WIDTH // LANES):
        q_ref[0, :, c * LANES:(c + 1) * LANES] = (rot(proj(c * LANES)) * scale).astype(BF16)
    for c in range(KV_WIDTH // LANES):
        k0, k1 = _dup_heads(rot(proj(Q_WIDTH + c * LANES)), lo)
        kd_ref[0, :, (2 * c) * LANES:(2 * c + 1) * LANES] = k0.astype(BF16)
        kd_ref[0, :, (2 * c + 1) * LANES:(2 * c + 2) * LANES] = k1.astype(BF16)
        v = proj(Q_WIDTH + KV_WIDTH + c * LANES)
        vt_ref[0, c * LANES:(c + 1) * LANES, :] = jnp.transpose(v).astype(BF16)


def _qkv(x, mods, w_qkv, *, t, rope):
    bsz, n, d = x.shape
    kdw = N_KV_HEADS * LANES
    ins = [x, mods, w_qkv]
    specs = [pl.BlockSpec((1, t, d), lambda b, i: (b, i, 0)),
             pl.BlockSpec((1, N_MODS, d), lambda b, i: (b, 0, 0)),
             pl.BlockSpec(w_qkv.shape, lambda b, i: (0, 0))]
    if rope:
        ins += list(_rope_tables(n))
        specs += [pl.BlockSpec((t, LANES), lambda b, i: (i, 0))] * 3
    return pl.pallas_call(
        functools.partial(_qkv_kernel, rope=rope),
        out_shape=(jax.ShapeDtypeStruct((bsz, n, Q_WIDTH), BF16),
                   jax.ShapeDtypeStruct((bsz, n, kdw), BF16),
                   jax.ShapeDtypeStruct((bsz, KV_WIDTH, n), BF16)),
        grid=(bsz, n // t),
        in_specs=specs,
        out_specs=(pl.BlockSpec((1, t, Q_WIDTH), lambda b, i: (b, i, 0)),
                   pl.BlockSpec((1, t, kdw), lambda b, i: (b, i, 0)),
                   pl.BlockSpec((1, KV_WIDTH, t), lambda b, i: (b, 0, i))),
        compiler_params=_cparams("arbitrary", "arbitrary"),
        name="qkv_rope" if rope else "qkv_ctx",
    )(*ins)


def _attn_kernel(sink_ref, q_ref, *refs, nsteps, qb, local):
    blk = ATTN_BLOCK
    if local:
        kp_ref, kc_ref, kn_ref, kx_ref, vp_ref, vc_ref, vn_ref, vx_ref, o_ref = refs
        k_band = [kp_ref[0]] + [kc_ref[0, j * blk:(j + 1) * blk, :] for j in range(qb)] + [kn_ref[0]]
        v_band = [vp_ref[0]] + [vc_ref[0, :, j * blk:(j + 1) * blk] for j in range(qb)] + [vn_ref[0]]
    else:
        kx_ref, vx_ref, o_ref = refs
    group = N_Q_HEADS // N_KV_HEADS
    cols = group * blk
    step = pl.program_id(1)
    if local:
        kj = lax.broadcasted_iota(jnp.int32, (blk, cols), 0)
        qi = lax.broadcasted_iota(jnp.int32, (blk, cols), 1) % blk
        behind, ahead = kj >= qi, kj <= qi
    lo = lax.broadcasted_iota(jnp.int32, (blk, LANES), 1) < HEAD_DIM
    head_of_col = lax.broadcasted_iota(jnp.int32, (1, cols), 1) // blk
    units = [(j, hk) for j in range(qb) for hk in range(N_KV_HEADS)]

    scores = []
    for j, hk in units:
        ks = (k_band[j:j + 3] if local else []) + [kx_ref[0]]
        kx = jnp.concatenate([k[:, hk * LANES:(hk + 1) * LANES] for k in ks], axis=0)
        parts = []
        for c in range(group // 2):
            qc = q_ref[0, j * blk:(j + 1) * blk, (hk * group // 2 + c) * LANES:(hk * group // 2 + c + 1) * LANES]
            zero = jnp.zeros_like(qc)
            parts += [jnp.where(lo, qc, zero), jnp.where(lo, zero, qc)]
        q4 = jnp.concatenate(parts, axis=0)
        scores.append(_dot_nt(kx, q4))
    probs = []
    for (j, hk), s in zip(units, scores):
        if local:
            valid_prev = behind & (step > 0) if j == 0 else behind
            valid_next = ahead & (step < nsteps - 1) if j == qb - 1 else ahead
            s = jnp.concatenate([jnp.where(valid_prev, s[:blk], NEG_INF), s[blk:2 * blk],
                                 jnp.where(valid_next, s[2 * blk:3 * blk], NEG_INF), s[3 * blk:]], axis=0)
        sink = jnp.zeros((1, cols), F32)
        for g in range(group):
            sink = jnp.where(head_of_col == g, sink_ref[hk * group + g] * LOG2E, sink)
        mx = jnp.maximum(jnp.max(s, axis=0, keepdims=True), sink)
        p = jnp.exp2(s - mx)
        den = jnp.sum(p, axis=0, keepdims=True) + jnp.exp2(sink - mx)
        probs.append((p.astype(BF16), den))
    for (j, hk), (p, den) in zip(units, probs):
        vs = (v_band[j:j + 3] if local else []) + [vx_ref[0]]
        vt = jnp.concatenate([v[hk * HEAD_DIM:(hk + 1) * HEAD_DIM, :] for v in vs], axis=1)
        ot = _dot(vt, p) / den
        for c in range(group // 2):
            pair = jnp.concatenate([ot[:, (2 * c) * blk:(2 * c + 1) * blk],
                                    ot[:, (2 * c + 1) * blk:(2 * c + 2) * blk]], axis=0)
            o_ref[0, j * blk:(j + 1) * blk, (hk * group // 2 + c) * LANES:(hk * group // 2 + c + 1) * LANES] = (
                jnp.transpose(pair).astype(BF16))


def _attn(sink, q, kd, vt, kd_ctx, vt_ctx, *, local, qb):
    bsz, n, _ = q.shape
    blk = ATTN_BLOCK
    assert WINDOW == blk, "the band is exactly the previous, own and next key block"
    nb = n // blk
    qb = math.gcd(qb, nb)
    nsteps = nb // qb
    lctx = kd_ctx.shape[1]
    kdw = kd_ctx.shape[2]
    smem = pl.BlockSpec(memory_space=pltpu.SMEM)
    qspec = pl.BlockSpec((1, qb * blk, Q_WIDTH), lambda b, i: (b, i, 0))
    kctx_spec = pl.BlockSpec((1, lctx, kdw), lambda b, i: (b, 0, 0))
    vctx_spec = pl.BlockSpec((1, KV_WIDTH, lctx), lambda b, i: (b, 0, 0))
    if local:
        prev, nxt = (lambda i: jnp.maximum(i * qb - 1, 0)), (lambda i: jnp.minimum((i + 1) * qb, nb - 1))
        kband = [pl.BlockSpec((1, blk, kdw), lambda b, i: (b, prev(i), 0)),
                 pl.BlockSpec((1, qb * blk, kdw), lambda b, i: (b, i, 0)),
                 pl.BlockSpec((1, blk, kdw), lambda b, i: (b, nxt(i), 0))]
        vband = [pl.BlockSpec((1, KV_WIDTH, blk), lambda b, i: (b, 0, prev(i))),
                 pl.BlockSpec((1, KV_WIDTH, qb * blk), lambda b, i: (b, 0, i)),
                 pl.BlockSpec((1, KV_WIDTH, blk), lambda b, i: (b, 0, nxt(i)))]
        specs = [smem, qspec] + kband + [kctx_spec] + vband + [vctx_spec]
        args = (sink, q, kd, kd, kd, kd_ctx, vt, vt, vt, vt_ctx)
    else:
        specs = [smem, qspec, kctx_spec, vctx_spec]
        args = (sink, q, kd_ctx, vt_ctx)
    return pl.pallas_call(
        functools.partial(_attn_kernel, nsteps=nsteps, qb=qb, local=local),
        out_shape=jax.ShapeDtypeStruct((bsz, n, Q_WIDTH), BF16),
        grid=(bsz, nsteps),
        in_specs=specs,
        out_specs=pl.BlockSpec((1, qb * blk, Q_WIDTH), lambda b, i: (b, i, 0)),
        compiler_params=_cparams("arbitrary", "arbitrary"),
        name="banded_attn" if local else "ctx_attn",
    )(*args)


def _proj_norm_kernel(a_ref, x_ref, mod_ref, w_ref, g_ref, b_ref, o_ref):
    chunks = _row_chunks(x_ref.shape[1])
    ys = [_dot(a_ref[0, rs, :], w_ref[...]) for rs in chunks]
    for rs, y in zip(chunks, ys):
        o_ref[0, rs, :] = _post_norm(x_ref[0, rs, :], y, mod_ref[0][2:3], g_ref[...], b_ref[...])


def _proj_norm(a, x, mods, w, ln_g, ln_b, *, t):
    bsz, n, d = x.shape
    ka = a.shape[2]
    vec = pl.BlockSpec((1, d), lambda b, i: (0, 0))
    return pl.pallas_call(
        _proj_norm_kernel,
        out_shape=jax.ShapeDtypeStruct((bsz, n, d), F32),
        grid=(bsz, n // t),
        in_specs=[pl.BlockSpec((1, t, ka), lambda b, i: (b, i, 0)),
                  pl.BlockSpec((1, t, d), lambda b, i: (b, i, 0)),
                  pl.BlockSpec((1, N_MODS, d), lambda b, i: (b, 0, 0)),
                  pl.BlockSpec(w.shape, lambda b, i: (0, 0)),
                  vec, vec],
        out_specs=pl.BlockSpec((1, t, d), lambda b, i: (b, i, 0)),
        compiler_params=_cparams("arbitrary", "arbitrary"),
        name="attn_out_norm",
    )(a, x, mods, w, ln_g, ln_b)


def _glu_norm_kernel(y_ref, x_ref, mod_ref, wa_ref, wb_ref, g_ref, b_ref, o_ref):
    gl = jax.nn.gelu(y_ref[0]).astype(BF16)
    out = _dot(gl, wa_ref[...]) * jax.nn.sigmoid(_dot(gl, wb_ref[...]))
    o_ref[0] = _post_norm(x_ref[0], out, mod_ref[0][2:3], g_ref[...], b_ref[...])


def _glu_norm(y, x, mods, wa, wb, ln_g, ln_b, *, t):
    bsz, n, d = x.shape
    tile = pl.BlockSpec((1, t, d), lambda b, i: (b, i, 0))
    vec = pl.BlockSpec((1, d), lambda b, i: (0, 0))
    wspec = pl.BlockSpec((d, d), lambda b, i: (0, 0))
    return pl.pallas_call(
        _glu_norm_kernel,
        out_shape=jax.ShapeDtypeStruct((bsz, n, d), F32),
        grid=(bsz, n // t),
        in_specs=[tile, tile, pl.BlockSpec((1, N_MODS, d), lambda b, i: (b, 0, 0)), wspec, wspec, vec, vec],
        out_specs=tile,
        compiler_params=_cparams("arbitrary", "arbitrary"),
        name="ssm_glu_norm",
    )(y, x, mods, wa, wb, ln_g, ln_b)


SSM_QG = LANES // SSM_GROUP
SSM_NQ = D_MODEL // LANES
SSM_ROWS = 8
SSM_W = SSM_CHUNK * LANES
SSM_HALF = SSM_QG * SSM_STATE
SSM_TAP = SSM_CHUNK * SSM_GROUP
SSM_VR = 2 * SSM_HALF // LANES


def _s5_prep_kernel(lr_ref, li_ref, ldt_ref, btr_ref, bti_ref, cr_ref, ci_ref, wb_ref, wc_ref, bd_ref, a_ref):
    c = SSM_CHUNK
    q = pl.program_id(0)
    lag = lax.broadcasted_iota(jnp.int32, (c + 1, LANES), 0).astype(F32)
    lane = lax.broadcasted_iota(jnp.int32, (1, LANES), 1)
    taps = [[[] for _ in range(c)] for _ in range(2)]
    for g8 in range(SSM_QG):
        own = jnp.where((lane < SSM_STATE) == (g8 % 2 == 0), 1.0, 0.0)
        wb_cols, wc_cols, a_cols = [], [], []
        for d in range(2):
            lr, li = lr_ref[g8, d], li_ref[g8, d]
            dt = jnp.exp(jnp.full((1, LANES), ldt_ref[q * SSM_QG + g8, d], F32))
            mag = jnp.exp(lag * (lr * dt))
            ang = lag * (li * dt)
            pw_r, pw_i = mag * jnp.cos(ang), mag * jnp.sin(ang)
            lbr, lbi = pw_r[1:2], pw_i[1:2]
            den = lr * lr + li * li
            qr = ((lbr - 1.0) * lr + lbi * li) / den
            qi = (lbi * lr - (lbr - 1.0) * li) / den
            btr, bti = btr_ref[g8, d], bti_ref[g8, d]
            bbr = qr * btr - qi * bti
            bbi = qr * bti + qi * btr
            cr, ci = cr_ref[g8, d], ci_ref[g8, d]

            def cl(j):
                return cr * pw_r[j:j + 1] - ci * pw_i[j:j + 1], -(cr * pw_i[j:j + 1] + ci * pw_r[j:j + 1])

            def bl(j):
                return bbr * pw_r[j:j + 1] - bbi * pw_i[j:j + 1], bbr * pw_i[j:j + 1] + bbi * pw_r[j:j + 1]

            e = jnp.concatenate([jnp.concatenate(cl(j), axis=1) for j in range(c)], axis=0)
            pieces = [jnp.concatenate([bbr * own, bbi * own], axis=1)]
            if g8 > 0:
                pieces.insert(0, jnp.zeros((g8 * SSM_GROUP, 2 * LANES), F32))
            if g8 < SSM_QG - 1:
                pieces.append(jnp.zeros(((SSM_QG - 1 - g8) * SSM_GROUP, 2 * LANES), F32))
            kt = lax.dot_general(e, jnp.concatenate(pieces, axis=0), (((1,), (1,)), ((), ())),
                                 preferred_element_type=F32, precision=lax.Precision.HIGHEST)
            for j in range(c):
                taps[d][j].append(kt[j * SSM_GROUP:(j + 1) * SSM_GROUP])
            wbl = [bl(c - 1 - k) if d == 0 else bl(k) for k in range(c)]
            wb_cols += [jnp.concatenate([w[0] for w in wbl], axis=0) * own,
                        jnp.concatenate([w[1] for w in wbl], axis=0) * own]
            wcl = [cl(k + 1) if d == 0 else cl(c - k) for k in range(c)]
            wc_cols += [jnp.concatenate([w[0] for w in wcl], axis=0) * own,
                        jnp.concatenate([w[1] for w in wcl], axis=0) * own]
            a_cols += [pw_r[c:c + 1], pw_i[c:c + 1]]
        wb_ref[g8] = jnp.concatenate(wb_cols, axis=1).astype(BF16)
        wc_ref[g8] = jnp.concatenate(wc_cols, axis=1).astype(BF16)
        a_ref[g8] = jnp.concatenate(a_cols, axis=1)
    for d in range(2):
        for j in range(c):
            blk = jnp.concatenate(taps[d][j], axis=0)
            if d == 0 and j == 0:
                blk = blk + jnp.concatenate(taps[1][0], axis=0)
            bd_ref[0, d, j] = blk.astype(BF16)


def _s5_prep(lam_re, lam_im, log_dt, b_re, b_im, c_re, c_im):
    g, nq, qg = SSM_N_GROUPS, SSM_NQ, SSM_QG
    per_g = lambda a: jnp.swapaxes(a, 0, 1)
    dup = lambda a: jnp.concatenate([a, a], axis=-1)
    lam_spec = pl.BlockSpec((qg, 2, 1, LANES), lambda i: (i, 0, 0, 0))
    mat_spec = pl.BlockSpec((qg, 2, SSM_GROUP, LANES), lambda i: (i, 0, 0, 0))
    w_spec = pl.BlockSpec((qg, SSM_TAP, 4 * LANES), lambda i: (i, 0, 0))
    return pl.pallas_call(
        _s5_prep_kernel,
        out_shape=(jax.ShapeDtypeStruct((g, SSM_TAP, 4 * LANES), BF16),
                   jax.ShapeDtypeStruct((g, SSM_TAP, 4 * LANES), BF16),
                   jax.ShapeDtypeStruct((nq, 2, SSM_CHUNK, LANES, LANES), BF16),
                   jax.ShapeDtypeStruct((g, 1, 4 * LANES), F32)),
        grid=(nq,),
        in_specs=[lam_spec, lam_spec, pl.BlockSpec(memory_space=pltpu.SMEM), mat_spec, mat_spec, mat_spec, mat_spec],
        out_specs=(w_spec, w_spec, pl.BlockSpec((1, 2, SSM_CHUNK, LANES, LANES), lambda i: (i, 0, 0, 0, 0)),
                   pl.BlockSpec((qg, 1, 4 * LANES), lambda i: (i, 0, 0))),
        compiler_params=_cparams("arbitrary"),
        name="s5_prep",
    )(dup(per_g(lam_re))[:, :, None, :], dup(per_g(lam_im))[:, :, None, :], per_g(log_dt),
      dup(jnp.swapaxes(per_g(b_re), 2, 3)), dup(jnp.swapaxes(per_g(b_im), 2, 3)), dup(per_g(c_re)), dup(per_g(c_im)))


def _s5_expand(dst_ref, src_ref):
    dst_ref[...] = jnp.zeros_like(dst_ref)
    for g8 in range(SSM_QG):
        for k in range(SSM_CHUNK):
            for c4 in range(4):
                r0 = k * LANES + g8 * SSM_GROUP
                c0 = c4 * SSM_HALF + (g8 // 2) * LANES
                dst_ref[r0:r0 + SSM_GROUP, c0:c0 + LANES] = src_ref[g8, k * SSM_GROUP:(k + 1) * SSM_GROUP,
                                                                    c4 * LANES:(c4 + 1) * LANES]


def _s5_modulate(x_ref, mod_ref, u_ref, bsz):
    for b in range(bsz):
        m = mod_ref[b]
        u_ref[b] = x_ref[b] * (1.0 + m[1:2]) + m[0:1]


def _s5_gather(u_ref, bsz, nchunk):
    rows = [jnp.concatenate([u_ref[b, pl.ds(pos, nchunk, stride=SSM_CHUNK), :] for pos in range(SSM_CHUNK)],
                            axis=1).astype(BF16) for b in range(bsz)]
    return jnp.concatenate(rows, axis=0)


def _s5_states_kernel(x_ref, mod_ref, wbc_ref, o_ref, wb_ref, u_ref, *, bsz, nchunk):
    @pl.when(pl.program_id(1) == 0)
    def _():
        _s5_expand(wb_ref, wbc_ref)

    _s5_modulate(x_ref, mod_ref, u_ref, bsz)
    s = _dot(_s5_gather(u_ref, bsz, nchunk), wb_ref[...])
    for b in range(bsz):
        for v in range(SSM_VR):
            for d in range(2):
                o_ref[v, pl.ds(d * bsz + b, nchunk, stride=SSM_ROWS), :] = (
                    s[b * nchunk:(b + 1) * nchunk, (d * SSM_VR + v) * LANES:(d * SSM_VR + v + 1) * LANES])


def _s5_states(x, mods, wbc, *, t):
    bsz, n, _ = x.shape
    nchunk = t // SSM_CHUNK
    return pl.pallas_call(
        functools.partial(_s5_states_kernel, bsz=bsz, nchunk=nchunk),
        out_shape=jax.ShapeDtypeStruct((SSM_NQ * SSM_VR, n // SSM_CHUNK * SSM_ROWS, LANES), F32),
        grid=(SSM_NQ, n // t),
        in_specs=[pl.BlockSpec((bsz, t, LANES), lambda q, i: (0, i, q)),
                  pl.BlockSpec((bsz, N_MODS, LANES), lambda q, i: (0, 0, q)),
                  pl.BlockSpec((SSM_QG, SSM_TAP, 4 * LANES), lambda q, i: (q, 0, 0))],
        out_specs=pl.BlockSpec((SSM_VR, nchunk * SSM_ROWS, LANES), lambda q, i: (q, i, 0)),
        scratch_shapes=[pltpu.VMEM((SSM_W, 4 * SSM_HALF), BF16), pltpu.VMEM((bsz, t, LANES), F32)],
        compiler_params=_cparams("arbitrary", "arbitrary"),
        name="s5_chunk_states",
    )(x, mods, wbc)


def _s5_scan_kernel(xc_ref, xl_ref, a_ref, o_ref):
    h = SSM_VR // 2
    ar, ai = a_ref[:h], a_ref[h:]
    nc, nl = xc_ref.shape[1] // SSM_ROWS, xl_ref.shape[1] // SSM_ROWS
    fwd_rows = lax.broadcasted_iota(jnp.int32, (SSM_VR, SSM_ROWS, LANES), 1) < SSM_ROWS // 2

    assert nl % 2 == 0
    chunk = lambda i: pl.ds(pl.multiple_of(i * SSM_ROWS, SSM_ROWS), SSM_ROWS)

    def step(ref, n, k, s):
        x = jnp.where(fwd_rows, ref[:, chunk(k), :], ref[:, chunk(n - 1 - k), :])
        return ar * s[0] - ai * s[1] + x[:h], ar * s[1] + ai * s[0] + x[h:]

    def first_touch(k, s):
        full = jnp.concatenate(s, axis=0)
        o_ref[:, chunk(k), :] = full
        o_ref[:, chunk(nl - 1 - k), :] = full
        return step(xl_ref, nl, k, s)

    def second_touch(k, s):
        full = jnp.concatenate(s, axis=0)
        o_ref[:, chunk(k), :] = jnp.where(fwd_rows, full, o_ref[:, chunk(k), :])
        o_ref[:, chunk(nl - 1 - k), :] = jnp.where(fwd_rows, o_ref[:, chunk(nl - 1 - k), :], full)
        return step(xl_ref, nl, k, s)

    zero = (jnp.zeros((h, SSM_ROWS, LANES), F32), jnp.zeros((h, SSM_ROWS, LANES), F32))
    s = lax.fori_loop(0, nc, lambda k, s: step(xc_ref, nc, k, s), zero)
    s = lax.fori_loop(0, nl // 2, first_touch, s)
    lax.fori_loop(nl // 2, nl, second_touch, s)


def _s5_scan(s_ctx, s_lat, a_rows):
    blk = lambda rows: pl.BlockSpec((SSM_VR, rows, LANES), lambda q: (q, 0, 0))
    return pl.pallas_call(
        _s5_scan_kernel,
        out_shape=jax.ShapeDtypeStruct(s_lat.shape, F32),
        grid=(SSM_NQ,),
        in_specs=[blk(s_ctx.shape[1]), blk(s_lat.shape[1]), blk(SSM_ROWS)],
        out_specs=blk(s_lat.shape[1]),
        compiler_params=_cparams("arbitrary"),
        name="s5_chunk_scan",
    )(s_ctx, s_lat, a_rows)


def _s5_out_kernel(x_ref, mod_ref, s_ref, bd_ref, wcc_ref, d_ref, o_ref, toep_ref, wc_ref, u_ref, y_ref, *,
                   bsz, nchunk):
    c = SSM_CHUNK

    @pl.when(pl.program_id(1) == 0)
    def _():
        _s5_expand(wc_ref, wcc_ref)
        for t in range(c):
            for k in range(c):
                toep_ref[t * LANES:(t + 1) * LANES, k * LANES:(k + 1) * LANES] = (
                    bd_ref[0, 0, t - k] if t >= k else bd_ref[0, 1, k - t])

    lhs_s = jnp.concatenate(
        [jnp.concatenate([s_ref[v, pl.ds(d * bsz + b, nchunk, stride=SSM_ROWS), :]
                          for d in range(2) for v in range(SSM_VR)], axis=1).astype(BF16)
         for b in range(bsz)], axis=0)
    _s5_modulate(x_ref, mod_ref, u_ref, bsz)
    y = _dot_nt(_s5_gather(u_ref, bsz, nchunk), toep_ref[...]) + _dot_nt(lhs_s, wc_ref[...])
    for b in range(bsz):
        for t in range(c):
            y_ref[b, pl.ds(t, nchunk, stride=c), :] = y[b * nchunk:(b + 1) * nchunk, t * LANES:(t + 1) * LANES]
    for b in range(bsz):
        o_ref[b] = y_ref[b] + d_ref[...] * u_ref[b]


def _s5_out(x, mods, s_in, bd, wcc, d_row, *, t):
    bsz, n, d = x.shape
    nchunk = t // SSM_CHUNK
    tile = pl.BlockSpec((bsz, t, LANES), lambda q, i: (0, i, q))
    return pl.pallas_call(
        functools.partial(_s5_out_kernel, bsz=bsz, nchunk=nchunk),
        out_shape=jax.ShapeDtypeStruct((bsz, n, d), F32),
        grid=(SSM_NQ, n // t),
        in_specs=[tile,
                  pl.BlockSpec((bsz, N_MODS, LANES), lambda q, i: (0, 0, q)),
                  pl.BlockSpec((SSM_VR, nchunk * SSM_ROWS, LANES), lambda q, i: (q, i, 0)),
                  pl.BlockSpec((1, 2, SSM_CHUNK, LANES, LANES), lambda q, i: (q, 0, 0, 0, 0)),
                  pl.BlockSpec((SSM_QG, SSM_TAP, 4 * LANES), lambda q, i: (q, 0, 0)),
                  pl.BlockSpec((1, LANES), lambda q, i: (0, q))],
        out_specs=tile,
        scratch_shapes=[pltpu.VMEM((SSM_W, SSM_W), BF16), pltpu.VMEM((SSM_W, 4 * SSM_HALF), BF16),
                        pltpu.VMEM((bsz, t, LANES), F32), pltpu.VMEM((bsz, t, LANES), F32)],
        compiler_params=_cparams("arbitrary", "arbitrary"),
        name="s5_chunk_out",
    )(x, mods, s_in, bd, wcc, d_row)


def _s5_mixer(x_lat, x_ctx, mods_lat, mods_ctx, lam_re, lam_im, log_dt, b_re, b_im, c_re, c_im, d_skip):
    bsz, n, _ = x_lat.shape
    assert 2 * bsz == SSM_ROWS, "state rows are (direction, batch) on the 8 sublanes"
    wbc, wcc, bd, a = _s5_prep(lam_re, lam_im, log_dt, b_re, b_im, c_re, c_im)
    t_lat = _tile(n, 1024)
    s_lat = _s5_states(x_lat, mods_lat, wbc, t=t_lat)
    s_ctx = _s5_states(x_ctx, mods_ctx, wbc, t=x_ctx.shape[1])
    a4 = a.reshape(SSM_NQ, SSM_QG, 2, 2, LANES)[..., :SSM_STATE]
    a_rows = jnp.repeat(a4.transpose(2, 0, 3, 1, 4).reshape(2, -1), bsz, axis=0)
    a_rows = a_rows.reshape(SSM_ROWS, SSM_NQ * SSM_VR, LANES).transpose(1, 0, 2)
    s_in = _s5_scan(s_ctx, s_lat, a_rows)
    return _s5_out(x_lat, mods_lat, s_in, bd, wcc, d_skip.reshape(1, -1), t=t_lat)


def _gmlp_kernel(x_ref, mod_ref, win_ref, bin_ref, lg_ref, lb_ref, ws_ref, bs_ref, wout_ref, g_ref, b_ref, o_ref,
                 v_ref, gated_ref):
    m = mod_ref[0]
    x = x_ref[0]
    t = x.shape[0]
    hw = GMLP_HEAD_DIM
    h = (x * (1.0 + m[1:2]) + m[0:1]).astype(BF16)

    def z_cols(c0):
        return jax.nn.gelu(_dot(h, win_ref[:, c0:c0 + hw]) + bin_ref[:, c0:c0 + hw])

    total = jnp.zeros((t, 1), F32)
    for hd in range(GMLP_HEADS):
        zc = z_cols(GMLP_HALF + hd * hw)
        v_ref[:, hd * hw:(hd + 1) * hw] = zc
        total = total + jnp.sum(zc, axis=-1, keepdims=True)
    mu = total / GMLP_HALF
    u_next = z_cols(0)
    sq = jnp.zeros((t, 1), F32)
    for hd in range(GMLP_HEADS):
        vc = v_ref[:, hd * hw:(hd + 1) * hw] - mu
        sq = sq + jnp.sum(vc * vc, axis=-1, keepdims=True)
    rstd = lax.rsqrt(sq / GMLP_HALF + LN_EPS)
    bs = bs_ref[...]
    for hd in range(GMLP_HEADS):
        cols = slice(hd * hw, (hd + 1) * hw)
        u = u_next
        if hd + 1 < GMLP_HEADS:
            u_next = z_cols((hd + 1) * hw)
        vn = ((v_ref[:, cols] - mu) * rstd * lg_ref[:, cols] + lb_ref[:, cols]).astype(BF16)
        for c in range(t // GMLP_CHUNK):
            rows = slice(c * GMLP_CHUNK, (c + 1) * GMLP_CHUNK)
            gate = _dot(ws_ref[hd], vn[rows]) + bs[:, hd:hd + 1]
            gated_ref[rows, cols] = (u[rows] * gate).astype(BF16)
    chunks = _row_chunks(t)
    ys = [_dot(gated_ref[rs, :], wout_ref[...]) for rs in chunks]
    for rs, y in zip(chunks, ys):
        o_ref[0, rs, :] = _post_norm(x_ref[0, rs, :], y, m[2:3], g_ref[...], b_ref[...])


def _gmlp(x, mods, w_in, b_in, lg, lb, w_s, b_s_t, w_out, ln_g, ln_b, *, t):
    bsz, n, d = x.shape
    tile = pl.BlockSpec((1, t, d), lambda b, i: (b, i, 0))
    const2 = lambda a: _const_spec(a.shape)
    return pl.pallas_call(
        _gmlp_kernel,
        out_shape=jax.ShapeDtypeStruct((bsz, n, d), F32),
        grid=(bsz, n // t),
        in_specs=[tile, pl.BlockSpec((1, N_MODS, d), lambda b, i: (b, 0, 0)),
                  const2(w_in), const2(b_in), const2(lg), const2(lb), const2(w_s), const2(b_s_t), const2(w_out),
                  const2(ln_g), const2(ln_b)],
        out_specs=tile,
        scratch_shapes=[pltpu.VMEM((t, GMLP_HALF), F32), pltpu.VMEM((t, GMLP_HALF), BF16)],
        compiler_params=_cparams("arbitrary", "arbitrary"),
        name="gmlp_mixer",
    )(x, mods, w_in, b_in, lg, lb, w_s, b_s_t, w_out, ln_g, ln_b)


def _tile(n, pref):
    return pref if n % pref == 0 else n


def _layer(layer, x_lat, x_ctx, mods, p):
    bsz, _, d = x_lat.shape
    lctx = x_ctx.shape[1]
    row = lambda v: v.reshape(1, -1)
    t_lat, t_ctx = 1024, _tile(lctx, 256)
    kind = MIXERS[layer % len(MIXERS)]
    j = layer // len(MIXERS)
    ctx_out = any(MIXERS[m % len(MIXERS)] in CTX_READING_MIXERS for m in range(layer + 1, DEPTH))
    m_lat = mods[layer, :bsz]
    m_ctx = jnp.broadcast_to(mods[layer, bsz:bsz + 1], (bsz, N_MODS, d))
    g1, b1 = row(p["ln1_g"][layer]), row(p["ln1_b"][layer])

    def ffn(xs, ms, t, seg=None):
        return _ffn(xs, ms, p["ffn_w_up_bf16"], p["ffn_conv_w"][layer], row(p["ffn_conv_b"][layer]),
                    p["ffn_w_down_bf16"], row(p["ln2_g"][layer]), row(p["ln2_b"][layer]),
                    layer=layer, t=t, cc=FFN_CHUNK, seg=seg)

    if kind == "pool":
        args = (p["pool_w"][j].astype(BF16), row(p["pool_b"][j]), row(p["pool_scale"][j]), g1, b1)
        x_lat = _pool(x_lat, m_lat, *args, t=t_lat)
        if ctx_out:
            x_ctx = _pool(x_ctx, m_ctx, *args, t=t_ctx)
    elif kind == "attn":
        wqkv = p["attn_w_qkv"][j].astype(BF16)
        wo = p["attn_w_o"][j].astype(BF16)
        sink = p["attn_sink"][j].astype(F32)
        q, kd, vd = _qkv(x_lat, m_lat, wqkv, t=t_lat, rope=True)
        qc, kdc, vdc = _qkv(x_ctx, m_ctx, wqkv, t=t_ctx, rope=False)
        o_lat = _attn(sink, q, kd, vd, kdc, vdc, local=True, qb=ATTN_QBLOCKS)
        x_lat = _proj_norm(o_lat, x_lat, m_lat, wo, g1, b1, t=t_lat)
        if ctx_out:
            o_ctx = _attn(sink, qc, None, None, kdc, vdc, local=False, qb=ATTN_QBLOCKS)
            x_ctx = _proj_norm(o_ctx, x_ctx, m_ctx, wo, g1, b1, t=t_ctx)
    elif kind == "ssm":
        assert not ctx_out
        y = _s5_mixer(x_lat, x_ctx, m_lat, m_ctx, p["ssm_lambda_re"][j], p["ssm_lambda_im"][j], p["ssm_log_dt"][j],
                       p["ssm_b_re"][j], p["ssm_b_im"][j], p["ssm_c_re"][j], p["ssm_c_im"][j], p["ssm_d"][j])
        x_lat = _glu_norm(y, x_lat, m_lat, p["ssm_w_glu_a"][j].astype(BF16), p["ssm_w_glu_b"][j].astype(BF16),
                          g1, b1, t=t_lat)
    else:
        assert not ctx_out
        x_lat = _gmlp(x_lat, m_lat, p["gmlp_w_in"][j].astype(BF16), row(p["gmlp_b_in"][j]), row(p["gmlp_ln_g"][j]),
                      row(p["gmlp_ln_b"][j]), p["gmlp_w_s"][j].astype(BF16), p["gmlp_b_s"][j].T,
                      p["gmlp_w_out"][j].astype(BF16), g1, b1, t=t_lat)
    x_lat = ffn(x_lat, m_lat, _tile(x_lat.shape[1], 1024))
    if ctx_out:
        x_ctx = ffn(x_ctx.reshape(1, bsz * lctx, d), m_ctx[:1], bsz * lctx, seg=lctx).reshape(bsz, lctx, d)
    return x_lat, x_ctx


def _mods(c, c_ctx, ada_w, ada_b):
    bsz, d = c.shape
    cond = jnp.concatenate([c, c_ctx[None, :], jnp.zeros((8 - bsz - 1, d), F32)], axis=0)
    return _ada(cond, ada_w, ada_b).reshape(DEPTH, 8, N_MODS, d)


def kernel(x, c, ctx, c_ctx, ada_w, ada_b, ln1_g, ln1_b, ln2_g, ln2_b, ffn_w_up, ffn_conv_w, ffn_conv_b, ffn_w_down, pool_w, pool_b, pool_scale, attn_w_qkv, attn_w_o, attn_sink, ssm_lambda_re, ssm_lambda_im, ssm_log_dt, ssm_b_re, ssm_b_im, ssm_c_re, ssm_c_im, ssm_d, ssm_w_glu_a, ssm_w_glu_b, gmlp_w_in, gmlp_b_in, gmlp_ln_g, gmlp_ln_b, gmlp_w_s, gmlp_b_s, gmlp_w_out):
    bsz, n, d = x.shape
    assert d == D_MODEL and bsz < 8 and n % 512 == 0 and ctx.shape[1] % ATTN_BLOCK == 0
    p = dict(ln1_g=ln1_g, ln1_b=ln1_b, ln2_g=ln2_g, ln2_b=ln2_b, ffn_w_up=ffn_w_up, ffn_conv_w=ffn_conv_w,
             ffn_conv_b=ffn_conv_b, ffn_w_down=ffn_w_down, pool_w=pool_w, pool_b=pool_b, pool_scale=pool_scale,
             attn_w_qkv=attn_w_qkv, attn_w_o=attn_w_o, attn_sink=attn_sink, ssm_lambda_re=ssm_lambda_re,
             ssm_lambda_im=ssm_lambda_im, ssm_log_dt=ssm_log_dt, ssm_b_re=ssm_b_re, ssm_b_im=ssm_b_im,
             ssm_c_re=ssm_c_re, ssm_c_im=ssm_c_im, ssm_d=ssm_d, ssm_w_glu_a=ssm_w_glu_a, ssm_w_glu_b=ssm_w_glu_b,
             gmlp_w_in=gmlp_w_in, gmlp_b_in=gmlp_b_in, gmlp_ln_g=gmlp_ln_g, gmlp_ln_b=gmlp_ln_b, gmlp_w_s=gmlp_w_s,
             gmlp_b_s=gmlp_b_s, gmlp_w_out=gmlp_w_out)
    p["ffn_w_up_bf16"] = ffn_w_up.astype(BF16)
    p["ffn_w_down_bf16"] = ffn_w_down.astype(BF16)
    mods = _mods(c, c_ctx, ada_w, ada_b)
    x_lat, x_ctx = x, ctx
    for layer in range(DEPTH):
        x_lat, x_ctx = _layer(layer, x_lat, x_ctx, mods, p)
    return x_lat
```

```python
import functools
import math

import jax
import jax.numpy as jnp
from jax import lax
from jax.experimental import pallas as pl
from jax.experimental.pallas import tpu as pltpu

F32 = jnp.float32
BF16 = jnp.bfloat16

D_MODEL = 1024
DEPTH = 4
MIXERS = ("pool", "attn", "ssm", "gmlp")
CTX_READING_MIXERS = ("attn", "ssm")
GRID_W = 64
N_MODS = 6
DEEPNORM_ALPHA = (2.0 * DEPTH) ** 0.25
LN_EPS = 1e-5

POOL_WINDOWS = (2, 4, 8, 16)
POOL_GROUP = D_MODEL // len(POOL_WINDOWS)

HEAD_DIM = 64
N_Q_HEADS = D_MODEL // HEAD_DIM
N_KV_HEADS = N_Q_HEADS // 4
Q_WIDTH = N_Q_HEADS * HEAD_DIM
KV_WIDTH = N_KV_HEADS * HEAD_DIM
WINDOW = 128
ATTN_BLOCK = 128
ATTN_QBLOCKS = 8
ROPE_BASE = 10000.0
NEG_INF = -1e30
LOG2E = math.log2(math.e)

SSM_GROUP = 16
SSM_N_GROUPS = D_MODEL // SSM_GROUP
SSM_STATE = 64
SSM_CHUNK = 16

GMLP_CHUNK = 128
GMLP_HALF = 2 * D_MODEL
GMLP_HEADS = 8
GMLP_HEAD_DIM = GMLP_HALF // GMLP_HEADS

FFN_HIDDEN = 2816
FFN_CHUNK = 256
OUT_ROWS = 256

LANES = 128
HALO = 16
VMEM_LIMIT = 56 * 1024 * 1024


def _cparams(*sem):
    return pltpu.CompilerParams(dimension_semantics=sem, vmem_limit_bytes=VMEM_LIMIT)


def _const_spec(shape):
    nd = len(shape)
    return pl.BlockSpec(shape, lambda *_: (0,) * nd, pipeline_mode=pl.Buffered(1))


def _post_norm(x, y, gate, g, b):
    z = DEEPNORM_ALPHA * x + gate * y
    mu = jnp.mean(z, axis=-1, keepdims=True)
    zc = z - mu
    var = jnp.mean(zc * zc, axis=-1, keepdims=True)
    return zc * lax.rsqrt(var + LN_EPS) * g + b


def _row_chunks(t):
    return [slice(r, min(r + OUT_ROWS, t)) for r in range(0, t, OUT_ROWS)]


def _dot(a, b):
    return jnp.dot(a, b, preferred_element_type=F32)


def _dot_nt(a, b):
    return lax.dot_general(a, b, (((1,), (1,)), ((), ())), preferred_element_type=F32)


def _ada_kernel(c_ref, w_ref, b_ref, o_ref):
    c = c_ref[...]
    s = (c * jax.nn.sigmoid(c)).astype(BF16)
    o_ref[0] = _dot(s, w_ref[0].astype(BF16)) + b_ref[0]


def _ada(cond, ada_w, ada_b):
    depth, d, n = ada_w.shape
    rows = cond.shape[0]
    tn = 1536
    return pl.pallas_call(
        _ada_kernel,
        out_shape=jax.ShapeDtypeStruct((depth, rows, n), F32),
        grid=(depth, n // tn),
        in_specs=[pl.BlockSpec((rows, d), lambda l, j: (0, 0)),
                  pl.BlockSpec((1, d, tn), lambda l, j: (l, 0, j)),
                  pl.BlockSpec((1, 1, tn), lambda l, j: (l, 0, j))],
        out_specs=pl.BlockSpec((1, rows, tn), lambda l, j: (l, 0, j)),
        compiler_params=_cparams("arbitrary", "arbitrary"),
        name="ada",
    )(cond, ada_w, ada_b.reshape(depth, 1, n))


def _halo_specs(t, n):
    per = t // HALO
    last = n // HALO - 1
    prev = pl.BlockSpec((1, HALO, D_MODEL), lambda b, i, *_: (b, jnp.maximum(i * per - 1, 0), 0))
    nxt = pl.BlockSpec((1, HALO, D_MODEL), lambda b, i, *_: (b, jnp.minimum((i + 1) * per, last), 0))
    return prev, nxt


def _ffn_kernel(xp_ref, x_ref, xn_ref, mod_ref, wu_ref, cw_ref, cb_ref, wd_ref, g_ref, b_ref, o_ref, act_ref,
                *, t, nt, f, cc, seg):
    i = pl.program_id(1)
    rows = t + 2 * HALO
    m = mod_ref[0]
    sh, sc = m[3:4], 1.0 + m[4:5]
    keep_p = jnp.where(i > 0, 1.0, 0.0)
    keep_n = jnp.where(i < nt - 1, 1.0, 0.0)
    x = x_ref[0]
    h = jnp.concatenate([((xp_ref[0] * sc + sh) * keep_p).astype(BF16), (x * sc + sh).astype(BF16),
                         ((xn_ref[0] * sc + sh) * keep_n).astype(BF16)], axis=0)

    if seg:
        p = lax.broadcasted_iota(jnp.int32, (rows, 1), 0) - HALO
        seq_first, seq_last = p % seg == 0, p % seg == seg - 1

    def conv(off):
        u = _dot(h, wu_ref[:, off:off + cc])
        cw = cw_ref[:, off:off + cc]
        before, after = pltpu.roll(u, 1, 0), pltpu.roll(u, rows - 1, 0)
        if seg:
            before, after = jnp.where(seq_first, 0.0, before), jnp.where(seq_last, 0.0, after)
        a = cb_ref[:, off:off + cc] + before * cw[0:1]
        a = a + u * cw[1:2]
        a = a + after * cw[2:3]
        return a[HALO:HALO + t]

    for c in range(f // cc):
        val = conv(c * cc)
        gate = conv(f + c * cc)
        act_ref[:, c * cc:(c + 1) * cc] = (val * (gate * jax.nn.sigmoid(gate))).astype(BF16)
    chunks = _row_chunks(t)
    ys = [_dot(act_ref[rs, :], wd_ref[...]) for rs in chunks]
    for rs, y in zip(chunks, ys):
        o_ref[0, rs, :] = _post_norm(x_ref[0, rs, :], y, m[5:6], g_ref[...], b_ref[...])


def _layer_spec(shape, layer):
    nd = len(shape) - 1
    return pl.BlockSpec((None,) + tuple(shape[1:]), lambda *_: (layer,) + (0,) * nd, pipeline_mode=pl.Buffered(1))


def _ffn(x, mods, w_up, conv_w, conv_b, w_down, ln_g, ln_b, *, layer, t, cc, seg=None):
    bsz, n, d = x.shape
    f = w_down.shape[1]
    nt = n // t
    assert seg is None or (nt == 1 and t % seg == 0)
    prev, nxt = _halo_specs(t, n)
    kern = functools.partial(_ffn_kernel, t=t, nt=nt, f=f, cc=cc, seg=seg)
    return pl.pallas_call(
        kern,
        out_shape=jax.ShapeDtypeStruct((bsz, n, d), F32),
        grid=(bsz, nt),
        in_specs=[prev,
                  pl.BlockSpec((1, t, d), lambda b, i: (b, i, 0)),
                  nxt,
                  pl.BlockSpec((1, N_MODS, d), lambda b, i: (b, 0, 0)),
                  _layer_spec(w_up.shape, layer), _const_spec(conv_w.shape), _const_spec(conv_b.shape),
                  _layer_spec(w_down.shape, layer), _const_spec(ln_g.shape), _const_spec(ln_b.shape)],
        out_specs=pl.BlockSpec((1, t, d), lambda b, i: (b, i, 0)),
        scratch_shapes=[pltpu.VMEM((t, f), BF16)],
        compiler_params=_cparams("arbitrary", "arbitrary"),
        name="conv_ffn",
    )(x, x, x, mods, w_up, conv_w, conv_b, w_down, ln_g, ln_b)


def _pool_kernel(xp_ref, x_ref, xn_ref, mod_ref, w_ref, pb_ref, ps_ref, g_ref, b_ref, o_ref, *, t, nt, n):
    i = pl.program_id(1)
    rows = t + 2 * HALO
    m = mod_ref[0]
    sh, sc = m[0:1], 1.0 + m[1:2]
    keep_p = jnp.where(i > 0, 1.0, 0.0)
    keep_n = jnp.where(i < nt - 1, 1.0, 0.0)
    x = x_ref[0]
    h = jnp.concatenate([(xp_ref[0] * sc + sh) * keep_p, x * sc + sh, (xn_ref[0] * sc + sh) * keep_n], axis=0)
    pos = i * t + lax.broadcasted_iota(jnp.int32, (t, 1), 0)
    outs = []
    for gi, win in enumerate(POOL_WINDOWS):
        hg = h[:, gi * POOL_GROUP:(gi + 1) * POOL_GROUP]
        s = hg + pltpu.roll(hg, 1, 0)
        span = 2
        while span < win:
            s = s + pltpu.roll(s, span, 0)
            span *= 2
        if win > 2:
            s = pltpu.roll(s, rows - (win // 2 - 1), 0)
        lo = jnp.maximum(pos - win // 2, 0)
        hi = jnp.minimum(pos - win // 2 + win, n)
        mean = s[HALO:HALO + t] / (hi - lo).astype(F32)
        mixed = (mean - hg[HALO:HALO + t]).astype(BF16)
        outs.append(_dot(mixed, w_ref[gi]))
    y = (jnp.concatenate(outs, axis=1) + pb_ref[...]) * ps_ref[...]
    o_ref[0] = _post_norm(x, y, m[2:3], g_ref[...], b_ref[...])


def _pool(x, mods, w, pb, ps, ln_g, ln_b, *, t):
    bsz, n, d = x.shape
    nt = n // t
    prev, nxt = _halo_specs(t, n)
    kern = functools.partial(_pool_kernel, t=t, nt=nt, n=n)
    vec = pl.BlockSpec((1, d), lambda b, i: (0, 0))
    return pl.pallas_call(
        kern,
        out_shape=jax.ShapeDtypeStruct((bsz, n, d), F32),
        grid=(bsz, nt),
        in_specs=[prev, pl.BlockSpec((1, t, d), lambda b, i: (b, i, 0)), nxt,
                  pl.BlockSpec((1, N_MODS, d), lambda b, i: (b, 0, 0)),
                  pl.BlockSpec(w.shape, lambda b, i: (0, 0, 0)),
                  vec, vec, vec, vec],
        out_specs=pl.BlockSpec((1, t, d), lambda b, i: (b, i, 0)),
        compiler_params=_cparams("arbitrary", "arbitrary"),
        name="pool_mixer",
    )(x, x, x, mods, w, pb, ps, ln_g, ln_b)


def _rope_tables(n):
    tpos = jnp.arange(n)
    half = HEAD_DIM // 4
    freqs = ROPE_BASE ** (-jnp.arange(half, dtype=F32) / half)
    ang_r = (tpos // GRID_W).astype(F32)[:, None] * freqs[None, :]
    ang_c = (tpos % GRID_W).astype(F32)[:, None] * freqs[None, :]
    zero = jnp.zeros_like(ang_r)
    cr, sr, cc, sc = jnp.cos(ang_r), jnp.sin(ang_r), jnp.cos(ang_c), jnp.sin(ang_c)
    cos = jnp.tile(jnp.concatenate([cr, cr, cc, cc], axis=1), (1, 2))
    sin_first = jnp.tile(jnp.concatenate([-sr, zero, -sc, zero], axis=1), (1, 2))
    sin_second = jnp.tile(jnp.concatenate([zero, sr, zero, sc], axis=1), (1, 2))
    return cos, sin_first, sin_second


def _dup_heads(chunk, lo):
    sw = pltpu.roll(chunk, HEAD_DIM, 1)
    return jnp.where(lo, chunk, sw), jnp.where(lo, sw, chunk)


def _qkv_kernel(*refs, rope):
    if rope:
        x_ref, mod_ref, w_ref, cos_ref, sa_ref, sb_ref, q_ref, kd_ref, vt_ref = refs
    else:
        x_ref, mod_ref, w_ref, q_ref, kd_ref, vt_ref = refs
    m = mod_ref[0]
    h = (x_ref[0] * (1.0 + m[1:2]) + m[0:1]).astype(BF16)
    qkv = _dot(h, w_ref[...])
    t = qkv.shape[0]
    lo = lax.broadcasted_iota(jnp.int32, (t, LANES), 1) < HEAD_DIM

    def proj(col):
        return qkv[:, col:col + LANES]

    scale = HEAD_DIM ** -0.5 * LOG2E

    def rot(v):
        if not rope:
            return v
        quarter = HEAD_DIM // 4
        return (v * cos_ref[...] + pltpu.roll(v, LANES - quarter, 1) * sa_ref[...]
                + pltpu.roll(v, quarter, 1) * sb_ref[...])

    for c in range(Q_WIDTH // LANES):
        q_ref[0, :, c * LANES:(c + 1) * LANES] = (rot(proj(c * LANES)) * scale).astype(BF16)
    for c in range(KV_WIDTH // LANES):
        k0, k1 = _dup_heads(rot(proj(Q_WIDTH + c * LANES)), lo)
        kd_ref[0, :, (2 * c) * LANES:(2 * c + 1) * LANES] = k0.astype(BF16)
        kd_ref[0, :, (2 * c + 1) * LANES:(2 * c + 2) * LANES] = k1.astype(BF16)
        v = proj(Q_WIDTH + KV_WIDTH + c * LANES)
        vt_ref[0, c * LANES:(c + 1) * LANES, :] = jnp.transpose(v).astype(BF16)


def _qkv(x, mods, w_qkv, *, t, rope):
    bsz, n, d = x.shape
    kdw = N_KV_HEADS * LANES
    ins = [x, mods, w_qkv]
    specs = [pl.BlockSpec((1, t, d), lambda b, i: (b, i, 0)),
             pl.BlockSpec((1, N_MODS, d), lambda b, i: (b, 0, 0)),
             pl.BlockSpec(w_qkv.shape, lambda b, i: (0, 0))]
    if rope:
        ins += list(_rope_tables(n))
        specs += [pl.BlockSpec((t, LANES), lambda b, i: (i, 0))] * 3
    return pl.pallas_call(
        functools.partial(_qkv_kernel, rope=rope),
        out_shape=(jax.ShapeDtypeStruct((bsz, n, Q_WIDTH), BF16),
                   jax.ShapeDtypeStruct((bsz, n, kdw), BF16),
                   jax.ShapeDtypeStruct((bsz, KV_WIDTH, n), BF16)),
        grid=(bsz, n // t),
        in_specs=specs,
        out_specs=(pl.BlockSpec((1, t, Q_WIDTH), lambda b, i: (b, i, 0)),
                   pl.BlockSpec((1, t, kdw), lambda b, i: (b, i, 0)),
                   pl.BlockSpec((1, KV_WIDTH, t), lambda b, i: (b, 0, i))),
        compiler_params=_cparams("arbitrary", "arbitrary"),
        name="qkv_rope" if rope else "qkv_ctx",
    )(*ins)


def _attn_kernel(sink_ref, q_ref, *refs, nsteps, qb, local):
    blk = ATTN_BLOCK
    if local:
        kp_ref, kc_ref, kn_ref, kx_ref, vp_ref, vc_ref, vn_ref, vx_ref, o_ref = refs
        k_band = [kp_ref[0]] + [kc_ref[0, j * blk:(j + 1) * blk, :] for j in range(qb)] + [kn_ref[0]]
        v_band = [vp_ref[0]] + [vc_ref[0, :, j * blk:(j + 1) * blk] for j in range(qb)] + [vn_ref[0]]
    else:
        kx_ref, vx_ref, o_ref = refs
    group = N_Q_HEADS // N_KV_HEADS
    cols = group * blk
    step = pl.program_id(1)
    if local:
        kj = lax.broadcasted_iota(jnp.int32, (blk, cols), 0)
        qi = lax.broadcasted_iota(jnp.int32, (blk, cols), 1) % blk
        behind, ahead = kj >= qi, kj <= qi
    lo = lax.broadcasted_iota(jnp.int32, (blk, LANES), 1) < HEAD_DIM
    head_of_col = lax.broadcasted_iota(jnp.int32, (1, cols), 1) // blk
    units = [(j, hk) for j in range(qb) for hk in range(N_KV_HEADS)]

    scores = []
    for j, hk in units:
        ks = (k_band[j:j + 3] if local else []) + [kx_ref[0]]
        kx = jnp.concatenate([k[:, hk * LANES:(hk + 1) * LANES] for k in ks], axis=0)
        parts = []
        for c in range(group // 2):
            qc = q_ref[0, j * blk:(j + 1) * blk, (hk * group // 2 + c) * LANES:(hk * group // 2 + c + 1) * LANES]
            zero = jnp.zeros_like(qc)
            parts += [jnp.where(lo, qc, zero), jnp.where(lo, zero, qc)]
        q4 = jnp.concatenate(parts, axis=0)
        scores.append(_dot_nt(kx, q4))
    probs = []
    for (j, hk), s in zip(units, scores):
        if local:
            valid_prev = behind & (step > 0) if j == 0 else behind
            valid_next = ahead & (step < nsteps - 1) if j == qb - 1 else ahead
            s = jnp.concatenate([jnp.where(valid_prev, s[:blk], NEG_INF), s[blk:2 * blk],
                                 jnp.where(valid_next, s[2 * blk:3 * blk], NEG_INF), s[3 * blk:]], axis=0)
        sink = jnp.zeros((1, cols), F32)
        for g in range(group):
            sink = jnp.where(head_of_col == g, sink_ref[hk * group + g] * LOG2E, sink)
        mx = jnp.maximum(jnp.max(s, axis=0, keepdims=True), sink)
        p = jnp.exp2(s - mx)
        den = jnp.sum(p, axis=0, keepdims=True) + jnp.exp2(sink - mx)
        probs.append((p.astype(BF16), den))
    for (j, hk), (p, den) in zip(units, probs):
        vs = (v_band[j:j + 3] if local else []) + [vx_ref[0]]
        vt = jnp.concatenate([v[hk * HEAD_DIM:(hk + 1) * HEAD_DIM, :] for v in vs], axis=1)
        ot = _dot(vt, p) / den
        for c in range(group // 2):
            pair = jnp.concatenate([ot[:, (2 * c) * blk:(2 * c + 1) * blk],
                                    ot[:, (2 * c + 1) * blk:(2 * c + 2) * blk]], axis=0)
            o_ref[0, j * blk:(j + 1) * blk, (hk * group // 2 + c) * LANES:(hk * group // 2 + c + 1) * LANES] = (
                jnp.transpose(pair).astype(BF16))


def _attn(sink, q, kd, vt, kd_ctx, vt_ctx, *, local, qb):
    bsz, n, _ = q.shape
    blk = ATTN_BLOCK
    assert WINDOW == blk, "the band is exactly the previous, own and next key block"
    nb = n // blk
    qb = math.gcd(qb, nb)
    nsteps = nb // qb
    lctx = kd_ctx.shape[1]
    kdw = kd_ctx.shape[2]
    smem = pl.BlockSpec(memory_space=pltpu.SMEM)
    qspec = pl.BlockSpec((1, qb * blk, Q_WIDTH), lambda b, i: (b, i, 0))
    kctx_spec = pl.BlockSpec((1, lctx, kdw), lambda b, i: (b, 0, 0))
    vctx_spec = pl.BlockSpec((1, KV_WIDTH, lctx), lambda b, i: (b, 0, 0))
    if local:
        prev, nxt = (lambda i: jnp.maximum(i * qb - 1, 0)), (lambda i: jnp.minimum((i + 1) * qb, nb - 1))
        kband = [pl.BlockSpec((1, blk, kdw), lambda b, i: (b, prev(i), 0)),
                 pl.BlockSpec((1, qb * blk, kdw), lambda b, i: (b, i, 0)),
                 pl.BlockSpec((1, blk, kdw), lambda b, i: (b, nxt(i), 0))]
        vband = [pl.BlockSpec((1, KV_WIDTH, blk), lambda b, i: (b, 0, prev(i))),
                 pl.BlockSpec((1, KV_WIDTH, qb * blk), lambda b, i: (b, 0, i)),
                 pl.BlockSpec((1, KV_WIDTH, blk), lambda b, i: (b, 0, nxt(i)))]
        specs = [smem, qspec] + kband + [kctx_spec] + vband + [vctx_spec]
        args = (sink, q, kd, kd, kd, kd_ctx, vt, vt, vt, vt_ctx)
    else:
        specs = [smem, qspec, kctx_spec, vctx_spec]
        args = (sink, q, kd_ctx, vt_ctx)
    return pl.pallas_call(
        functools.partial(_attn_kernel, nsteps=nsteps, qb=qb, local=local),
        out_shape=jax.ShapeDtypeStruct((bsz, n, Q_WIDTH), BF16),
        grid=(bsz, nsteps),
        in_specs=specs,
        out_specs=pl.BlockSpec((1, qb * blk, Q_WIDTH), lambda b, i: (b, i, 0)),
        compiler_params=_cparams("arbitrary", "arbitrary"),
        name="banded_attn" if local else "ctx_attn",
    )(*args)


def _proj_norm_kernel(a_ref, x_ref, mod_ref, w_ref, g_ref, b_ref, o_ref):
    chunks = _row_chunks(x_ref.shape[1])
    ys = [_dot(a_ref[0, rs, :], w_ref[...]) for rs in chunks]
    for rs, y in zip(chunks, ys):
        o_ref[0, rs, :] = _post_norm(x_ref[0, rs, :], y, mod_ref[0][2:3], g_ref[...], b_ref[...])


def _proj_norm(a, x, mods, w, ln_g, ln_b, *, t):
    bsz, n, d = x.shape
    ka = a.shape[2]
    vec = pl.BlockSpec((1, d), lambda b, i: (0, 0))
    return pl.pallas_call(
        _proj_norm_kernel,
        out_shape=jax.ShapeDtypeStruct((bsz, n, d), F32),
        grid=(bsz, n // t),
        in_specs=[pl.BlockSpec((1, t, ka), lambda b, i: (b, i, 0)),
                  pl.BlockSpec((1, t, d), lambda b, i: (b, i, 0)),
                  pl.BlockSpec((1, N_MODS, d), lambda b, i: (b, 0, 0)),
                  pl.BlockSpec(w.shape, lambda b, i: (0, 0)),
                  vec, vec],
        out_specs=pl.BlockSpec((1, t, d), lambda b, i: (b, i, 0)),
        compiler_params=_cparams("arbitrary", "arbitrary"),
        name="attn_out_norm",
    )(a, x, mods, w, ln_g, ln_b)


def _glu_norm_kernel(y_ref, x_ref, mod_ref, wa_ref, wb_ref, g_ref, b_ref, o_ref):
    gl = jax.nn.gelu(y_ref[0]).astype(BF16)
    out = _dot(gl, wa_ref[...]) * jax.nn.sigmoid(_dot(gl, wb_ref[...]))
    o_ref[0] = _post_norm(x_ref[0], out, mod_ref[0][2:3], g_ref[...], b_ref[...])


def _glu_norm(y, x, mods, wa, wb, ln_g, ln_b, *, t):
    bsz, n, d = x.shape
    tile = pl.BlockSpec((1, t, d), lambda b, i: (b, i, 0))
    vec = pl.BlockSpec((1, d), lambda b, i: (0, 0))
    wspec = pl.BlockSpec((d, d), lambda b, i: (0, 0))
    return pl.pallas_call(
        _glu_norm_kernel,
        out_shape=jax.ShapeDtypeStruct((bsz, n, d), F32),
        grid=(bsz, n // t),
        in_specs=[tile, tile, pl.BlockSpec((1, N_MODS, d), lambda b, i: (b, 0, 0)), wspec, wspec, vec, vec],
        out_specs=tile,
        compiler_params=_cparams("arbitrary", "arbitrary"),
        name="ssm_glu_norm",
    )(y, x, mods, wa, wb, ln_g, ln_b)


SSM_QG = LANES // SSM_GROUP
SSM_NQ = D_MODEL // LANES
SSM_ROWS = 8
SSM_W = SSM_CHUNK * LANES
SSM_HALF = SSM_QG * SSM_STATE
SSM_TAP = SSM_CHUNK * SSM_GROUP
SSM_VR = 2 * SSM_HALF // LANES


def _s5_prep_kernel(lr_ref, li_ref, ldt_ref, btr_ref, bti_ref, cr_ref, ci_ref, wb_ref, wc_ref, bd_ref, a_ref):
    c = SSM_CHUNK
    q = pl.program_id(0)
    lag = lax.broadcasted_iota(jnp.int32, (c + 1, LANES), 0).astype(F32)
    lane = lax.broadcasted_iota(jnp.int32, (1, LANES), 1)
    taps = [[[] for _ in range(c)] for _ in range(2)]
    for g8 in range(SSM_QG):
        own = jnp.where((lane < SSM_STATE) == (g8 % 2 == 0), 1.0, 0.0)
        wb_cols, wc_cols, a_cols = [], [], []
        for d in range(2):
            lr, li = lr_ref[g8, d], li_ref[g8, d]
            dt = jnp.exp(jnp.full((1, LANES), ldt_ref[q * SSM_QG + g8, d], F32))
            mag = jnp.exp(lag * (lr * dt))
            ang = lag * (li * dt)
            pw_r, pw_i = mag * jnp.cos(ang), mag * jnp.sin(ang)
            lbr, lbi = pw_r[1:2], pw_i[1:2]
            den = lr * lr + li * li
            qr = ((lbr - 1.0) * lr + lbi * li) / den
            qi = (lbi * lr - (lbr - 1.0) * li) / den
            btr, bti = btr_ref[g8, d], bti_ref[g8, d]
            bbr = qr * btr - qi * bti
            bbi = qr * bti + qi * btr
            cr, ci = cr_ref[g8, d], ci_ref[g8, d]

            def cl(j):
                return cr * pw_r[j:j + 1] - ci * pw_i[j:j + 1], -(cr * pw_i[j:j + 1] + ci * pw_r[j:j + 1])

            def bl(j):
                return bbr * pw_r[j:j + 1] - bbi * pw_i[j:j + 1], bbr * pw_i[j:j + 1] + bbi * pw_r[j:j + 1]

            e = jnp.concatenate([jnp.concatenate(cl(j), axis=1) for j in range(c)], axis=0)
            pieces = [jnp.concatenate([bbr * own, bbi * own], axis=1)]
            if g8 > 0:
                pieces.insert(0, jnp.zeros((g8 * SSM_GROUP, 2 * LANES), F32))
            if g8 < SSM_QG - 1:
                pieces.append(jnp.zeros(((SSM_QG - 1 - g8) * SSM_GROUP, 2 * LANES), F32))
            kt = lax.dot_general(e, jnp.concatenate(pieces, axis=0), (((1,), (1,)), ((), ())),
                                 preferred_element_type=F32, precision=lax.Precision.HIGHEST)
            for j in range(c):
                taps[d][j].append(kt[j * SSM_GROUP:(j + 1) * SSM_GROUP])
            wbl = [bl(c - 1 - k) if d == 0 else bl(k) for k in range(c)]
            wb_cols += [jnp.concatenate([w[0] for w in wbl], axis=0) * own,
                        jnp.concatenate([w[1] for w in wbl], axis=0) * own]
            wcl = [cl(k + 1) if d == 0 else cl(c - k) for k in range(c)]
            wc_cols += [jnp.concatenate([w[0] for w in wcl], axis=0) * own,
                        jnp.concatenate([w[1] for w in wcl], axis=0) * own]
            a_cols += [pw_r[c:c + 1], pw_i[c:c + 1]]
        wb_ref[g8] = jnp.concatenate(wb_cols, axis=1).astype(BF16)
        wc_ref[g8] = jnp.concatenate(wc_cols, axis=1).astype(BF16)
        a_ref[g8] = jnp.concatenate(a_cols, axis=1)
    for d in range(2):
        for j in range(c):
            blk = jnp.concatenate(taps[d][j], axis=0)
            if d == 0 and j == 0:
                blk = blk + jnp.concatenate(taps[1][0], axis=0)
            bd_ref[0, d, j] = blk.astype(BF16)


def _s5_prep(lam_re, lam_im, log_dt, b_re, b_im, c_re, c_im):
    g, nq, qg = SSM_N_GROUPS, SSM_NQ, SSM_QG
    per_g = lambda a: jnp.swapaxes(a, 0, 1)
    dup = lambda a: jnp.concatenate([a, a], axis=-1)
    lam_spec = pl.BlockSpec((qg, 2, 1, LANES), lambda i: (i, 0, 0, 0))
    mat_spec = pl.BlockSpec((qg, 2, SSM_GROUP, LANES), lambda i: (i, 0, 0, 0))
    w_spec = pl.BlockSpec((qg, SSM_TAP, 4 * LANES), lambda i: (i, 0, 0))
    return pl.pallas_call(
        _s5_prep_kernel,
        out_shape=(jax.ShapeDtypeStruct((g, SSM_TAP, 4 * LANES), BF16),
                   jax.ShapeDtypeStruct((g, SSM_TAP, 4 * LANES), BF16),
                   jax.ShapeDtypeStruct((nq, 2, SSM_CHUNK, LANES, LANES), BF16),
                   jax.ShapeDtypeStruct((g, 1, 4 * LANES), F32)),
        grid=(nq,),
        in_specs=[lam_spec, lam_spec, pl.BlockSpec(memory_space=pltpu.SMEM), mat_spec, mat_spec, mat_spec, mat_spec],
        out_specs=(w_spec, w_spec, pl.BlockSpec((1, 2, SSM_CHUNK, LANES, LANES), lambda i: (i, 0, 0, 0, 0)),
                   pl.BlockSpec((qg, 1, 4 * LANES), lambda i: (i, 0, 0))),
        compiler_params=_cparams("arbitrary"),
        name="s5_prep",
    )(dup(per_g(lam_re))[:, :, None, :], dup(per_g(lam_im))[:, :, None, :], per_g(log_dt),
      dup(jnp.swapaxes(per_g(b_re), 2, 3)), dup(jnp.swapaxes(per_g(b_im), 2, 3)), dup(per_g(c_re)), dup(per_g(c_im)))


def _s5_expand(dst_ref, src_ref):
    dst_ref[...] = jnp.zeros_like(dst_ref)
    for g8 in range(SSM_QG):
        for k in range(SSM_CHUNK):
            for c4 in range(4):
                r0 = k * LANES + g8 * SSM_GROUP
                c0 = c4 * SSM_HALF + (g8 // 2) * LANES
                dst_ref[r0:r0 + SSM_GROUP, c0:c0 + LANES] = src_ref[g8, k * SSM_GROUP:(k + 1) * SSM_GROUP,
                                                                    c4 * LANES:(c4 + 1) * LANES]


def _s5_modulate(x_ref, mod_ref, u_ref, bsz):
    for b in range(bsz):
        m = mod_ref[b]
        u_ref[b] = x_ref[b] * (1.0 + m[1:2]) + m[0:1]


def _s5_gather(u_ref, bsz, nchunk):
    rows = [jnp.concatenate([u_ref[b, pl.ds(pos, nchunk, stride=SSM_CHUNK), :] for pos in range(SSM_CHUNK)],
                            axis=1).astype(BF16) for b in range(bsz)]
    return jnp.concatenate(rows, axis=0)


def _s5_states_kernel(x_ref, mod_ref, wbc_ref, o_ref, wb_ref, u_ref, *, bsz, nchunk):
    @pl.when(pl.program_id(1) == 0)
    def _():
        _s5_expand(wb_ref, wbc_ref)

    _s5_modulate(x_ref, mod_ref, u_ref, bsz)
    s = _dot(_s5_gather(u_ref, bsz, nchunk), wb_ref[...])
    for b in range(bsz):
        for v in range(SSM_VR):
            for d in range(2):
                o_ref[v, pl.ds(d * bsz + b, nchunk, stride=SSM_ROWS), :] = (
                    s[b * nchunk:(b + 1) * nchunk, (d * SSM_VR + v) * LANES:(d * SSM_VR + v + 1) * LANES])


def _s5_states(x, mods, wbc, *, t):
    bsz, n, _ = x.shape
    nchunk = t // SSM_CHUNK
    return pl.pallas_call(
        functools.partial(_s5_states_kernel, bsz=bsz, nchunk=nchunk),
        out_shape=jax.ShapeDtypeStruct((SSM_NQ * SSM_VR, n // SSM_CHUNK * SSM_ROWS, LANES), F32),
        grid=(SSM_NQ, n // t),
        in_specs=[pl.BlockSpec((bsz, t, LANES), lambda q, i: (0, i, q)),
                  pl.BlockSpec((bsz, N_MODS, LANES), lambda q, i: (0, 0, q)),
                  pl.BlockSpec((SSM_QG, SSM_TAP, 4 * LANES), lambda q, i: (q, 0, 0))],
        out_specs=pl.BlockSpec((SSM_VR, nchunk * SSM_ROWS, LANES), lambda q, i: (q, i, 0)),
        scratch_shapes=[pltpu.VMEM((SSM_W, 4 * SSM_HALF), BF16), pltpu.VMEM((bsz, t, LANES), F32)],
        compiler_params=_cparams("arbitrary", "arbitrary"),
        name="s5_chunk_states",
    )(x, mods, wbc)


def _s5_scan_kernel(xc_ref, xl_ref, a_ref, o_ref):
    h = SSM_VR // 2
    ar, ai = a_ref[:h], a_ref[h:]
    nc, nl = xc_ref.shape[1] // SSM_ROWS, xl_ref.shape[1] // SSM_ROWS
    fwd_rows = lax.broadcasted_iota(jnp.int32, (SSM_VR, SSM_ROWS, LANES), 1) < SSM_ROWS // 2

    assert nl % 2 == 0
    chunk = lambda i: pl.ds(pl.multiple_of(i * SSM_ROWS, SSM_ROWS), SSM_ROWS)

    def step(ref, n, k, s):
        x = jnp.where(fwd_rows, ref[:, chunk(k), :], ref[:, chunk(n - 1 - k), :])
        return ar * s[0] - ai * s[1] + x[:h], ar * s[1] + ai * s[0] + x[h:]

    def first_touch(k, s):
        full = jnp.concatenate(s, axis=0)
        o_ref[:, chunk(k), :] = full
        o_ref[:, chunk(nl - 1 - k), :] = full
        return step(xl_ref, nl, k, s)

    def second_touch(k, s):
        full = jnp.concatenate(s, axis=0)
        o_ref[:, chunk(k), :] = jnp.where(fwd_rows, full, o_ref[:, chunk(k), :])
        o_ref[:, chunk(nl - 1 - k), :] = jnp.where(fwd_rows, o_ref[:, chunk(nl - 1 - k), :], full)
        return step(xl_ref, nl, k, s)

    zero = (jnp.zeros((h, SSM_ROWS, LANES), F32), jnp.zeros((h, SSM_ROWS, LANES), F32))
    s = lax.fori_loop(0, nc, lambda k, s: step(xc_ref, nc, k, s), zero)
    s = lax.fori_loop(0, nl // 2, first_touch, s)
    lax.fori_loop(nl // 2, nl, second_touch, s)


def _s5_scan(s_ctx, s_lat, a_rows):
    blk = lambda rows: pl.BlockSpec((SSM_VR, rows, LANES), lambda q: (q, 0, 0))
    return pl.pallas_call(
        _s5_scan_kernel,
        out_shape=jax.ShapeDtypeStruct(s_lat.shape, F32),
        grid=(SSM_NQ,),
        in_specs=[blk(s_ctx.shape[1]), blk(s_lat.shape[1]), blk(SSM_ROWS)],
        out_specs=blk(s_lat.shape[1]),
        compiler_params=_cparams("arbitrary"),
        name="s5_chunk_scan",
    )(s_ctx, s_lat, a_rows)


def _s5_out_kernel(x_ref, mod_ref, s_ref, bd_ref, wcc_ref, d_ref, o_ref, toep_ref, wc_ref, u_ref, y_ref, *,
                   bsz, nchunk):
    c = SSM_CHUNK

    @pl.when(pl.program_id(1) == 0)
    def _():
        _s5_expand(wc_ref, wcc_ref)
        for t in range(c):
            for k in range(c):
                toep_ref[t * LANES:(t + 1) * LANES, k * LANES:(k + 1) * LANES] = (
                    bd_ref[0, 0, t - k] if t >= k else bd_ref[0, 1, k - t])

    lhs_s = jnp.concatenate(
        [jnp.concatenate([s_ref[v, pl.ds(d * bsz + b, nchunk, stride=SSM_ROWS), :]
                          for d in range(2) for v in range(SSM_VR)], axis=1).astype(BF16)
         for b in range(bsz)], axis=0)
    _s5_modulate(x_ref, mod_ref, u_ref, bsz)
    y = _dot_nt(_s5_gather(u_ref, bsz, nchunk), toep_ref[...]) + _dot_nt(lhs_s, wc_ref[...])
    for b in range(bsz):
        for t in range(c):
            y_ref[b, pl.ds(t, nchunk, stride=c), :] = y[b * nchunk:(b + 1) * nchunk, t * LANES:(t + 1) * LANES]
    for b in range(bsz):
        o_ref[b] = y_ref[b] + d_ref[...] * u_ref[b]


def _s5_out(x, mods, s_in, bd, wcc, d_row, *, t):
    bsz, n, d = x.shape
    nchunk = t // SSM_CHUNK
    tile = pl.BlockSpec((bsz, t, LANES), lambda q, i: (0, i, q))
    return pl.pallas_call(
        functools.partial(_s5_out_kernel, bsz=bsz, nchunk=nchunk),
        out_shape=jax.ShapeDtypeStruct((bsz, n, d), F32),
        grid=(SSM_NQ, n // t),
        in_specs=[tile,
                  pl.BlockSpec((bsz, N_MODS, LANES), lambda q, i: (0, 0, q)),
                  pl.BlockSpec((SSM_VR, nchunk * SSM_ROWS, LANES), lambda q, i: (q, i, 0)),
                  pl.BlockSpec((1, 2, SSM_CHUNK, LANES, LANES), lambda q, i: (q, 0, 0, 0, 0)),
                  pl.BlockSpec((SSM_QG, SSM_TAP, 4 * LANES), lambda q, i: (q, 0, 0)),
                  pl.BlockSpec((1, LANES), lambda q, i: (0, q))],
        out_specs=tile,
        scratch_shapes=[pltpu.VMEM((SSM_W, SSM_W), BF16), pltpu.VMEM((SSM_W, 4 * SSM_HALF), BF16),
                        pltpu.VMEM((bsz, t, LANES), F32), pltpu.VMEM((bsz, t, LANES), F32)],
        compiler_params=_cparams("arbitrary", "arbitrary"),
        name="s5_chunk_out",
    )(x, mods, s_in, bd, wcc, d_row)


def _s5_mixer(x_lat, x_ctx, mods_lat, mods_ctx, lam_re, lam_im, log_dt, b_re, b_im, c_re, c_im, d_skip):
    bsz, n, _ = x_lat.shape
    assert 2 * bsz == SSM_ROWS, "state rows are (direction, batch) on the 8 sublanes"
    wbc, wcc, bd, a = _s5_prep(lam_re, lam_im, log_dt, b_re, b_im, c_re, c_im)
    t_lat = _tile(n, 1024)
    s_lat = _s5_states(x_lat, mods_lat, wbc, t=t_lat)
    s_ctx = _s5_states(x_ctx, mods_ctx, wbc, t=x_ctx.shape[1])
    a4 = a.reshape(SSM_NQ, SSM_QG, 2, 2, LANES)[..., :SSM_STATE]
    a_rows = jnp.repeat(a4.transpose(2, 0, 3, 1, 4).reshape(2, -1), bsz, axis=0)
    a_rows = a_rows.reshape(SSM_ROWS, SSM_NQ * SSM_VR, LANES).transpose(1, 0, 2)
    s_in = _s5_scan(s_ctx, s_lat, a_rows)
    return _s5_out(x_lat, mods_lat, s_in, bd, wcc, d_skip.reshape(1, -1), t=t_lat)


def _gmlp_kernel(x_ref, mod_ref, win_ref, bin_ref, lg_ref, lb_ref, ws_ref, bs_ref, wout_ref, g_ref, b_ref, o_ref,
                 v_ref, gated_ref):
    m = mod_ref[0]
    x = x_ref[0]
    t = x.shape[0]
    hw = GMLP_HEAD_DIM
    h = (x * (1.0 + m[1:2]) + m[0:1]).astype(BF16)

    def z_cols(c0):
        return jax.nn.gelu(_dot(h, win_ref[:, c0:c0 + hw]) + bin_ref[:, c0:c0 + hw])

    total = jnp.zeros((t, 1), F32)
    for hd in range(GMLP_HEADS):
        zc = z_cols(GMLP_HALF + hd * hw)
        v_ref[:, hd * hw:(hd + 1) * hw] = zc
        total = total + jnp.sum(zc, axis=-1, keepdims=True)
    mu = total / GMLP_HALF
    u_next = z_cols(0)
    sq = jnp.zeros((t, 1), F32)
    for hd in range(GMLP_HEADS):
        vc = v_ref[:, hd * hw:(hd + 1) * hw] - mu
        sq = sq + jnp.sum(vc * vc, axis=-1, keepdims=True)
    rstd = lax.rsqrt(sq / GMLP_HALF + LN_EPS)
    bs = bs_ref[...]
    for hd in range(GMLP_HEADS):
        cols = slice(hd * hw, (hd + 1) * hw)
        u = u_next
        if hd + 1 < GMLP_HEADS:
            u_next = z_cols((hd + 1) * hw)
        vn = ((v_ref[:, cols] - mu) * rstd * lg_ref[:, cols] + lb_ref[:, cols]).astype(BF16)
        for c in range(t // GMLP_CHUNK):
            rows = slice(c * GMLP_CHUNK, (c + 1) * GMLP_CHUNK)
            gate = _dot(ws_ref[hd], vn[rows]) + bs[:, hd:hd + 1]
            gated_ref[rows, cols] = (u[rows] * gate).astype(BF16)
    chunks = _row_chunks(t)
    ys = [_dot(gated_ref[rs, :], wout_ref[...]) for rs in chunks]
    for rs, y in zip(chunks, ys):
        o_ref[0, rs, :] = _post_norm(x_ref[0, rs, :], y, m[2:3], g_ref[...], b_ref[...])


def _gmlp(x, mods, w_in, b_in, lg, lb, w_s, b_s_t, w_out, ln_g, ln_b, *, t):
    bsz, n, d = x.shape
    tile = pl.BlockSpec((1, t, d), lambda b, i: (b, i, 0))
    const2 = lambda a: _const_spec(a.shape)
    return pl.pallas_call(
        _gmlp_kernel,
        out_shape=jax.ShapeDtypeStruct((bsz, n, d), F32),
        grid=(bsz, n // t),
        in_specs=[tile, pl.BlockSpec((1, N_MODS, d), lambda b, i: (b, 0, 0)),
                  const2(w_in), const2(b_in), const2(lg), const2(lb), const2(w_s), const2(b_s_t), const2(w_out),
                  const2(ln_g), const2(ln_b)],
        out_specs=tile,
        scratch_shapes=[pltpu.VMEM((t, GMLP_HALF), F32), pltpu.VMEM((t, GMLP_HALF), BF16)],
        compiler_params=_cparams("arbitrary", "arbitrary"),
        name="gmlp_mixer",
    )(x, mods, w_in, b_in, lg, lb, w_s, b_s_t, w_out, ln_g, ln_b)


def _tile(n, pref):
    return pref if n % pref == 0 else n


def _layer(layer, x_lat, x_ctx, mods, p):
    bsz, _, d = x_lat.shape
    lctx = x_ctx.shape[1]
    row = lambda v: v.reshape(1, -1)
    t_lat, t_ctx = 1024, _tile(lctx, 256)
    kind = MIXERS[layer % len(MIXERS)]
    j = layer // len(MIXERS)
    ctx_out = any(MIXERS[m % len(MIXERS)] in CTX_READING_MIXERS for m in range(layer + 1, DEPTH))
    m_lat = mods[layer, :bsz]
    m_ctx = jnp.broadcast_to(mods[layer, bsz:bsz + 1], (bsz, N_MODS, d))
    g1, b1 = row(p["ln1_g"][layer]), row(p["ln1_b"][layer])

    def ffn(xs, ms, t, seg=None):
        return _ffn(xs, ms, p["ffn_w_up_bf16"], p["ffn_conv_w"][layer], row(p["ffn_conv_b"][layer]),
                    p["ffn_w_down_bf16"], row(p["ln2_g"][layer]), row(p["ln2_b"][layer]),
                    layer=layer, t=t, cc=FFN_CHUNK, seg=seg)

    if kind == "pool":
        args = (p["pool_w"][j].astype(BF16), row(p["pool_b"][j]), row(p["pool_scale"][j]), g1, b1)
        x_lat = _pool(x_lat, m_lat, *args, t=t_lat)
        if ctx_out:
            x_ctx = _pool(x_ctx, m_ctx, *args, t=t_ctx)
    elif kind == "attn":
        wqkv = p["attn_w_qkv"][j].astype(BF16)
        wo = p["attn_w_o"][j].astype(BF16)
        sink = p["attn_sink"][j].astype(F32)
        q, kd, vd = _qkv(x_lat, m_lat, wqkv, t=t_lat, rope=True)
        qc, kdc, vdc = _qkv(x_ctx, m_ctx, wqkv, t=t_ctx, rope=False)
        o_lat = _attn(sink, q, kd, vd, kdc, vdc, local=True, qb=ATTN_QBLOCKS)
        x_lat = _proj_norm(o_lat, x_lat, m_lat, wo, g1, b1, t=t_lat)
        if ctx_out:
            o_ctx = _attn(sink, qc, None, None, kdc, vdc, local=False, qb=ATTN_QBLOCKS)
            x_ctx = _proj_norm(o_ctx, x_ctx, m_ctx, wo, g1, b1, t=t_ctx)
    elif kind == "ssm":
        assert not ctx_out
        y = _s5_mixer(x_lat, x_ctx, m_lat, m_ctx, p["ssm_lambda_re"][j], p["ssm_lambda_im"][j], p["ssm_log_dt"][j],
                       p["ssm_b_re"][j], p["ssm_b_im"][j], p["ssm_c_re"][j], p["ssm_c_im"][j], p["ssm_d"][j])
        x_lat = _glu_norm(y, x_lat, m_lat, p["ssm_w_glu_a"][j].astype(BF16), p["ssm_w_glu_b"][j].astype(BF16),
                          g1, b1, t=t_lat)
    else:
        assert not ctx_out
        x_lat = _gmlp(x_lat, m_lat, p["gmlp_w_in"][j].astype(BF16), row(p["gmlp_b_in"][j]), row(p["gmlp_ln_g"][j]),
                      row(p["gmlp_ln_b"][j]), p["gmlp_w_s"][j].astype(BF16), p["gmlp_b_s"][j].T,
                      p["gmlp_w_out"][j].astype(BF16), g1, b1, t=t_lat)
    x_lat = ffn(x_lat, m_lat, _tile(x_lat.shape[1], 1024))
    if ctx_out:
        x_ctx = ffn(x_ctx.reshape(1, bsz * lctx, d), m_ctx[:1], bsz * lctx, seg=lctx).reshape(bsz, lctx, d)
    return x_lat, x_ctx


def _mods(c, c_ctx, ada_w, ada_b):
    bsz, d = c.shape
    cond = jnp.concatenate([c, c_ctx[None, :], jnp.zeros((8 - bsz - 1, d), F32)], axis=0)
    return _ada(cond, ada_w, ada_b).reshape(DEPTH, 8, N_MODS, d)


def kernel(x, c, ctx, c_ctx, ada_w, ada_b, ln1_g, ln1_b, ln2_g, ln2_b, ffn_w_up, ffn_conv_w, ffn_conv_b, ffn_w_down, pool_w, pool_b, pool_scale, attn_w_qkv, attn_w_o, attn_sink, ssm_lambda_re, ssm_lambda_im, ssm_log_dt, ssm_b_re, ssm_b_im, ssm_c_re, ssm_c_im, ssm_d, ssm_w_glu_a, ssm_w_glu_b, gmlp_w_in, gmlp_b_in, gmlp_ln_g, gmlp_ln_b, gmlp_w_s, gmlp_b_s, gmlp_w_out):
    bsz, n, d = x.shape
    assert d == D_MODEL and bsz < 8 and n % 512 == 0 and ctx.shape[1] % ATTN_BLOCK == 0
    p = dict(ln1_g=ln1_g, ln1_b=ln1_b, ln2_g=ln2_g, ln2_b=ln2_b, ffn_w_up=ffn_w_up, ffn_conv_w=ffn_conv_w,
             ffn_conv_b=ffn_conv_b, ffn_w_down=ffn_w_down, pool_w=pool_w, pool_b=pool_b, pool_scale=pool_scale,
             attn_w_qkv=attn_w_qkv, attn_w_o=attn_w_o, attn_sink=attn_sink, ssm_lambda_re=ssm_lambda_re,
             ssm_lambda_im=ssm_lambda_im, ssm_log_dt=ssm_log_dt, ssm_b_re=ssm_b_re, ssm_b_im=ssm_b_im,
             ssm_c_re=ssm_c_re, ssm_c_im=ssm_c_im, ssm_d=ssm_d, ssm_w_glu_a=ssm_w_glu_a, ssm_w_glu_b=ssm_w_glu_b,
             gmlp_w_in=gmlp_w_in, gmlp_b_in=gmlp_b_in, gmlp_ln_g=gmlp_ln_g, gmlp_ln_b=gmlp_ln_b, gmlp_w_s=gmlp_w_s,
             gmlp_b_s=gmlp_b_s, gmlp_w_out=gmlp_w_out)
    p["ffn_w_up_bf16"] = ffn_w_up.astype(BF16)
    p["ffn_w_down_bf16"] = ffn_w_down.astype(BF16)
    mods = _mods(c, c_ctx, ada_w, ada_b)
    x_lat, x_ctx = x, ctx
    for layer in range(DEPTH):
        x_lat, x_ctx = _layer(layer, x_lat, x_ctx, mods, p)
    return x_lat
```

```python
import functools
import math

import jax
import jax.numpy as jnp
from jax import lax
from jax.experimental import pallas as pl
from jax.experimental.pallas import tpu as pltpu

F32 = jnp.float32
BF16 = jnp.bfloat16

D_MODEL = 1024
DEPTH = 4
MIXERS = ("pool", "attn", "ssm", "gmlp")
CTX_READING_MIXERS = ("attn", "ssm")
GRID_W = 64
N_MODS = 6
DEEPNORM_ALPHA = (2.0 * DEPTH) ** 0.25
LN_EPS = 1e-5

POOL_WINDOWS = (2, 4, 8, 16)
POOL_GROUP = D_MODEL // len(POOL_WINDOWS)

HEAD_DIM = 64
N_Q_HEADS = D_MODEL // HEAD_DIM
N_KV_HEADS = N_Q_HEADS // 4
Q_WIDTH = N_Q_HEADS * HEAD_DIM
KV_WIDTH = N_KV_HEADS * HEAD_DIM
WINDOW = 128
ATTN_BLOCK = 128
ATTN_QBLOCKS = 8
ROPE_BASE = 10000.0
NEG_INF = -1e30
LOG2E = math.log2(math.e)

SSM_GROUP = 16
SSM_N_GROUPS = D_MODEL // SSM_GROUP
SSM_STATE = 64
SSM_CHUNK = 16

GMLP_CHUNK = 128
GMLP_HALF = 2 * D_MODEL
GMLP_HEADS = 8
GMLP_HEAD_DIM = GMLP_HALF // GMLP_HEADS

FFN_HIDDEN = 2816
FFN_CHUNK = 256
OUT_ROWS = 256

LANES = 128
HALO = 16
VMEM_LIMIT = 56 * 1024 * 1024


def _cparams(*sem):
    return pltpu.CompilerParams(dimension_semantics=sem, vmem_limit_bytes=VMEM_LIMIT)


def _const_spec(shape):
    nd = len(shape)
    return pl.BlockSpec(shape, lambda *_: (0,) * nd, pipeline_mode=pl.Buffered(1))


def _post_norm(x, y, gate, g, b):
    z = DEEPNORM_ALPHA * x + gate * y
    mu = jnp.mean(z, axis=-1, keepdims=True)
    zc = z - mu
    var = jnp.mean(zc * zc, axis=-1, keepdims=True)
    return zc * lax.rsqrt(var + LN_EPS) * g + b


def _row_chunks(t):
    return [slice(r, min(r + OUT_ROWS, t)) for r in range(0, t, OUT_ROWS)]


def _dot(a, b):
    return jnp.dot(a, b, preferred_element_type=F32)


def _dot_nt(a, b):
    return lax.dot_general(a, b, (((1,), (1,)), ((), ())), preferred_element_type=F32)


def _ada_kernel(c_ref, w_ref, b_ref, o_ref):
    c = c_ref[...]
    s = (c * jax.nn.sigmoid(c)).astype(BF16)
    o_ref[0] = _dot(s, w_ref[0].astype(BF16)) + b_ref[0]


def _ada(cond, ada_w, ada_b):
    depth, d, n = ada_w.shape
    rows = cond.shape[0]
    tn = 1536
    return pl.pallas_call(
        _ada_kernel,
        out_shape=jax.ShapeDtypeStruct((depth, rows, n), F32),
        grid=(depth, n // tn),
        in_specs=[pl.BlockSpec((rows, d), lambda l, j: (0, 0)),
                  pl.BlockSpec((1, d, tn), lambda l, j: (l, 0, j)),
                  pl.BlockSpec((1, 1, tn), lambda l, j: (l, 0, j))],
        out_specs=pl.BlockSpec((1, rows, tn), lambda l, j: (l, 0, j)),
        compiler_params=_cparams("arbitrary", "arbitrary"),
        name="ada",
    )(cond, ada_w, ada_b.reshape(depth, 1, n))


def _halo_specs(t, n):
    per = t // HALO
    last = n // HALO - 1
    prev = pl.BlockSpec((1, HALO, D_MODEL), lambda b, i, *_: (b, jnp.maximum(i * per - 1, 0), 0))
    nxt = pl.BlockSpec((1, HALO, D_MODEL), lambda b, i, *_: (b, jnp.minimum((i + 1) * per, last), 0))
    return prev, nxt


def _ffn_kernel(xp_ref, x_ref, xn_ref, mod_ref, wu_ref, cw_ref, cb_ref, wd_ref, g_ref, b_ref, o_ref, act_ref,
                *, t, nt, f, cc, seg):
    i = pl.program_id(1)
    rows = t + 2 * HALO
    m = mod_ref[0]
    sh, sc = m[3:4], 1.0 + m[4:5]
    keep_p = jnp.where(i > 0, 1.0, 0.0)
    keep_n = jnp.where(i < nt - 1, 1.0, 0.0)
    x = x_ref[0]
    h = jnp.concatenate([((xp_ref[0] * sc + sh) * keep_p).astype(BF16), (x * sc + sh).astype(BF16),
                         ((xn_ref[0] * sc + sh) * keep_n).astype(BF16)], axis=0)

    if seg:
        p = lax.broadcasted_iota(jnp.int32, (rows, 1), 0) - HALO
        seq_first, seq_last = p % seg == 0, p % seg == seg - 1

    def conv(off):
        u = _dot(h, wu_ref[:, off:off + cc])
        cw = cw_ref[:, off:off + cc]
        before, after = pltpu.roll(u, 1, 0), pltpu.roll(u, rows - 1, 0)
        if seg:
            before, after = jnp.where(seq_first, 0.0, before), jnp.where(seq_last, 0.0, after)
        a = cb_ref[:, off:off + cc] + before * cw[0:1]
        a = a + u * cw[1:2]
        a = a + after * cw[2:3]
        return a[HALO:HALO + t]

    for c in range(f // cc):
        val = conv(c * cc)
        gate = conv(f + c * cc)
        act_ref[:, c * cc:(c + 1) * cc] = (val * (gate * jax.nn.sigmoid(gate))).astype(BF16)
    chunks = _row_chunks(t)
    ys = [_dot(act_ref[rs, :], wd_ref[...]) for rs in chunks]
    for rs, y in zip(chunks, ys):
        o_ref[0, rs, :] = _post_norm(x_ref[0, rs, :], y, m[5:6], g_ref[...], b_ref[...])


def _layer_spec(shape, layer):
    nd = len(shape) - 1
    return pl.BlockSpec((None,) + tuple(shape[1:]), lambda *_: (layer,) + (0,) * nd, pipeline_mode=pl.Buffered(1))


def _ffn(x, mods, w_up, conv_w, conv_b, w_down, ln_g, ln_b, *, layer, t, cc, seg=None):
    bsz, n, d = x.shape
    f = w_down.shape[1]
    nt = n // t
    assert seg is None or (nt == 1 and t % seg == 0)
    prev, nxt = _halo_specs(t, n)
    kern = functools.partial(_ffn_kernel, t=t, nt=nt, f=f, cc=cc, seg=seg)
    return pl.pallas_call(
        kern,
        out_shape=jax.ShapeDtypeStruct((bsz, n, d), F32),
        grid=(bsz, nt),
        in_specs=[prev,
                  pl.BlockSpec((1, t, d), lambda b, i: (b, i, 0)),
                  nxt,
                  pl.BlockSpec((1, N_MODS, d), lambda b, i: (b, 0, 0)),
                  _layer_spec(w_up.shape, layer), _const_spec(conv_w.shape), _const_spec(conv_b.shape),
                  _layer_spec(w_down.shape, layer), _const_spec(ln_g.shape), _const_spec(ln_b.shape)],
        out_specs=pl.BlockSpec((1, t, d), lambda b, i: (b, i, 0)),
        scratch_shapes=[pltpu.VMEM((t, f), BF16)],
        compiler_params=pltpu.CompilerParams(
            dimension_semantics=("arbitrary", "arbitrary"), vmem_limit_bytes=VMEM_LIMIT,
            allow_input_fusion=[False, False, False, False, True, False, False, True, False, False]),
        name="conv_ffn",
    )(x, x, x, mods, w_up, conv_w, conv_b, w_down, ln_g, ln_b)


def _pool_kernel(xp_ref, x_ref, xn_ref, mod_ref, w_ref, pb_ref, ps_ref, g_ref, b_ref, o_ref, *, t, nt, n):
    i = pl.program_id(1)
    rows = t + 2 * HALO
    m = mod_ref[0]
    sh, sc = m[0:1], 1.0 + m[1:2]
    keep_p = jnp.where(i > 0, 1.0, 0.0)
    keep_n = jnp.where(i < nt - 1, 1.0, 0.0)
    x = x_ref[0]
    h = jnp.concatenate([(xp_ref[0] * sc + sh) * keep_p, x * sc + sh, (xn_ref[0] * sc + sh) * keep_n], axis=0)
    pos = i * t + lax.broadcasted_iota(jnp.int32, (t, 1), 0)
    outs = []
    for gi, win in enumerate(POOL_WINDOWS):
        hg = h[:, gi * POOL_GROUP:(gi + 1) * POOL_GROUP]
        s = hg + pltpu.roll(hg, 1, 0)
        span = 2
        while span < win:
            s = s + pltpu.roll(s, span, 0)
            span *= 2
        if win > 2:
            s = pltpu.roll(s, rows - (win // 2 - 1), 0)
        lo = jnp.maximum(pos - win // 2, 0)
        hi = jnp.minimum(pos - win // 2 + win, n)
        mean = s[HALO:HALO + t] / (hi - lo).astype(F32)
        mixed = (mean - hg[HALO:HALO + t]).astype(BF16)
        outs.append(_dot(mixed, w_ref[gi]))
    y = (jnp.concatenate(outs, axis=1) + pb_ref[...]) * ps_ref[...]
    o_ref[0] = _post_norm(x, y, m[2:3], g_ref[...], b_ref[...])


def _pool(x, mods, w, pb, ps, ln_g, ln_b, *, t):
    bsz, n, d = x.shape
    nt = n // t
    prev, nxt = _halo_specs(t, n)
    kern = functools.partial(_pool_kernel, t=t, nt=nt, n=n)
    vec = pl.BlockSpec((1, d), lambda b, i: (0, 0))
    return pl.pallas_call(
        kern,
        out_shape=jax.ShapeDtypeStruct((bsz, n, d), F32),
        grid=(bsz, nt),
        in_specs=[prev, pl.BlockSpec((1, t, d), lambda b, i: (b, i, 0)), nxt,
                  pl.BlockSpec((1, N_MODS, d), lambda b, i: (b, 0, 0)),
                  pl.BlockSpec(w.shape, lambda b, i: (0, 0, 0)),
                  vec, vec, vec, vec],
        out_specs=pl.BlockSpec((1, t, d), lambda b, i: (b, i, 0)),
        compiler_params=_cparams("arbitrary", "arbitrary"),
        name="pool_mixer",
    )(x, x, x, mods, w, pb, ps, ln_g, ln_b)


def _rope_tables(n):
    tpos = jnp.arange(n)
    half = HEAD_DIM // 4
    freqs = ROPE_BASE ** (-jnp.arange(half, dtype=F32) / half)
    ang_r = (tpos // GRID_W).astype(F32)[:, None] * freqs[None, :]
    ang_c = (tpos % GRID_W).astype(F32)[:, None] * freqs[None, :]
    zero = jnp.zeros_like(ang_r)
    cr, sr, cc, sc = jnp.cos(ang_r), jnp.sin(ang_r), jnp.cos(ang_c), jnp.sin(ang_c)
    cos = jnp.tile(jnp.concatenate([cr, cr, cc, cc], axis=1), (1, 2))
    sin_first = jnp.tile(jnp.concatenate([-sr, zero, -sc, zero], axis=1), (1, 2))
    sin_second = jnp.tile(jnp.concatenate([zero, sr, zero, sc], axis=1), (1, 2))
    return cos, sin_first, sin_second


def _dup_heads(chunk, lo):
    sw = pltpu.roll(chunk, HEAD_DIM, 1)
    return jnp.where(lo, chunk, sw), jnp.where(lo, sw, chunk)


def _qkv_kernel(*refs, rope):
    if rope:
        x_ref, mod_ref, w_ref, cos_ref, sa_ref, sb_ref, q_ref, kd_ref, vt_ref = refs
    else:
        x_ref, mod_ref, w_ref, q_ref, kd_ref, vt_ref = refs
    m = mod_ref[0]
    h = (x_ref[0] * (1.0 + m[1:2]) + m[0:1]).astype(BF16)
    qkv = _dot(h, w_ref[...])
    t = qkv.shape[0]
    lo = lax.broadcasted_iota(jnp.int32, (t, LANES), 1) < HEAD_DIM

    def proj(col):
        return qkv[:, col:col + LANES]

    scale = HEAD_DIM ** -0.5 * LOG2E

    def rot(v):
        if not rope:
            return v
        quarter = HEAD_DIM // 4
        return (v * cos_ref[...] + pltpu.roll(v, LANES - quarter, 1) * sa_ref[...]
                + pltpu.roll(v, quarter, 1) * sb_ref[...])

    for c in range(Q_WIDTH // LANES):
        q_ref[0, :, c * LANES:(c + 1) * LANES] = (rot(proj(c * LANES)) * scale).astype(BF16)
    for c in range(KV_WIDTH // LANES):
        k0, k1 = _dup_heads(rot(proj(Q_WIDTH + c * LANES)), lo)
        kd_ref[0, :, (2 * c) * LANES:(2 * c + 1) * LANES] = k0.astype(BF16)
        kd_ref[0, :, (2 * c + 1) * LANES:(2 * c + 2) * LANES] = k1.astype(BF16)
        v = proj(Q_WIDTH + KV_WIDTH + c * LANES)
        vt_ref[0, c * LANES:(c + 1) * LANES, :] = jnp.transpose(v).astype(BF16)


def _qkv(x, mods, w_qkv, *, t, rope):
    bsz, n, d = x.shape
    kdw = N_KV_HEADS * LANES
    ins = [x, mods, w_qkv]
    specs = [pl.BlockSpec((1, t, d), lambda b, i: (b, i, 0)),
             pl.BlockSpec((1, N_MODS, d), lambda b, i: (b, 0, 0)),
             pl.BlockSpec(w_qkv.shape, lambda b, i: (0, 0))]
    if rope:
        ins += list(_rope_tables(n))
        specs += [pl.BlockSpec((t, LANES), lambda b, i: (i, 0))] * 3
    return pl.pallas_call(
        functools.partial(_qkv_kernel, rope=rope),
        out_shape=(jax.ShapeDtypeStruct((bsz, n, Q_WIDTH), BF16),
                   jax.ShapeDtypeStruct((bsz, n, kdw), BF16),
                   jax.ShapeDtypeStruct((bsz, KV_WIDTH, n), BF16)),
        grid=(bsz, n // t),
        in_specs=specs,
        out_specs=(pl.BlockSpec((1, t, Q_WIDTH), lambda b, i: (b, i, 0)),
                   pl.BlockSpec((1, t, kdw), lambda b, i: (b, i, 0)),
                   pl.BlockSpec((1, KV_WIDTH, t), lambda b, i: (b, 0, i))),
        compiler_params=_cparams("arbitrary", "arbitrary"),
        name="qkv_rope" if rope else "qkv_ctx",
    )(*ins)


def _attn_kernel(sink_ref, q_ref, *refs, nsteps, qb, local):
    blk = ATTN_BLOCK
    if local:
        kp_ref, kc_ref, kn_ref, kx_ref, vp_ref, vc_ref, vn_ref, vx_ref, o_ref = refs
        k_band = [kp_ref[0]] + [kc_ref[0, j * blk:(j + 1) * blk, :] for j in range(qb)] + [kn_ref[0]]
        v_band = [vp_ref[0]] + [vc_ref[0, :, j * blk:(j + 1) * blk] for j in range(qb)] + [vn_ref[0]]
    else:
        kx_ref, vx_ref, o_ref = refs
    group = N_Q_HEADS // N_KV_HEADS
    cols = group * blk
    step = pl.program_id(1)
    if local:
        kj = lax.broadcasted_iota(jnp.int32, (blk, cols), 0)
        qi = lax.broadcasted_iota(jnp.int32, (blk, cols), 1) % blk
        behind, ahead = kj >= qi, kj <= qi
    lo = lax.broadcasted_iota(jnp.int32, (blk, LANES), 1) < HEAD_DIM
    head_of_col = lax.broadcasted_iota(jnp.int32, (1, cols), 1) // blk
    units = [(j, hk) for j in range(qb) for hk in range(N_KV_HEADS)]

    scores = []
    for j, hk in units:
        ks = (k_band[j:j + 3] if local else []) + [kx_ref[0]]
        kx = jnp.concatenate([k[:, hk * LANES:(hk + 1) * LANES] for k in ks], axis=0)
        parts = []
        for c in range(group // 2):
            qc = q_ref[0, j * blk:(j + 1) * blk, (hk * group // 2 + c) * LANES:(hk * group // 2 + c + 1) * LANES]
            zero = jnp.zeros_like(qc)
            parts += [jnp.where(lo, qc, zero), jnp.where(lo, zero, qc)]
        q4 = jnp.concatenate(parts, axis=0)
        scores.append(_dot_nt(kx, q4))
    probs = []
    for (j, hk), s in zip(units, scores):
        if local:
            valid_prev = behind & (step > 0) if j == 0 else behind
            valid_next = ahead & (step < nsteps - 1) if j == qb - 1 else ahead
            s = jnp.concatenate([jnp.where(valid_prev, s[:blk], NEG_INF), s[blk:2 * blk],
                                 jnp.where(valid_next, s[2 * blk:3 * blk], NEG_INF), s[3 * blk:]], axis=0)
        sink = jnp.zeros((1, cols), F32)
        for g in range(group):
            sink = jnp.where(head_of_col == g, sink_ref[hk * group + g] * LOG2E, sink)
        mx = jnp.maximum(jnp.max(s, axis=0, keepdims=True), sink)
        p = jnp.exp2(s - mx)
        den = jnp.sum(p, axis=0, keepdims=True) + jnp.exp2(sink - mx)
        probs.append((p.astype(BF16), den))
    for (j, hk), (p, den) in zip(units, probs):
        vs = (v_band[j:j + 3] if local else []) + [vx_ref[0]]
        vt = jnp.concatenate([v[hk * HEAD_DIM:(hk + 1) * HEAD_DIM, :] for v in vs], axis=1)
        ot = _dot(vt, p) / den
        for c in range(group // 2):
            pair = jnp.concatenate([ot[:, (2 * c) * blk:(2 * c + 1) * blk],
                                    ot[:, (2 * c + 1) * blk:(2 * c + 2) * blk]], axis=0)
            o_ref[0, j * blk:(j + 1) * blk, (hk * group // 2 + c) * LANES:(hk * group // 2 + c + 1) * LANES] = (
                jnp.transpose(pair).astype(BF16))


def _attn(sink, q, kd, vt, kd_ctx, vt_ctx, *, local, qb):
    bsz, n, _ = q.shape
    blk = ATTN_BLOCK
    assert WINDOW == blk, "the band is exactly the previous, own and next key block"
    nb = n // blk
    qb = math.gcd(qb, nb)
    nsteps = nb // qb
    lctx = kd_ctx.shape[1]
    kdw = kd_ctx.shape[2]
    smem = pl.BlockSpec(memory_space=pltpu.SMEM)
    qspec = pl.BlockSpec((1, qb * blk, Q_WIDTH), lambda b, i: (b, i, 0))
    kctx_spec = pl.BlockSpec((1, lctx, kdw), lambda b, i: (b, 0, 0))
    vctx_spec = pl.BlockSpec((1, KV_WIDTH, lctx), lambda b, i: (b, 0, 0))
    if local:
        prev, nxt = (lambda i: jnp.maximum(i * qb - 1, 0)), (lambda i: jnp.minimum((i + 1) * qb, nb - 1))
        kband = [pl.BlockSpec((1, blk, kdw), lambda b, i: (b, prev(i), 0)),
                 pl.BlockSpec((1, qb * blk, kdw), lambda b, i: (b, i, 0)),
                 pl.BlockSpec((1, blk, kdw), lambda b, i: (b, nxt(i), 0))]
        vband = [pl.BlockSpec((1, KV_WIDTH, blk), lambda b, i: (b, 0, prev(i))),
                 pl.BlockSpec((1, KV_WIDTH, qb * blk), lambda b, i: (b, 0, i)),
                 pl.BlockSpec((1, KV_WIDTH, blk), lambda b, i: (b, 0, nxt(i)))]
        specs = [smem, qspec] + kband + [kctx_spec] + vband + [vctx_spec]
        args = (sink, q, kd, kd, kd, kd_ctx, vt, vt, vt, vt_ctx)
    else:
        specs = [smem, qspec, kctx_spec, vctx_spec]
        args = (sink, q, kd_ctx, vt_ctx)
    return pl.pallas_call(
        functools.partial(_attn_kernel, nsteps=nsteps, qb=qb, local=local),
        out_shape=jax.ShapeDtypeStruct((bsz, n, Q_WIDTH), BF16),
        grid=(bsz, nsteps),
        in_specs=specs,
        out_specs=pl.BlockSpec((1, qb * blk, Q_WIDTH), lambda b, i: (b, i, 0)),
        compiler_params=_cparams("arbitrary", "arbitrary"),
        name="banded_attn" if local else "ctx_attn",
    )(*args)


def _proj_norm_kernel(a_ref, x_ref, mod_ref, w_ref, g_ref, b_ref, o_ref):
    chunks = _row_chunks(x_ref.shape[1])
    ys = [_dot(a_ref[0, rs, :], w_ref[...]) for rs in chunks]
    for rs, y in zip(chunks, ys):
        o_ref[0, rs, :] = _post_norm(x_ref[0, rs, :], y, mod_ref[0][2:3], g_ref[...], b_ref[...])


def _proj_norm(a, x, mods, w, ln_g, ln_b, *, t):
    bsz, n, d = x.shape
    ka = a.shape[2]
    vec = pl.BlockSpec((1, d), lambda b, i: (0, 0))
    return pl.pallas_call(
        _proj_norm_kernel,
        out_shape=jax.ShapeDtypeStruct((bsz, n, d), F32),
        grid=(bsz, n // t),
        in_specs=[pl.BlockSpec((1, t, ka), lambda b, i: (b, i, 0)),
                  pl.BlockSpec((1, t, d), lambda b, i: (b, i, 0)),
                  pl.BlockSpec((1, N_MODS, d), lambda b, i: (b, 0, 0)),
                  pl.BlockSpec(w.shape, lambda b, i: (0, 0)),
                  vec, vec],
        out_specs=pl.BlockSpec((1, t, d), lambda b, i: (b, i, 0)),
        compiler_params=_cparams("arbitrary", "arbitrary"),
        name="attn_out_norm",
    )(a, x, mods, w, ln_g, ln_b)


def _glu_norm_kernel(y_ref, x_ref, mod_ref, wa_ref, wb_ref, g_ref, b_ref, o_ref):
    gl = jax.nn.gelu(y_ref[0]).astype(BF16)
    out = _dot(gl, wa_ref[...]) * jax.nn.sigmoid(_dot(gl, wb_ref[...]))
    o_ref[0] = _post_norm(x_ref[0], out, mod_ref[0][2:3], g_ref[...], b_ref[...])


def _glu_norm(y, x, mods, wa, wb, ln_g, ln_b, *, t):
    bsz, n, d = x.shape
    tile = pl.BlockSpec((1, t, d), lambda b, i: (b, i, 0))
    vec = pl.BlockSpec((1, d), lambda b, i: (0, 0))
    wspec = pl.BlockSpec((d, d), lambda b, i: (0, 0))
    return pl.pallas_call(
        _glu_norm_kernel,
        out_shape=jax.ShapeDtypeStruct((bsz, n, d), F32),
        grid=(bsz, n // t),
        in_specs=[tile, tile, pl.BlockSpec((1, N_MODS, d), lambda b, i: (b, 0, 0)), wspec, wspec, vec, vec],
        out_specs=tile,
        compiler_params=_cparams("arbitrary", "arbitrary"),
        name="ssm_glu_norm",
    )(y, x, mods, wa, wb, ln_g, ln_b)


SSM_QG = LANES // SSM_GROUP
SSM_NQ = D_MODEL // LANES
SSM_ROWS = 8
SSM_W = SSM_CHUNK * LANES
SSM_HALF = SSM_QG * SSM_STATE
SSM_TAP = SSM_CHUNK * SSM_GROUP
SSM_VR = 2 * SSM_HALF // LANES


def _s5_prep_kernel(lr_ref, li_ref, ldt_ref, btr_ref, bti_ref, cr_ref, ci_ref, wb_ref, wc_ref, bd_ref, a_ref):
    c = SSM_CHUNK
    q = pl.program_id(0)
    lag = lax.broadcasted_iota(jnp.int32, (c + 1, LANES), 0).astype(F32)
    lane = lax.broadcasted_iota(jnp.int32, (1, LANES), 1)
    taps = [[[] for _ in range(c)] for _ in range(2)]
    for g8 in range(SSM_QG):
        own = jnp.where((lane < SSM_STATE) == (g8 % 2 == 0), 1.0, 0.0)
        wb_cols, wc_cols, a_cols = [], [], []
        for d in range(2):
            lr, li = lr_ref[g8, d], li_ref[g8, d]
            dt = jnp.exp(jnp.full((1, LANES), ldt_ref[q * SSM_QG + g8, d], F32))
            mag = jnp.exp(lag * (lr * dt))
            ang = lag * (li * dt)
            pw_r, pw_i = mag * jnp.cos(ang), mag * jnp.sin(ang)
            lbr, lbi = pw_r[1:2], pw_i[1:2]
            den = lr * lr + li * li
            qr = ((lbr - 1.0) * lr + lbi * li) / den
            qi = (lbi * lr - (lbr - 1.0) * li) / den
            btr, bti = btr_ref[g8, d], bti_ref[g8, d]
            bbr = qr * btr - qi * bti
            bbi = qr * bti + qi * btr
            cr, ci = cr_ref[g8, d], ci_ref[g8, d]

            def cl(j):
                return cr * pw_r[j:j + 1] - ci * pw_i[j:j + 1], -(cr * pw_i[j:j + 1] + ci * pw_r[j:j + 1])

            def bl(j):
                return bbr * pw_r[j:j + 1] - bbi * pw_i[j:j + 1], bbr * pw_i[j:j + 1] + bbi * pw_r[j:j + 1]

            e = jnp.concatenate([jnp.concatenate(cl(j), axis=1) for j in range(c)], axis=0)
            pieces = [jnp.concatenate([bbr * own, bbi * own], axis=1)]
            if g8 > 0:
                pieces.insert(0, jnp.zeros((g8 * SSM_GROUP, 2 * LANES), F32))
            if g8 < SSM_QG - 1:
                pieces.append(jnp.zeros(((SSM_QG - 1 - g8) * SSM_GROUP, 2 * LANES), F32))
            kt = lax.dot_general(e, jnp.concatenate(pieces, axis=0), (((1,), (1,)), ((), ())),
                                 preferred_element_type=F32, precision=lax.Precision.HIGHEST)
            for j in range(c):
                taps[d][j].append(kt[j * SSM_GROUP:(j + 1) * SSM_GROUP])
            wbl = [bl(c - 1 - k) if d == 0 else bl(k) for k in range(c)]
            wb_cols += [jnp.concatenate([w[0] for w in wbl], axis=0) * own,
                        jnp.concatenate([w[1] for w in wbl], axis=0) * own]
            wcl = [cl(k + 1) if d == 0 else cl(c - k) for k in range(c)]
            wc_cols += [jnp.concatenate([w[0] for w in wcl], axis=0) * own,
                        jnp.concatenate([w[1] for w in wcl], axis=0) * own]
            a_cols += [pw_r[c:c + 1], pw_i[c:c + 1]]
        wb_ref[g8] = jnp.concatenate(wb_cols, axis=1).astype(BF16)
        wc_ref[g8] = jnp.concatenate(wc_cols, axis=1).astype(BF16)
        a_ref[g8] = jnp.concatenate(a_cols, axis=1)
    for d in range(2):
        for j in range(c):
            blk = jnp.concatenate(taps[d][j], axis=0)
            if d == 0 and j == 0:
                blk = blk + jnp.concatenate(taps[1][0], axis=0)
            bd_ref[0, d, j] = blk.astype(BF16)


def _s5_prep(lam_re, lam_im, log_dt, b_re, b_im, c_re, c_im):
    g, nq, qg = SSM_N_GROUPS, SSM_NQ, SSM_QG
    per_g = lambda a: jnp.swapaxes(a, 0, 1)
    dup = lambda a: jnp.concatenate([a, a], axis=-1)
    lam_spec = pl.BlockSpec((qg, 2, 1, LANES), lambda i: (i, 0, 0, 0))
    mat_spec = pl.BlockSpec((qg, 2, SSM_GROUP, LANES), lambda i: (i, 0, 0, 0))
    w_spec = pl.BlockSpec((qg, SSM_TAP, 4 * LANES), lambda i: (i, 0, 0))
    return pl.pallas_call(
        _s5_prep_kernel,
        out_shape=(jax.ShapeDtypeStruct((g, SSM_TAP, 4 * LANES), BF16),
                   jax.ShapeDtypeStruct((g, SSM_TAP, 4 * LANES), BF16),
                   jax.ShapeDtypeStruct((nq, 2, SSM_CHUNK, LANES, LANES), BF16),
                   jax.ShapeDtypeStruct((g, 1, 4 * LANES), F32)),
        grid=(nq,),
        in_specs=[lam_spec, lam_spec, pl.BlockSpec(memory_space=pltpu.SMEM), mat_spec, mat_spec, mat_spec, mat_spec],
        out_specs=(w_spec, w_spec, pl.BlockSpec((1, 2, SSM_CHUNK, LANES, LANES), lambda i: (i, 0, 0, 0, 0)),
                   pl.BlockSpec((qg, 1, 4 * LANES), lambda i: (i, 0, 0))),
        compiler_params=_cparams("arbitrary"),
        name="s5_prep",
    )(dup(per_g(lam_re))[:, :, None, :], dup(per_g(lam_im))[:, :, None, :], per_g(log_dt),
      dup(jnp.swapaxes(per_g(b_re), 2, 3)), dup(jnp.swapaxes(per_g(b_im), 2, 3)), dup(per_g(c_re)), dup(per_g(c_im)))


def _s5_expand(dst_ref, src_ref):
    dst_ref[...] = jnp.zeros_like(dst_ref)
    for g8 in range(SSM_QG):
        for k in range(SSM_CHUNK):
            for c4 in range(4):
                r0 = k * LANES + g8 * SSM_GROUP
                c0 = c4 * SSM_HALF + (g8 // 2) * LANES
                dst_ref[r0:r0 + SSM_GROUP, c0:c0 + LANES] = src_ref[g8, k * SSM_GROUP:(k + 1) * SSM_GROUP,
                                                                    c4 * LANES:(c4 + 1) * LANES]


def _s5_modulate(x_ref, mod_ref, u_ref, bsz):
    for b in range(bsz):
        m = mod_ref[b]
        u_ref[b] = x_ref[b] * (1.0 + m[1:2]) + m[0:1]


def _s5_gather(u_ref, bsz, nchunk):
    rows = [jnp.concatenate([u_ref[b, pl.ds(pos, nchunk, stride=SSM_CHUNK), :] for pos in range(SSM_CHUNK)],
                            axis=1).astype(BF16) for b in range(bsz)]
    return jnp.concatenate(rows, axis=0)


def _s5_states_kernel(x_ref, mod_ref, wbc_ref, o_ref, wb_ref, u_ref, *, bsz, nchunk):
    @pl.when(pl.program_id(1) == 0)
    def _():
        _s5_expand(wb_ref, wbc_ref)

    _s5_modulate(x_ref, mod_ref, u_ref, bsz)
    s = _dot(_s5_gather(u_ref, bsz, nchunk), wb_ref[...])
    for b in range(bsz):
        for v in range(SSM_VR):
            for d in range(2):
                o_ref[v, pl.ds(d * bsz + b, nchunk, stride=SSM_ROWS), :] = (
                    s[b * nchunk:(b + 1) * nchunk, (d * SSM_VR + v) * LANES:(d * SSM_VR + v + 1) * LANES])


def _s5_states(x, mods, wbc, *, t):
    bsz, n, _ = x.shape
    nchunk = t // SSM_CHUNK
    return pl.pallas_call(
        functools.partial(_s5_states_kernel, bsz=bsz, nchunk=nchunk),
        out_shape=jax.ShapeDtypeStruct((SSM_NQ * SSM_VR, n // SSM_CHUNK * SSM_ROWS, LANES), F32),
        grid=(SSM_NQ, n // t),
        in_specs=[pl.BlockSpec((bsz, t, LANES), lambda q, i: (0, i, q)),
                  pl.BlockSpec((bsz, N_MODS, LANES), lambda q, i: (0, 0, q)),
                  pl.BlockSpec((SSM_QG, SSM_TAP, 4 * LANES), lambda q, i: (q, 0, 0))],
        out_specs=pl.BlockSpec((SSM_VR, nchunk * SSM_ROWS, LANES), lambda q, i: (q, i, 0)),
        scratch_shapes=[pltpu.VMEM((SSM_W, 4 * SSM_HALF), BF16), pltpu.VMEM((bsz, t, LANES), F32)],
        compiler_params=_cparams("arbitrary", "arbitrary"),
        name="s5_chunk_states",
    )(x, mods, wbc)


def _s5_scan_kernel(xc_ref, xl_ref, a_ref, o_ref):
    h = SSM_VR // 2
    ar, ai = a_ref[:h], a_ref[h:]
    nc, nl = xc_ref.shape[1] // SSM_ROWS, xl_ref.shape[1] // SSM_ROWS
    fwd_rows = lax.broadcasted_iota(jnp.int32, (SSM_VR, SSM_ROWS, LANES), 1) < SSM_ROWS // 2

    assert nl % 2 == 0
    chunk = lambda i: pl.ds(pl.multiple_of(i * SSM_ROWS, SSM_ROWS), SSM_ROWS)

    def step(ref, n, k, s):
        x = jnp.where(fwd_rows, ref[:, chunk(k), :], ref[:, chunk(n - 1 - k), :])
        return ar * s[0] - ai * s[1] + x[:h], ar * s[1] + ai * s[0] + x[h:]

    def first_touch(k, s):
        full = jnp.concatenate(s, axis=0)
        o_ref[:, chunk(k), :] = full
        o_ref[:, chunk(nl - 1 - k), :] = full
        return step(xl_ref, nl, k, s)

    def second_touch(k, s):
        full = jnp.concatenate(s, axis=0)
        o_ref[:, chunk(k), :] = jnp.where(fwd_rows, full, o_ref[:, chunk(k), :])
        o_ref[:, chunk(nl - 1 - k), :] = jnp.where(fwd_rows, o_ref[:, chunk(nl - 1 - k), :], full)
        return step(xl_ref, nl, k, s)

    zero = (jnp.zeros((h, SSM_ROWS, LANES), F32), jnp.zeros((h, SSM_ROWS, LANES), F32))
    s = lax.fori_loop(0, nc, lambda k, s: step(xc_ref, nc, k, s), zero)
    s = lax.fori_loop(0, nl // 2, first_touch, s)
    lax.fori_loop(nl // 2, nl, second_touch, s)


def _s5_scan(s_ctx, s_lat, a_rows):
    blk = lambda rows: pl.BlockSpec((SSM_VR, rows, LANES), lambda q: (q, 0, 0))
    return pl.pallas_call(
        _s5_scan_kernel,
        out_shape=jax.ShapeDtypeStruct(s_lat.shape, F32),
        grid=(SSM_NQ,),
        in_specs=[blk(s_ctx.shape[1]), blk(s_lat.shape[1]), blk(SSM_ROWS)],
        out_specs=blk(s_lat.shape[1]),
        compiler_params=_cparams("arbitrary"),
        name="s5_chunk_scan",
    )(s_ctx, s_lat, a_rows)


def _s5_out_kernel(x_ref, mod_ref, s_ref, bd_ref, wcc_ref, d_ref, o_ref, toep_ref, wc_ref, u_ref, y_ref, *,
                   bsz, nchunk):
    c = SSM_CHUNK

    @pl.when(pl.program_id(1) == 0)
    def _():
        _s5_expand(wc_ref, wcc_ref)
        for t in range(c):
            for k in range(c):
                toep_ref[t * LANES:(t + 1) * LANES, k * LANES:(k + 1) * LANES] = (
                    bd_ref[0, 0, t - k] if t >= k else bd_ref[0, 1, k - t])

    lhs_s = jnp.concatenate(
        [jnp.concatenate([s_ref[v, pl.ds(d * bsz + b, nchunk, stride=SSM_ROWS), :]
                          for d in range(2) for v in range(SSM_VR)], axis=1).astype(BF16)
         for b in range(bsz)], axis=0)
    _s5_modulate(x_ref, mod_ref, u_ref, bsz)
    y = _dot_nt(_s5_gather(u_ref, bsz, nchunk), toep_ref[...]) + _dot_nt(lhs_s, wc_ref[...])
    for b in range(bsz):
        for t in range(c):
            y_ref[b, pl.ds(t, nchunk, stride=c), :] = y[b * nchunk:(b + 1) * nchunk, t * LANES:(t + 1) * LANES]
    for b in range(bsz):
        o_ref[b] = y_ref[b] + d_ref[...] * u_ref[b]


def _s5_out(x, mods, s_in, bd, wcc, d_row, *, t):
    bsz, n, d = x.shape
    nchunk = t // SSM_CHUNK
    tile = pl.BlockSpec((bsz, t, LANES), lambda q, i: (0, i, q))
    return pl.pallas_call(
        functools.partial(_s5_out_kernel, bsz=bsz, nchunk=nchunk),
        out_shape=jax.ShapeDtypeStruct((bsz, n, d), F32),
        grid=(SSM_NQ, n // t),
        in_specs=[tile,
                  pl.BlockSpec((bsz, N_MODS, LANES), lambda q, i: (0, 0, q)),
                  pl.BlockSpec((SSM_VR, nchunk * SSM_ROWS, LANES), lambda q, i: (q, i, 0)),
                  pl.BlockSpec((1, 2, SSM_CHUNK, LANES, LANES), lambda q, i: (q, 0, 0, 0, 0)),
                  pl.BlockSpec((SSM_QG, SSM_TAP, 4 * LANES), lambda q, i: (q, 0, 0)),
                  pl.BlockSpec((1, LANES), lambda q, i: (0, q))],
        out_specs=tile,
        scratch_shapes=[pltpu.VMEM((SSM_W, SSM_W), BF16), pltpu.VMEM((SSM_W, 4 * SSM_HALF), BF16),
                        pltpu.VMEM((bsz, t, LANES), F32), pltpu.VMEM((bsz, t, LANES), F32)],
        compiler_params=_cparams("arbitrary", "arbitrary"),
        name="s5_chunk_out",
    )(x, mods, s_in, bd, wcc, d_row)


def _s5_mixer(x_lat, x_ctx, mods_lat, mods_ctx, lam_re, lam_im, log_dt, b_re, b_im, c_re, c_im, d_skip):
    bsz, n, _ = x_lat.shape
    assert 2 * bsz == SSM_ROWS, "state rows are (direction, batch) on the 8 sublanes"
    wbc, wcc, bd, a = _s5_prep(lam_re, lam_im, log_dt, b_re, b_im, c_re, c_im)
    t_lat = _tile(n, 1024)
    s_lat = _s5_states(x_lat, mods_lat, wbc, t=t_lat)
    s_ctx = _s5_states(x_ctx, mods_ctx, wbc, t=x_ctx.shape[1])
    a4 = a.reshape(SSM_NQ, SSM_QG, 2, 2, LANES)[..., :SSM_STATE]
    a_rows = jnp.repeat(a4.transpose(2, 0, 3, 1, 4).reshape(2, -1), bsz, axis=0)
    a_rows = a_rows.reshape(SSM_ROWS, SSM_NQ * SSM_VR, LANES).transpose(1, 0, 2)
    s_in = _s5_scan(s_ctx, s_lat, a_rows)
    return _s5_out(x_lat, mods_lat, s_in, bd, wcc, d_skip.reshape(1, -1), t=t_lat)


def _gmlp_kernel(x_ref, mod_ref, win_ref, bin_ref, lg_ref, lb_ref, ws_ref, bs_ref, wout_ref, g_ref, b_ref, o_ref,
                 v_ref, gated_ref):
    m = mod_ref[0]
    x = x_ref[0]
    t = x.shape[0]
    hw = GMLP_HEAD_DIM
    h = (x * (1.0 + m[1:2]) + m[0:1]).astype(BF16)

    def z_cols(c0):
        return jax.nn.gelu(_dot(h, win_ref[:, c0:c0 + hw]) + bin_ref[:, c0:c0 + hw])

    total = jnp.zeros((t, 1), F32)
    for hd in range(GMLP_HEADS):
        zc = z_cols(GMLP_HALF + hd * hw)
        v_ref[:, hd * hw:(hd + 1) * hw] = zc
        total = total + jnp.sum(zc, axis=-1, keepdims=True)
    mu = total / GMLP_HALF
    u_next = z_cols(0)
    sq = jnp.zeros((t, 1), F32)
    for hd in range(GMLP_HEADS):
        vc = v_ref[:, hd * hw:(hd + 1) * hw] - mu
        sq = sq + jnp.sum(vc * vc, axis=-1, keepdims=True)
    rstd = lax.rsqrt(sq / GMLP_HALF + LN_EPS)
    bs = bs_ref[...]
    for hd in range(GMLP_HEADS):
        cols = slice(hd * hw, (hd + 1) * hw)
        u = u_next
        if hd + 1 < GMLP_HEADS:
            u_next = z_cols((hd + 1) * hw)
        vn = ((v_ref[:, cols] - mu) * rstd * lg_ref[:, cols] + lb_ref[:, cols]).astype(BF16)
        for c in range(t // GMLP_CHUNK):
            rows = slice(c * GMLP_CHUNK, (c + 1) * GMLP_CHUNK)
            gate = _dot(ws_ref[hd], vn[rows]) + bs[:, hd:hd + 1]
            gated_ref[rows, cols] = (u[rows] * gate).astype(BF16)
    chunks = _row_chunks(t)
    ys = [_dot(gated_ref[rs, :], wout_ref[...]) for rs in chunks]
    for rs, y in zip(chunks, ys):
        o_ref[0, rs, :] = _post_norm(x_ref[0, rs, :], y, m[2:3], g_ref[...], b_ref[...])


def _gmlp(x, mods, w_in, b_in, lg, lb, w_s, b_s_t, w_out, ln_g, ln_b, *, t):
    bsz, n, d = x.shape
    tile = pl.BlockSpec((1, t, d), lambda b, i: (b, i, 0))
    const2 = lambda a: _const_spec(a.shape)
    return pl.pallas_call(
        _gmlp_kernel,
        out_shape=jax.ShapeDtypeStruct((bsz, n, d), F32),
        grid=(bsz, n // t),
        in_specs=[tile, pl.BlockSpec((1, N_MODS, d), lambda b, i: (b, 0, 0)),
                  const2(w_in), const2(b_in), const2(lg), const2(lb), const2(w_s), const2(b_s_t), const2(w_out),
                  const2(ln_g), const2(ln_b)],
        out_specs=tile,
        scratch_shapes=[pltpu.VMEM((t, GMLP_HALF), F32), pltpu.VMEM((t, GMLP_HALF), BF16)],
        compiler_params=_cparams("arbitrary", "arbitrary"),
        name="gmlp_mixer",
    )(x, mods, w_in, b_in, lg, lb, w_s, b_s_t, w_out, ln_g, ln_b)


def _tile(n, pref):
    return pref if n % pref == 0 else n


def _layer(layer, x_lat, x_ctx, mods, p):
    bsz, _, d = x_lat.shape
    lctx = x_ctx.shape[1]
    row = lambda v: v.reshape(1, -1)
    t_lat, t_ctx = 1024, _tile(lctx, 256)
    kind = MIXERS[layer % len(MIXERS)]
    j = layer // len(MIXERS)
    ctx_out = any(MIXERS[m % len(MIXERS)] in CTX_READING_MIXERS for m in range(layer + 1, DEPTH))
    m_lat = mods[layer, :bsz]
    m_ctx = jnp.broadcast_to(mods[layer, bsz:bsz + 1], (bsz, N_MODS, d))
    g1, b1 = row(p["ln1_g"][layer]), row(p["ln1_b"][layer])

    def ffn(xs, ms, t, seg=None):
        return _ffn(xs, ms, p["ffn_w_up_bf16"], p["ffn_conv_w"][layer], row(p["ffn_conv_b"][layer]),
                    p["ffn_w_down_bf16"], row(p["ln2_g"][layer]), row(p["ln2_b"][layer]),
                    layer=layer, t=t, cc=FFN_CHUNK, seg=seg)

    if kind == "pool":
        args = (p["pool_w"][j].astype(BF16), row(p["pool_b"][j]), row(p["pool_scale"][j]), g1, b1)
        x_lat = _pool(x_lat, m_lat, *args, t=t_lat)
        if ctx_out:
            x_ctx = _pool(x_ctx, m_ctx, *args, t=t_ctx)
    elif kind == "attn":
        wqkv = p["attn_w_qkv"][j].astype(BF16)
        wo = p["attn_w_o"][j].astype(BF16)
        sink = p["attn_sink"][j].astype(F32)
        q, kd, vd = _qkv(x_lat, m_lat, wqkv, t=t_lat, rope=True)
        qc, kdc, vdc = _qkv(x_ctx, m_ctx, wqkv, t=t_ctx, rope=False)
        o_lat = _attn(sink, q, kd, vd, kdc, vdc, local=True, qb=ATTN_QBLOCKS)
        x_lat = _proj_norm(o_lat, x_lat, m_lat, wo, g1, b1, t=t_lat)
        if ctx_out:
            o_ctx = _attn(sink, qc, None, None, kdc, vdc, local=False, qb=ATTN_QBLOCKS)
            x_ctx = _proj_norm(o_ctx, x_ctx, m_ctx, wo, g1, b1, t=t_ctx)
    elif kind == "ssm":
        assert not ctx_out
        y = _s5_mixer(x_lat, x_ctx, m_lat, m_ctx, p["ssm_lambda_re"][j], p["ssm_lambda_im"][j], p["ssm_log_dt"][j],
                       p["ssm_b_re"][j], p["ssm_b_im"][j], p["ssm_c_re"][j], p["ssm_c_im"][j], p["ssm_d"][j])
        x_lat = _glu_norm(y, x_lat, m_lat, p["ssm_w_glu_a"][j].astype(BF16), p["ssm_w_glu_b"][j].astype(BF16),
                          g1, b1, t=t_lat)
    else:
        assert not ctx_out
        x_lat = _gmlp(x_lat, m_lat, p["gmlp_w_in"][j].astype(BF16), row(p["gmlp_b_in"][j]), row(p["gmlp_ln_g"][j]),
                      row(p["gmlp_ln_b"][j]), p["gmlp_w_s"][j].astype(BF16), p["gmlp_b_s"][j].T,
                      p["gmlp_w_out"][j].astype(BF16), g1, b1, t=t_lat)
    x_lat = ffn(x_lat, m_lat, _tile(x_lat.shape[1], 1024))
    if ctx_out:
        x_ctx = ffn(x_ctx.reshape(1, bsz * lctx, d), m_ctx[:1], bsz * lctx, seg=lctx).reshape(bsz, lctx, d)
    return x_lat, x_ctx


def _mods(c, c_ctx, ada_w, ada_b):
    bsz, d = c.shape
    cond = jnp.concatenate([c, c_ctx[None, :], jnp.zeros((8 - bsz - 1, d), F32)], axis=0)
    return _ada(cond, ada_w, ada_b).reshape(DEPTH, 8, N_MODS, d)


def kernel(x, c, ctx, c_ctx, ada_w, ada_b, ln1_g, ln1_b, ln2_g, ln2_b, ffn_w_up, ffn_conv_w, ffn_conv_b, ffn_w_down, pool_w, pool_b, pool_scale, attn_w_qkv, attn_w_o, attn_sink, ssm_lambda_re, ssm_lambda_im, ssm_log_dt, ssm_b_re, ssm_b_im, ssm_c_re, ssm_c_im, ssm_d, ssm_w_glu_a, ssm_w_glu_b, gmlp_w_in, gmlp_b_in, gmlp_ln_g, gmlp_ln_b, gmlp_w_s, gmlp_b_s, gmlp_w_out):
    bsz, n, d = x.shape
    assert d == D_MODEL and bsz < 8 and n % 512 == 0 and ctx.shape[1] % ATTN_BLOCK == 0
    p = dict(ln1_g=ln1_g, ln1_b=ln1_b, ln2_g=ln2_g, ln2_b=ln2_b, ffn_w_up=ffn_w_up, ffn_conv_w=ffn_conv_w,
             ffn_conv_b=ffn_conv_b, ffn_w_down=ffn_w_down, pool_w=pool_w, pool_b=pool_b, pool_scale=pool_scale,
             attn_w_qkv=attn_w_qkv, attn_w_o=attn_w_o, attn_sink=attn_sink, ssm_lambda_re=ssm_lambda_re,
             ssm_lambda_im=ssm_lambda_im, ssm_log_dt=ssm_log_dt, ssm_b_re=ssm_b_re, ssm_b_im=ssm_b_im,
             ssm_c_re=ssm_c_re, ssm_c_im=ssm_c_im, ssm_d=ssm_d, ssm_w_glu_a=ssm_w_glu_a, ssm_w_glu_b=ssm_w_glu_b,
             gmlp_w_in=gmlp_w_in, gmlp_b_in=gmlp_b_in, gmlp_ln_g=gmlp_ln_g, gmlp_ln_b=gmlp_ln_b, gmlp_w_s=gmlp_w_s,
             gmlp_b_s=gmlp_b_s, gmlp_w_out=gmlp_w_out)
    p["ffn_w_up_bf16"] = ffn_w_up.astype(BF16)
    p["ffn_w_down_bf16"] = ffn_w_down.astype(BF16)
    mods = _mods(c, c_ctx, ada_w, ada_b)
    x_lat, x_ctx = x, ctx
    for layer in range(DEPTH):
        x_lat, x_ctx = _layer(layer, x_lat, x_ctx, mods, p)
    return x_lat
```
